```python
import math
import jax, jax.numpy as jnp
from jax import lax
import numpy as np

D_MODEL = 1024
BATCH = 4
SEQ = 8192
DEPTH = 1

D_MIX = D_MODEL
GLA_WIDTH = D_MIX // 2
GLA_HEADS = 4
GLA_DK = GLA_WIDTH // 2 // GLA_HEADS
GLA_DV = GLA_WIDTH // GLA_HEADS
GLA_RANK = 16
GLA_TAU = 16.0
GLA_CHUNK = 64
DSA_WIDTH = D_MIX - GLA_WIDTH
DSA_HEADS = 8
DSA_DH = DSA_WIDTH // DSA_HEADS
DSA_PATTERN = ((128, 1), (512, 4), (2048, 16))
DSA_BLOCK = 128
REL_BUCKETS = 32
REL_MAX_DIST = 2048
D_FF = 4 * D_MODEL
EPS = 1e-6
NEG = -1e30

IN_SPLITS = (
    GLA_HEADS * GLA_DK,
    GLA_HEADS * GLA_DK,
    GLA_WIDTH,
    GLA_WIDTH,
    GLA_RANK,
    DSA_WIDTH,
    DSA_WIDTH,
    DSA_WIDTH,
)
D_IN = sum(IN_SPLITS)

kernel_name = "hymba_gla_dilated_swa_block"


def rmsnorm(x, g):
    xf = x.astype(jnp.float32)
    y = xf * lax.rsqrt(jnp.mean(xf * xf, axis=-1, keepdims=True) + EPS)
    return (y * g.astype(jnp.float32)).astype(x.dtype)


def t5_bucket(dist):
    max_exact = REL_BUCKETS // 2
    n = np.maximum(dist, 0)
    large = max_exact + (np.log(np.maximum(n, 1) / max_exact)
                         / math.log(REL_MAX_DIST / max_exact)
                         * (REL_BUCKETS - max_exact)).astype(np.int32)
    large = np.minimum(large, REL_BUCKETS - 1)
    return np.where(n < max_exact, n, large).astype(np.int32)


def gla_mixer(q, k, v, glog):
    B, S, H, dk = q.shape
    dv = v.shape[-1]
    C = GLA_CHUNK
    n = S // C
    f32 = jnp.float32
    q = (q.astype(f32) * dk ** -0.5).reshape(B, n, C, H, dk)
    k = k.astype(f32).reshape(B, n, C, H, dk)
    v = v.astype(f32).reshape(B, n, C, H, dv)
    b = jnp.cumsum(glog.astype(f32).reshape(B, n, C, H, dk), axis=2)
    b_last = b[:, :, -1]
    q_dec = q * jnp.exp(b)
    k_inv = k * jnp.exp(-b)
    k_end = k * jnp.exp(b_last[:, :, None] - b)
    causal = jnp.tril(jnp.ones((C, C), dtype=bool))
    att = jnp.einsum('bnihk,bnjhk->bnhij', q_dec, k_inv)
    att = jnp.where(causal, att, 0.0)
    o_intra = jnp.einsum('bnhij,bnjhv->bnihv', att, v)
    inc = jnp.einsum('bnjhk,bnjhv->bnhkv', k_end, v)

    def step(state, inp):
        decay, upd = inp
        return decay[..., None] * state + upd, state

    _, s_prev = lax.scan(step, jnp.zeros((B, H, dk, dv), f32),
                         (jnp.exp(b_last).swapaxes(0, 1), inc.swapaxes(0, 1)))
    s_prev = s_prev.swapaxes(0, 1)
    o_inter = jnp.einsum('bnihk,bnhkv->bnihv', q_dec, s_prev)
    return (o_intra + o_inter).reshape(B, S, H, dv)


def dilated_branch(q, k, v, rel_bias, window, dilation):
    B, S, H, Dh = q.shape
    L = DSA_BLOCK
    span = window // dilation
    unit = dilation * L
    Sp = -(-S // unit) * unit
    n = Sp // dilation
    nb = n // L
    f32 = jnp.float32

    def to_sub(t):
        t = jnp.pad(t, ((0, 0), (0, Sp - S), (0, 0), (0, 0)))
        return t.reshape(B, n, dilation, H, Dh).transpose(0, 2, 3, 1, 4)

    def band(t):
        tb = jnp.pad(t, ((0, 0), (0, 0), (0, 0), (L, 0), (0, 0))).reshape(B, dilation, H, nb + 1, L, Dh)
        return jnp.concatenate([tb[:, :, :, :-1], tb[:, :, :, 1:]], axis=4)

    qb = to_sub(q).reshape(B, dilation, H, nb, L, Dh)
    kb = band(to_sub(k))
    vb = band(to_sub(v))

    steps = L + np.arange(L)[:, None] - np.arange(2 * L)[None, :]
    in_band = (steps >= 0) & (steps <= span)
    key_idx = np.arange(nb)[:, None, None] * L + np.arange(2 * L)[None, None, :] - L
    mask = jnp.asarray(in_band[None] & (key_idx >= 0))
    bias = jnp.transpose(rel_bias[t5_bucket(steps * dilation)], (2, 0, 1)).astype(f32)

    s = jnp.einsum('bdhcqe,bdhcke->bdhcqk', qb, kb).astype(f32) + bias[:, None]
    s = jnp.where(mask, s, NEG)
    m = jnp.max(s, axis=-1, keepdims=True)
    p = jnp.exp(s - m)
    den = jnp.sum(p, axis=-1, keepdims=True)
    o = jnp.einsum('bdhcqk,bdhcke->bdhcqe', p, vb.astype(f32)) / den
    lse = (m + jnp.log(den))[..., 0]
    o = o.reshape(B, dilation, H, n, Dh).transpose(0, 3, 1, 2, 4).reshape(B, Sp, H, Dh)[:, :S]
    lse = lse.reshape(B, dilation, H, n).transpose(0, 3, 1, 2).reshape(B, Sp, H)[:, :S]
    return o, lse


def dilated_mixer(q, k, v, rel_bias):
    q = q * DSA_DH ** -0.5
    outs, lses = [], []
    for window, dilation in DSA_PATTERN:
        o, lse = dilated_branch(q, k, v, rel_bias, window, dilation)
        outs.append(o)
        lses.append(lse)
    w = jax.nn.softmax(jnp.stack(lses, axis=0), axis=0)
    return jnp.sum(w[..., None] * jnp.stack(outs, axis=0), axis=0)


def setup_inputs(seed: int = 0) -> dict:
    key = jax.random.key(seed)
    ks = jax.random.split(key, 16)
    f32 = jnp.float32
    nrm = lambda k, shape, scale: jax.random.normal(k, shape, f32) * scale
    return {
        "x": jax.random.normal(ks[0], (BATCH, SEQ, D_MODEL), f32),
        "attn_norm_g": 1.0 + nrm(ks[1], (DEPTH, D_MODEL), 0.02),
        "w_in": nrm(ks[2], (DEPTH, D_MODEL, D_IN), D_MODEL ** -0.5),
        "gla_gate_w2": nrm(ks[3], (DEPTH, GLA_RANK, GLA_HEADS * GLA_DK), GLA_RANK ** -0.5),
        "gla_gate_b": nrm(ks[4], (DEPTH, GLA_HEADS * GLA_DK), 0.1),
        "gla_norm_g": 1.0 + nrm(ks[5], (DEPTH, GLA_WIDTH), 0.02),
        "rel_bias": nrm(ks[6], (REL_BUCKETS, DSA_HEADS), 0.1),
        "w_out": nrm(ks[7], (DEPTH, D_MIX, D_MODEL), D_MIX ** -0.5),
        "mlp_norm_g": 1.0 + nrm(ks[8], (DEPTH, D_MODEL), 0.02),
        "w_ff1": nrm(ks[9], (DEPTH, D_MODEL, D_FF), D_MODEL ** -0.5),
        "w_ff2": nrm(ks[10], (DEPTH, D_FF, D_MODEL), D_FF ** -0.5),
        "final_norm_g": 1.0 + nrm(ks[11], (D_MODEL,), 0.02),
    }


def reference(x, attn_norm_g, w_in, gla_gate_w2, gla_gate_b, gla_norm_g, rel_bias,
              w_out, mlp_norm_g, w_ff1, w_ff2, final_norm_g):
    B, S, _ = x.shape
    split_at = [int(i) for i in np.cumsum(IN_SPLITS)[:-1]]
    h = x
    for l in range(DEPTH):
        nx = rmsnorm(h, attn_norm_g[l])
        proj = jnp.einsum('bsd,dp->bsp', nx, w_in[l])
        gq, gk, gv, gr, glow, dq, dk_, dv_ = jnp.split(proj, split_at, axis=-1)

        gate_pre = (jnp.einsum('bsr,rk->bsk', glow, gla_gate_w2[l]) + gla_gate_b[l]).astype(jnp.float32)
        glog = jax.nn.log_sigmoid(gate_pre) / GLA_TAU
        o_a = gla_mixer(gq.reshape(B, S, GLA_HEADS, GLA_DK),
                        gk.reshape(B, S, GLA_HEADS, GLA_DK),
                        gv.reshape(B, S, GLA_HEADS, GLA_DV),
                        glog.reshape(B, S, GLA_HEADS, GLA_DK))
        o_a = o_a * lax.rsqrt(jnp.mean(o_a * o_a, axis=-1, keepdims=True) + EPS)
        o_a = o_a * gla_norm_g[l].astype(jnp.float32).reshape(GLA_HEADS, GLA_DV)
        o_a = o_a.reshape(B, S, GLA_WIDTH) * jax.nn.silu(gr.astype(jnp.float32))

        o_b = dilated_mixer(dq.reshape(B, S, DSA_HEADS, DSA_DH),
                            dk_.reshape(B, S, DSA_HEADS, DSA_DH),
                            dv_.reshape(B, S, DSA_HEADS, DSA_DH),
                            rel_bias).reshape(B, S, DSA_WIDTH)

        mixed = jnp.concatenate([o_a, o_b], axis=-1).astype(h.dtype)
        h = h + jnp.einsum('bsm,md->bsd', mixed, w_out[l])

        nm = rmsnorm(h, mlp_norm_g[l])
        a = jnp.square(jax.nn.relu(jnp.einsum('bsd,df->bsf', nm, w_ff1[l])))
        h = h + jnp.einsum('bsf,fd->bsd', a, w_ff2[l])
    return rmsnorm(h, final_norm_g)
```

```python
import functools
import math

import numpy as np
import jax
import jax.numpy as jnp
from jax import lax
from jax.experimental import pallas as pl
from jax.experimental.pallas import tpu as pltpu

D_MODEL = 1024
GLA_WIDTH = 512
GLA_HEADS = 4
GLA_DK = 64
GLA_DV = 128
GLA_QK = GLA_HEADS * GLA_DK
GLA_RANK = 16
GLA_TAU = 16.0
GLA_CHUNK = 64
DSA_WIDTH = 512
DSA_HEADS = 8
DSA_DH = 64
DSA_PATTERN = ((128, 1), (512, 4), (2048, 16))
DSA_BLOCK = 128
REL_BUCKETS = 32
REL_MAX_DIST = 2048
D_FF = 4096
EPS = 1e-6
NEG = -1e30

LANES = 128
RANK_PAD = LANES
VMEM_LIMIT = 56 * 1024 * 1024

F32 = jnp.float32
BF16 = jnp.bfloat16

_NT = (((1,), (1,)), ((), ()))


def _dot(a, b):
    return jnp.dot(a, b, preferred_element_type=F32)


def _dot_nt(a, b):
    return lax.dot_general(a, b, _NT, preferred_element_type=F32)


def _split_hi_lo(x):
    hi = x.astype(BF16)
    lo = (x - hi.astype(F32)).astype(BF16)
    return hi, lo


def _log_sigmoid(x):
    return jnp.minimum(x, 0.0) - jnp.log1p(jnp.exp(-jnp.abs(x)))


def _bucket_tables():
    max_exact = REL_BUCKETS // 2
    L = DSA_BLOCK
    steps = L + np.arange(L)[:, None] - np.arange(2 * L)[None, :]
    tables = []
    for window, dilation in DSA_PATTERN:
        span = window // dilation
        in_band = (steps >= 0) & (steps <= span)
        n = np.maximum(steps * dilation, 0)
        large = max_exact + (np.log(np.maximum(n, 1) / max_exact)
                             / math.log(REL_MAX_DIST / max_exact)
                             * (REL_BUCKETS - max_exact)).astype(np.int32)
        large = np.minimum(large, REL_BUCKETS - 1)
        bucket = np.where(n < max_exact, n, large).astype(np.int32)
        tables.append(np.where(in_band, bucket, -1).astype(np.int32))
    return np.stack(tables)


def _bias_kernel(rb_ref, bt_ref, out_ref):
    h = pl.program_id(1)
    bt = bt_ref[0]
    acc = jnp.full(bt.shape, NEG, F32)
    for b in range(REL_BUCKETS):
        acc = jnp.where(bt == b, rb_ref[b, h], acc)
    out_ref[0, 0] = acc


def _bias_tables(rel_bias):
    nb = len(DSA_PATTERN)
    L = DSA_BLOCK
    return pl.pallas_call(
        _bias_kernel,
        grid=(nb, DSA_HEADS),
        in_specs=[pl.BlockSpec(memory_space=pltpu.SMEM),
                  pl.BlockSpec((1, L, 2 * L), lambda d, h: (d, 0, 0))],
        out_specs=pl.BlockSpec((1, 1, L, 2 * L), lambda d, h: (d, h, 0, 0)),
        out_shape=jax.ShapeDtypeStruct((nb, DSA_HEADS, L, 2 * L), F32),
        name="dsa_bias",
    )(rel_bias.astype(F32), jnp.asarray(_bucket_tables()))


def _proj_kernel(x_ref, g_ref, wq_ref, wv_ref, wr_ref, wkt_ref, wg_ref, wgt_ref, w2_ref, w2t_ref,
                 b_ref, bt_ref, wd_ref,
                 gq_ref, gv_ref, gr_ref, kt_ref, glog_ref, glogt_ref, d_ref):
    x = x_ref[...]
    ms = jnp.mean(x * x, axis=-1, keepdims=True)
    nx = (x * lax.rsqrt(ms + EPS) * g_ref[...]).astype(BF16)

    gq_ref[...] = _dot(nx, wq_ref[...]).astype(BF16)
    gv_ref[...] = _dot(nx, wv_ref[...]).astype(BF16)
    gr_ref[...] = _dot(nx, wr_ref[...]).astype(BF16)
    kt_ref[0] = _dot_nt(wkt_ref[...], nx).astype(BF16)

    glow = _dot(nx, wg_ref[...]).astype(BF16)
    pre = _dot(glow, w2_ref[...]) + b_ref[...]
    glog_ref[...] = _log_sigmoid(pre) * (1.0 / GLA_TAU)
    glow_t = _dot_nt(wgt_ref[...], nx).astype(BF16)
    pre_t = _dot(w2t_ref[...], glow_t) + bt_ref[...]
    glogt_ref[0] = _log_sigmoid(pre_t) * (1.0 / GLA_TAU)

    d_ref[...] = _dot(nx, wd_ref[...]).astype(BF16)


def _proj(x2, g, wq, wv, wr, wkt, wg, wgt, w2, w2t, b, bt, wd, *, batch, seq, tm):
    T = batch * seq
    spb = seq // tm
    full = lambda a: pl.BlockSpec(a.shape, lambda i: (0,) * a.ndim)
    row = lambda n: pl.BlockSpec((tm, n), lambda i: (i, 0))
    colt = pl.BlockSpec((1, GLA_QK, tm), lambda i: (i // spb, 0, i % spb))
    return pl.pallas_call(
        _proj_kernel,
        grid=(T // tm,),
        in_specs=[row(D_MODEL)] + [full(a) for a in (g, wq, wv, wr, wkt, wg, wgt, w2, w2t, b, bt, wd)],
        out_specs=[row(GLA_QK), row(GLA_WIDTH), row(GLA_WIDTH), colt, row(GLA_QK), colt, row(3 * DSA_WIDTH)],
        out_shape=[jax.ShapeDtypeStruct((T, GLA_QK), BF16),
                   jax.ShapeDtypeStruct((T, GLA_WIDTH), BF16),
                   jax.ShapeDtypeStruct((T, GLA_WIDTH), BF16),
                   jax.ShapeDtypeStruct((batch, GLA_QK, seq), BF16),
                   jax.ShapeDtypeStruct((T, GLA_QK), F32),
                   jax.ShapeDtypeStruct((batch, GLA_QK, seq), F32),
                   jax.ShapeDtypeStruct((T, 3 * DSA_WIDTH), BF16)],
        compiler_params=pltpu.CompilerParams(dimension_semantics=("arbitrary",),
                                             vmem_limit_bytes=VMEM_LIMIT),
        name="proj",
    )(x2, g, wq, wv, wr, wkt, wg, wgt, w2, w2t, b, bt, wd)


def _gla_kernel(gq_ref, kt_ref, gv_ref, gr_ref, glog_ref, glogt_ref, gn_ref, o_ref, s_ref, *, pairs):
    C = GLA_CHUNK
    P = 2 * C

    @pl.when(pl.program_id(1) == 0)
    def _():
        s_ref[...] = jnp.zeros_like(s_ref)

    ri = lax.broadcasted_iota(jnp.int32, (P, P), 0)
    ci = lax.broadcasted_iota(jnp.int32, (P, P), 1)
    same_chunk = (ri < C) == (ci < C)
    causal = same_chunk & (ci <= ri)
    low = jnp.where(causal, 1.0, 0.0).astype(BF16)
    upp = jnp.where(same_chunk & (ri <= ci), 1.0, 0.0).astype(BF16)
    lane = lax.broadcasted_iota(jnp.int32, (P, LANES), 1)
    first_half = lane < C
    lane_t = lax.broadcasted_iota(jnp.int32, (GLA_QK, P), 1)
    first_t = lane_t < C

    for p in range(pairs):
        rows = slice(p * P, (p + 1) * P)
        g_hi, g_lo = _split_hi_lo(glog_ref[rows, :])
        b = _dot(low, g_hi) + _dot(low, g_lo)
        qd = (gq_ref[rows, :].astype(F32) * jnp.exp(b)).astype(BF16)

        gt = glogt_ref[0, :, rows]
        gt_hi, gt_lo = _split_hi_lo(gt)
        bt = _dot(gt_hi, upp) + _dot(gt_lo, upp)
        tot_a = jnp.sum(jnp.where(first_t, gt, 0.0), axis=-1, keepdims=True)
        tot_b = jnp.sum(jnp.where(first_t, 0.0, gt), axis=-1, keepdims=True)
        tot = jnp.where(first_t, tot_a, tot_b)
        kt = kt_ref[0, :, rows].astype(F32)
        kinv_t = (kt * jnp.exp(-bt)).astype(BF16)
        kend_t = kt * jnp.exp(tot - bt)
        kend_a = jnp.where(first_t, kend_t, 0.0).astype(BF16)
        kend_b = jnp.where(first_t, 0.0, kend_t).astype(BF16)
        dec_a = jnp.exp(tot_a)
        dec_b = jnp.exp(tot_b)

        for h in range(GLA_HEADS):
            hp = h // 2
            grp = slice(hp * LANES, (hp + 1) * LANES)
            own = first_half if h % 2 == 0 else ~first_half
            qm = jnp.where(own, qd[:, grp], jnp.zeros((), BF16))
            hk = slice(h * GLA_DK, (h + 1) * GLA_DK)
            hv = slice(h * GLA_DV, (h + 1) * GLA_DV)
            v = gv_ref[rows, hv]

            att = _dot(qm, kinv_t[grp, :])
            att = jnp.where(causal, att, 0.0).astype(BF16)
            o = _dot(att, v)

            s_pair = s_ref[grp, :]
            s_a = s_ref[hk, :]
            s_b = dec_a[hk, :] * s_a + _dot(kend_a[hk, :], v)
            s_ref[hk, :] = dec_b[hk, :] * s_b + _dot(kend_b[hk, :], v)
            if h % 2 == 0:
                s_pair_b = jnp.concatenate([s_b, s_pair[GLA_DK:, :]], axis=0)
            else:
                s_pair_b = jnp.concatenate([s_pair[:GLA_DK, :], s_b], axis=0)
            o_a = _dot(qm[:C, :], s_pair.astype(BF16))
            o_b = _dot(qm[C:, :], s_pair_b.astype(BF16))
            o = o + jnp.concatenate([o_a, o_b], axis=0)

            o = o * lax.rsqrt(jnp.mean(o * o, axis=-1, keepdims=True) + EPS) * gn_ref[:, hv]
            r = gr_ref[rows, hv].astype(F32)
            o_ref[rows, hv] = (o * (r * jax.nn.sigmoid(r))).astype(BF16)


def _gla(gq, kt, gv, gr, glog, glogt, gn, *, batch, seq, tg):
    T = batch * seq
    spb = seq // tg
    row = lambda n: pl.BlockSpec((tg, n), lambda b, i: (b * spb + i, 0))
    colt = pl.BlockSpec((1, GLA_QK, tg), lambda b, i: (b, 0, i))
    return pl.pallas_call(
        functools.partial(_gla_kernel, pairs=tg // (2 * GLA_CHUNK)),
        grid=(batch, spb),
        in_specs=[row(GLA_QK), colt, row(GLA_WIDTH), row(GLA_WIDTH), row(GLA_QK), colt,
                  pl.BlockSpec((1, GLA_WIDTH), lambda b, i: (0, 0))],
        out_specs=row(GLA_WIDTH),
        out_shape=jax.ShapeDtypeStruct((T, GLA_WIDTH), BF16),
        scratch_shapes=[pltpu.VMEM((GLA_QK, GLA_DV), F32)],
        compiler_params=pltpu.CompilerParams(dimension_semantics=("arbitrary", "arbitrary"),
                                             vmem_limit_bytes=VMEM_LIMIT),
        name="gla",
    )(gq, kt, gv, gr, glog, glogt, gn)


def _dsa_kernel(q_ref, k_ref, v_ref, kh_ref, vh_ref, bias_ref, o_ref, lse_ref, *, blocks):
    L = DSA_BLOCK
    first_tile = pl.program_id(2) == 0
    lane = lax.broadcasted_iota(jnp.int32, (L, LANES), 1)
    first_half = lane < DSA_DH
    prev_cols = lax.broadcasted_iota(jnp.int32, (L, 2 * L), 1) < L

    for blk in range(blocks):
        rows = slice(blk * L, (blk + 1) * L)
        lse_tile = jnp.zeros((L, LANES), F32)
        for hp in range(DSA_HEADS // 2):
            grp = slice(hp * LANES, (hp + 1) * LANES)
            qp = q_ref[0, rows, grp]
            if blk == 0:
                kcat = jnp.concatenate([kh_ref[0, :, grp], k_ref[0, rows, grp]], axis=0)
                vcat = jnp.concatenate([vh_ref[0, :, grp], v_ref[0, rows, grp]], axis=0)
            else:
                both = slice((blk - 1) * L, (blk + 1) * L)
                kcat = k_ref[0, both, grp]
                vcat = v_ref[0, both, grp]
            outs = []
            for hh in range(2):
                h = 2 * hp + hh
                own = first_half if hh == 0 else ~first_half
                qm = jnp.where(own, qp, jnp.zeros((), BF16))
                bias = bias_ref[0, h]
                if blk == 0:
                    bias = jnp.where(prev_cols & first_tile, NEG, bias)
                s = _dot_nt(qm, kcat) + bias
                m = jnp.max(s, axis=-1, keepdims=True)
                p = jnp.exp(s - m)
                den = jnp.sum(p, axis=-1, keepdims=True)
                pv = _dot(p.astype(BF16), vcat)
                outs.append(pv / den)
                lse_tile = jnp.where(lane == h, m + jnp.log(den), lse_tile)
            o_ref[0, rows, grp] = jnp.where(first_half, outs[0], outs[1]).astype(BF16)
        lse_ref[0, rows, :] = lse_tile


def _dsa_branch(dqkv, bias, branch, *, batch, seq, tr_max):
    _, d = DSA_PATTERN[branch]
    n = seq // d
    tr = min(tr_max, n)
    hb = tr // DSA_BLOCK
    W = DSA_WIDTH
    x = dqkv.reshape(batch, n, d * 3 * W)
    blk = lambda part: pl.BlockSpec((1, tr, W), lambda b, r, i: (b, i, 3 * r + part))
    halo = lambda part: pl.BlockSpec((1, DSA_BLOCK, W),
                                     lambda b, r, i: (b, jnp.maximum(i * hb - 1, 0), 3 * r + part))
    o, lse = pl.pallas_call(
        functools.partial(_dsa_kernel, blocks=hb),
        grid=(batch, d, n // tr),
        in_specs=[blk(0), blk(1), blk(2), halo(1), halo(2),
                  pl.BlockSpec((1, DSA_HEADS, DSA_BLOCK, 2 * DSA_BLOCK), lambda b, r, i: (branch, 0, 0, 0))],
        out_specs=[pl.BlockSpec((1, tr, W), lambda b, r, i: (b, i, r)),
                   pl.BlockSpec((1, tr, LANES), lambda b, r, i: (b, i, r))],
        out_shape=[jax.ShapeDtypeStruct((batch, n, d * W), BF16),
                   jax.ShapeDtypeStruct((batch, n, d * LANES), F32)],
        compiler_params=pltpu.CompilerParams(dimension_semantics=("arbitrary",) * 3,
                                             vmem_limit_bytes=VMEM_LIMIT),
        name=f"dsa_d{d}",
    )(x, x, x, x, x, bias)
    return o.reshape(batch * seq, W), lse.reshape(batch * seq, LANES)


def _out_kernel(x_ref, oa_ref, o1_ref, o2_ref, o3_ref, l1_ref, l2_ref, l3_ref,
                wo_ref, g2_ref, w1_ref, w2_ref, gf_ref, y_ref, *, ff_chunk):
    tm = x_ref.shape[0]
    lane = lax.broadcasted_iota(jnp.int32, (tm, LANES), 1)
    first_half = lane < DSA_DH

    l1, l2, l3 = l1_ref[...], l2_ref[...], l3_ref[...]
    m = jnp.maximum(jnp.maximum(l1, l2), l3)
    e1, e2, e3 = jnp.exp(l1 - m), jnp.exp(l2 - m), jnp.exp(l3 - m)
    inv = 1.0 / (e1 + e2 + e3)
    ws = (e1 * inv, e2 * inv, e3 * inv)
    o_refs = (o1_ref, o2_ref, o3_ref)

    ob = []
    for hp in range(DSA_HEADS // 2):
        grp = slice(hp * LANES, (hp + 1) * LANES)
        acc = jnp.zeros((tm, LANES), F32)
        for w, o_ref in zip(ws, o_refs):
            wa = jnp.sum(jnp.where(lane == 2 * hp, w, 0.0), axis=-1, keepdims=True)
            wb = jnp.sum(jnp.where(lane == 2 * hp + 1, w, 0.0), axis=-1, keepdims=True)
            acc = acc + jnp.where(first_half, wa, wb) * o_ref[:, grp].astype(F32)
        ob.append(acc.astype(BF16))
    mixed = _dot(oa_ref[...], wo_ref[:GLA_WIDTH, :]) + _dot(jnp.concatenate(ob, axis=1), wo_ref[GLA_WIDTH:, :])
    h = x_ref[...] + mixed

    nm = (h * lax.rsqrt(jnp.mean(h * h, axis=-1, keepdims=True) + EPS) * g2_ref[...]).astype(BF16)
    ff = None
    for c in range(D_FF // ff_chunk):
        cols = slice(c * ff_chunk, (c + 1) * ff_chunk)
        a = jnp.maximum(_dot(nm, w1_ref[:, cols]), 0.0)
        d = _dot((a * a).astype(BF16), w2_ref[cols, :])
        ff = d if ff is None else ff + d
    h = h + ff
    y_ref[...] = h * lax.rsqrt(jnp.mean(h * h, axis=-1, keepdims=True) + EPS) * gf_ref[...]


def _out(x2, oa, os_, ls, wo, g2, w1, w2, gf, *, tm, ff_chunk):
    T = x2.shape[0]
    row = lambda n: pl.BlockSpec((tm, n), lambda i: (i, 0))
    const = lambda a: pl.BlockSpec(a.shape, lambda i: (0,) * a.ndim, pipeline_mode=pl.Buffered(1))
    return pl.pallas_call(
        functools.partial(_out_kernel, ff_chunk=ff_chunk),
        grid=(T // tm,),
        in_specs=[row(D_MODEL), row(GLA_WIDTH)] + [row(DSA_WIDTH)] * 3 + [row(LANES)] * 3
                 + [const(wo), const(g2), const(w1), const(w2), const(gf)],
        out_specs=row(D_MODEL),
        out_shape=jax.ShapeDtypeStruct((T, D_MODEL), F32),
        compiler_params=pltpu.CompilerParams(dimension_semantics=("arbitrary",),
                                             vmem_limit_bytes=VMEM_LIMIT),
        name="out_mlp",
    )(x2, oa, *os_, *ls, wo, g2, w1, w2, gf)


def kernel(x, attn_norm_g, w_in, gla_gate_w2, gla_gate_b, gla_norm_g, rel_bias, w_out, mlp_norm_g,
           w_ff1, w_ff2, final_norm_g):
    batch, seq, _ = x.shape
    assert seq % (DSA_PATTERN[-1][1] * DSA_BLOCK) == 0
    T = batch * seq
    x2 = x.reshape(T, D_MODEL)

    w = w_in[0]
    splits = np.cumsum([GLA_QK, GLA_QK, GLA_WIDTH, GLA_WIDTH, GLA_RANK, DSA_WIDTH, DSA_WIDTH, DSA_WIDTH])[:-1]
    wq, wk, wv, wr, wg, wdq, wdk, wdv = jnp.split(w, [int(s) for s in splits], axis=1)
    wq = (wq * GLA_DK ** -0.5).astype(BF16)
    wkt = wk.T.astype(BF16)
    wv = wv.astype(BF16)
    wr = wr.astype(BF16)
    wg = jnp.pad(wg, ((0, 0), (0, RANK_PAD - GLA_RANK))).astype(BF16)
    wgt = wg.T
    w2 = jnp.pad(gla_gate_w2[0], ((0, RANK_PAD - GLA_RANK), (0, 0))).astype(BF16)
    w2t = w2.T
    gb = gla_gate_b[0].astype(F32).reshape(1, GLA_QK)
    gbt = gb.reshape(GLA_QK, 1)
    wd = jnp.concatenate([wdq * DSA_DH ** -0.5, wdk, wdv], axis=1).astype(BF16)

    gq, gv, gr, kt, glog, glogt, dqkv = _proj(
        x2, attn_norm_g[0].reshape(1, D_MODEL).astype(F32), wq, wv, wr, wkt, wg, wgt, w2, w2t, gb, gbt, wd,
        batch=batch, seq=seq, tm=512)

    o_a = _gla(gq, kt, gv, gr, glog, glogt, gla_norm_g[0].reshape(1, GLA_WIDTH).astype(F32),
               batch=batch, seq=seq, tg=512)

    bias = _bias_tables(rel_bias)
    dq3 = dqkv.reshape(batch, seq, 3 * DSA_WIDTH)
    os_, ls = [], []
    for branch in range(len(DSA_PATTERN)):
        o, lse = _dsa_branch(dq3, bias, branch, batch=batch, seq=seq, tr_max=512)
        os_.append(o)
        ls.append(lse)

    y = _out(x2, o_a, os_, ls, w_out[0].astype(BF16), mlp_norm_g[0].reshape(1, D_MODEL).astype(F32),
             w_ff1[0].astype(BF16), w_ff2[0].astype(BF16), final_norm_g.reshape(1, D_MODEL).astype(F32),
             tm=512, ff_chunk=1024)
    return y.reshape(batch, seq, D_MODEL)
```

```python
import functools
import math

import numpy as np
import jax
import jax.numpy as jnp
from jax import lax
from jax.experimental import pallas as pl
from jax.experimental.pallas import tpu as pltpu

D_MODEL = 1024
GLA_WIDTH = 512
GLA_HEADS = 4
GLA_DK = 64
GLA_DV = 128
GLA_QK = GLA_HEADS * GLA_DK
GLA_RANK = 16
GLA_TAU = 16.0
GLA_CHUNK = 64
DSA_WIDTH = 512
DSA_HEADS = 8
DSA_DH = 64
DSA_PATTERN = ((128, 1), (512, 4), (2048, 16))
DSA_BLOCK = 128
REL_BUCKETS = 32
REL_MAX_DIST = 2048
D_FF = 4096
EPS = 1e-6
NEG = -1e30

LANES = 128
RANK_PAD = LANES
VMEM_LIMIT = 56 * 1024 * 1024

F32 = jnp.float32
BF16 = jnp.bfloat16

_NT = (((1,), (1,)), ((), ()))


def _dot(a, b):
    return jnp.dot(a, b, preferred_element_type=F32)


def _dot_nt(a, b):
    return lax.dot_general(a, b, _NT, preferred_element_type=F32)


def _split_hi_lo(x):
    hi = x.astype(BF16)
    lo = (x - hi.astype(F32)).astype(BF16)
    return hi, lo


def _log_sigmoid(x):
    return jnp.minimum(x, 0.0) - jnp.log1p(jnp.exp(-jnp.abs(x)))


def _bucket_tables():
    max_exact = REL_BUCKETS // 2
    L = DSA_BLOCK
    steps = L + np.arange(L)[:, None] - np.arange(2 * L)[None, :]
    tables = []
    for window, dilation in DSA_PATTERN:
        span = window // dilation
        in_band = (steps >= 0) & (steps <= span)
        n = np.maximum(steps * dilation, 0)
        large = max_exact + (np.log(np.maximum(n, 1) / max_exact)
                             / math.log(REL_MAX_DIST / max_exact)
                             * (REL_BUCKETS - max_exact)).astype(np.int32)
        large = np.minimum(large, REL_BUCKETS - 1)
        bucket = np.where(n < max_exact, n, large).astype(np.int32)
        tables.append(np.where(in_band, bucket, -1).astype(np.int32))
    return np.stack(tables)


def _bias_kernel(rb_ref, bt_ref, out_ref):
    h = pl.program_id(1)
    bt = bt_ref[0]
    acc = jnp.full(bt.shape, NEG, F32)
    for b in range(REL_BUCKETS):
        acc = jnp.where(bt == b, rb_ref[b, h], acc)
    out_ref[0, 0] = acc


def _bias_tables(rel_bias):
    nb = len(DSA_PATTERN)
    L = DSA_BLOCK
    return pl.pallas_call(
        _bias_kernel,
        grid=(nb, DSA_HEADS),
        in_specs=[pl.BlockSpec(memory_space=pltpu.SMEM),
                  pl.BlockSpec((1, L, 2 * L), lambda d, h: (d, 0, 0))],
        out_specs=pl.BlockSpec((1, 1, L, 2 * L), lambda d, h: (d, h, 0, 0)),
        out_shape=jax.ShapeDtypeStruct((nb, DSA_HEADS, L, 2 * L), F32),
        name="dsa_bias",
    )(rel_bias.astype(F32), jnp.asarray(_bucket_tables()))


def _proj_kernel(x_ref, g_ref, wq_ref, wv_ref, wr_ref, wkt_ref, wg_ref, wgt_ref, w2_ref, w2t_ref,
                 b_ref, bt_ref, wd_ref,
                 gq_ref, gv_ref, gr_ref, kt_ref, glog_ref, glogt_ref, d1_ref, d4_ref, d16_ref, dsc_ref):
    x = x_ref[...]
    ms = jnp.mean(x * x, axis=-1, keepdims=True)
    nx = (x * lax.rsqrt(ms + EPS) * g_ref[...]).astype(BF16)

    gq_ref[...] = _dot(nx, wq_ref[...]).astype(BF16)
    gv_ref[...] = _dot(nx, wv_ref[...]).astype(BF16)
    gr_ref[...] = _dot(nx, wr_ref[...]).astype(BF16)
    kt_ref[0] = _dot_nt(wkt_ref[...], nx).astype(BF16)

    glow = _dot(nx, wg_ref[...]).astype(BF16)
    pre = _dot(glow, w2_ref[...]) + b_ref[...]
    glog_ref[...] = _log_sigmoid(pre) * (1.0 / GLA_TAU)
    glow_t = _dot_nt(wgt_ref[...], nx).astype(BF16)
    pre_t = _dot(w2t_ref[...], glow_t) + bt_ref[...]
    glogt_ref[0] = _log_sigmoid(pre_t) * (1.0 / GLA_TAU)

    tm = x.shape[0]
    ncol = 3 * DSA_WIDTH // LANES
    for j in range(ncol // 2):
        res = _dot(nx, wd_ref[:, 2 * j * LANES:(2 * j + 2) * LANES])
        d1_ref[:, 2 * j * LANES:(2 * j + 2) * LANES] = res.astype(BF16)
        dsc_ref[2 * j] = res[:, :LANES]
        dsc_ref[2 * j + 1] = res[:, LANES:]
    for d, out_ref in ((4, d4_ref), (16, d16_ref)):
        for r in range(d):
            for c in range(ncol):
                out_ref[r, :, c * LANES:(c + 1) * LANES] = (
                    dsc_ref[c, pl.ds(r, tm // d, stride=d), :].astype(BF16))


def _proj(x2, g, wq, wv, wr, wkt, wg, wgt, w2, w2t, b, bt, wd, *, batch, seq, tm):
    T = batch * seq
    spb = seq // tm
    full = lambda a: pl.BlockSpec(a.shape, lambda i: (0,) * a.ndim)
    row = lambda n: pl.BlockSpec((tm, n), lambda i: (i, 0))
    colt = pl.BlockSpec((1, GLA_QK, tm), lambda i: (i // spb, 0, i % spb))
    W3 = 3 * DSA_WIDTH
    strided = lambda d: pl.BlockSpec((None, d, tm // d, W3), lambda i: (i // spb, 0, i % spb, 0))
    return pl.pallas_call(
        _proj_kernel,
        grid=(T // tm,),
        in_specs=[row(D_MODEL)] + [full(a) for a in (g, wq, wv, wr, wkt, wg, wgt, w2, w2t, b, bt, wd)],
        out_specs=[row(GLA_QK), row(GLA_WIDTH), row(GLA_WIDTH), colt, row(GLA_QK), colt,
                   row(W3), strided(4), strided(16)],
        out_shape=[jax.ShapeDtypeStruct((T, GLA_QK), BF16),
                   jax.ShapeDtypeStruct((T, GLA_WIDTH), BF16),
                   jax.ShapeDtypeStruct((T, GLA_WIDTH), BF16),
                   jax.ShapeDtypeStruct((batch, GLA_QK, seq), BF16),
                   jax.ShapeDtypeStruct((T, GLA_QK), F32),
                   jax.ShapeDtypeStruct((batch, GLA_QK, seq), F32),
                   jax.ShapeDtypeStruct((T, W3), BF16),
                   jax.ShapeDtypeStruct((batch, 4, seq // 4, W3), BF16),
                   jax.ShapeDtypeStruct((batch, 16, seq // 16, W3), BF16)],
        scratch_shapes=[pltpu.VMEM((W3 // LANES, tm, LANES), F32)],
        compiler_params=pltpu.CompilerParams(dimension_semantics=("arbitrary",),
                                             vmem_limit_bytes=VMEM_LIMIT),
        name="proj",
    )(x2, g, wq, wv, wr, wkt, wg, wgt, w2, w2t, b, bt, wd)


def _gla_kernel(gq_ref, kt_ref, gv_ref, gr_ref, glog_ref, glogt_ref, gn_ref, o_ref, s_ref, *, pairs):
    C = GLA_CHUNK
    P = 2 * C

    @pl.when(pl.program_id(1) == 0)
    def _():
        s_ref[...] = jnp.zeros_like(s_ref)

    ri = lax.broadcasted_iota(jnp.int32, (P, P), 0)
    ci = lax.broadcasted_iota(jnp.int32, (P, P), 1)
    same_chunk = (ri < C) == (ci < C)
    causal = same_chunk & (ci <= ri)
    low = jnp.where(causal, 1.0, 0.0).astype(BF16)
    upp = jnp.where(same_chunk & (ri <= ci), 1.0, 0.0).astype(BF16)
    lane = lax.broadcasted_iota(jnp.int32, (P, LANES), 1)
    first_half = lane < C
    lane_t = lax.broadcasted_iota(jnp.int32, (GLA_QK, P), 1)
    first_t = lane_t < C

    for p in range(pairs):
        rows = slice(p * P, (p + 1) * P)
        g_hi, g_lo = _split_hi_lo(glog_ref[rows, :])
        b = _dot(low, g_hi) + _dot(low, g_lo)
        qd = (gq_ref[rows, :].astype(F32) * jnp.exp(b)).astype(BF16)

        gt = glogt_ref[0, :, rows]
        gt_hi, gt_lo = _split_hi_lo(gt)
        bt = _dot(gt_hi, upp) + _dot(gt_lo, upp)
        tot_a = jnp.sum(jnp.where(first_t, gt, 0.0), axis=-1, keepdims=True)
        tot_b = jnp.sum(jnp.where(first_t, 0.0, gt), axis=-1, keepdims=True)
        tot = jnp.where(first_t, tot_a, tot_b)
        kt = kt_ref[0, :, rows].astype(F32)
        kinv_t = (kt * jnp.exp(-bt)).astype(BF16)
        kend_t = kt * jnp.exp(tot - bt)
        kend_a = jnp.where(first_t, kend_t, 0.0).astype(BF16)
        kend_b = jnp.where(first_t, 0.0, kend_t).astype(BF16)
        dec_a = jnp.exp(tot_a)
        dec_b = jnp.exp(tot_b)

        for h in range(GLA_HEADS):
            hp = h // 2
            grp = slice(hp * LANES, (hp + 1) * LANES)
            own = first_half if h % 2 == 0 else ~first_half
            qm = jnp.where(own, qd[:, grp], jnp.zeros((), BF16))
            hk = slice(h * GLA_DK, (h + 1) * GLA_DK)
            hv = slice(h * GLA_DV, (h + 1) * GLA_DV)
            v = gv_ref[rows, hv]

            att = _dot(qm, kinv_t[grp, :])
            att = jnp.where(causal, att, 0.0).astype(BF16)
            o = _dot(att, v)

            s_pair = s_ref[grp, :]
            s_a = s_ref[hk, :]
            s_b = dec_a[hk, :] * s_a + _dot(kend_a[hk, :], v)
            s_ref[hk, :] = dec_b[hk, :] * s_b + _dot(kend_b[hk, :], v)
            if h % 2 == 0:
                s_pair_b = jnp.concatenate([s_b, s_pair[GLA_DK:, :]], axis=0)
            else:
                s_pair_b = jnp.concatenate([s_pair[:GLA_DK, :], s_b], axis=0)
            o_a = _dot(qm[:C, :], s_pair.astype(BF16))
            o_b = _dot(qm[C:, :], s_pair_b.astype(BF16))
            o = o + jnp.concatenate([o_a, o_b], axis=0)

            o = o * lax.rsqrt(jnp.mean(o * o, axis=-1, keepdims=True) + EPS) * gn_ref[:, hv]
            r = gr_ref[rows, hv].astype(F32)
            o_ref[rows, hv] = (o * (r * jax.nn.sigmoid(r))).astype(BF16)


def _gla(gq, kt, gv, gr, glog, glogt, gn, *, batch, seq, tg):
    T = batch * seq
    spb = seq // tg
    row = lambda n: pl.BlockSpec((tg, n), lambda b, i: (b * spb + i, 0))
    colt = pl.BlockSpec((1, GLA_QK, tg), lambda b, i: (b, 0, i))
    return pl.pallas_call(
        functools.partial(_gla_kernel, pairs=tg // (2 * GLA_CHUNK)),
        grid=(batch, spb),
        in_specs=[row(GLA_QK), colt, row(GLA_WIDTH), row(GLA_WIDTH), row(GLA_QK), colt,
                  pl.BlockSpec((1, GLA_WIDTH), lambda b, i: (0, 0))],
        out_specs=row(GLA_WIDTH),
        out_shape=jax.ShapeDtypeStruct((T, GLA_WIDTH), BF16),
        scratch_shapes=[pltpu.VMEM((GLA_QK, GLA_DV), F32)],
        compiler_params=pltpu.CompilerParams(dimension_semantics=("arbitrary", "arbitrary"),
                                             vmem_limit_bytes=VMEM_LIMIT),
        name="gla",
    )(gq, kt, gv, gr, glog, glogt, gn)


def _dsa_kernel(q_ref, k_ref, v_ref, kh_ref, vh_ref, bias_ref, o_ref, lse_ref, *, blocks):
    L = DSA_BLOCK
    first_tile = pl.program_id(2) == 0
    lane = lax.broadcasted_iota(jnp.int32, (L, LANES), 1)
    first_half = lane < DSA_DH
    prev_cols = lax.broadcasted_iota(jnp.int32, (L, 2 * L), 1) < L

    for blk in range(blocks):
        rows = slice(blk * L, (blk + 1) * L)
        lse_tile = jnp.zeros((L, LANES), F32)
        for hp in range(DSA_HEADS // 2):
            grp = slice(hp * LANES, (hp + 1) * LANES)
            qp = q_ref[rows, grp]
            if blk == 0:
                kcat = jnp.concatenate([kh_ref[:, grp], k_ref[rows, grp]], axis=0)
                vcat = jnp.concatenate([vh_ref[:, grp], v_ref[rows, grp]], axis=0)
            else:
                both = slice((blk - 1) * L, (blk + 1) * L)
                kcat = k_ref[both, grp]
                vcat = v_ref[both, grp]
            outs = []
            for hh in range(2):
                h = 2 * hp + hh
                own = first_half if hh == 0 else ~first_half
                qm = jnp.where(own, qp, jnp.zeros((), BF16))
                bias = bias_ref[h]
                if blk == 0:
                    bias = jnp.where(prev_cols & first_tile, NEG, bias)
                s = _dot_nt(qm, kcat) + bias
                m = jnp.max(s, axis=-1, keepdims=True)
                p = jnp.exp(s - m)
                den = jnp.sum(p, axis=-1, keepdims=True)
                pv = _dot(p.astype(BF16), vcat)
                outs.append(pv / den)
                lse_tile = jnp.where(lane == h, m + jnp.log(den), lse_tile)
            o_ref[rows, grp] = jnp.where(first_half, outs[0], outs[1]).astype(BF16)
        lse_ref[rows, :] = lse_tile


def _dsa_branch(x, bias, branch, *, tr_max):
    batch, d, n, _ = x.shape
    tr = min(tr_max, n)
    hb = tr // DSA_BLOCK
    W = DSA_WIDTH
    blk = lambda part: pl.BlockSpec((None, None, tr, W), lambda b, r, i: (b, r, i, part))
    halo = lambda part: pl.BlockSpec((None, None, DSA_BLOCK, W),
                                     lambda b, r, i: (b, r, jnp.maximum(i * hb - 1, 0), part))
    return pl.pallas_call(
        functools.partial(_dsa_kernel, blocks=hb),
        grid=(batch, d, n // tr),
        in_specs=[blk(0), blk(1), blk(2), halo(1), halo(2),
                  pl.BlockSpec((None, DSA_HEADS, DSA_BLOCK, 2 * DSA_BLOCK), lambda b, r, i: (branch, 0, 0, 0))],
        out_specs=[pl.BlockSpec((None, None, tr, W), lambda b, r, i: (b, r, i, 0)),
                   pl.BlockSpec((None, None, tr, LANES), lambda b, r, i: (b, r, i, 0))],
        out_shape=[jax.ShapeDtypeStruct((batch, d, n, W), BF16),
                   jax.ShapeDtypeStruct((batch, d, n, LANES), F32)],
        compiler_params=pltpu.CompilerParams(dimension_semantics=("arbitrary",) * 3,
                                             vmem_limit_bytes=VMEM_LIMIT),
        name=f"dsa_d{d}",
    )(x, x, x, x, x, bias)


def _out_kernel(x_ref, oa_ref, o1_ref, o4_ref, o16_ref, l1_ref, l4_ref, l16_ref,
                wo_ref, g2_ref, w1_ref, w2_ref, gf_ref, y_ref, osc_ref, lsc_ref, *, ff_chunk):
    tm = x_ref.shape[0]
    npair = DSA_HEADS // 2
    lane = lax.broadcasted_iota(jnp.int32, (tm, LANES), 1)
    first_half = lane < DSA_DH

    for j, (d, o_ref, l_ref) in enumerate(((4, o4_ref, l4_ref), (16, o16_ref, l16_ref))):
        for r in range(d):
            lsc_ref[j, pl.ds(r, tm // d, stride=d), :] = l_ref[r]
            for hp in range(npair):
                osc_ref[j * npair + hp, pl.ds(r, tm // d, stride=d), :] = (
                    o_ref[r, :, hp * LANES:(hp + 1) * LANES].astype(F32))

    l1, l2, l3 = l1_ref[...], lsc_ref[0], lsc_ref[1]
    m = jnp.maximum(jnp.maximum(l1, l2), l3)
    e1, e2, e3 = jnp.exp(l1 - m), jnp.exp(l2 - m), jnp.exp(l3 - m)
    inv = 1.0 / (e1 + e2 + e3)
    ws = (e1 * inv, e2 * inv, e3 * inv)

    ob = []
    for hp in range(npair):
        grp = slice(hp * LANES, (hp + 1) * LANES)
        branch_o = (o1_ref[:, grp].astype(F32), osc_ref[hp], osc_ref[npair + hp])
        acc = jnp.zeros((tm, LANES), F32)
        for w, o in zip(ws, branch_o):
            wa = jnp.sum(jnp.where(lane == 2 * hp, w, 0.0), axis=-1, keepdims=True)
            wb = jnp.sum(jnp.where(lane == 2 * hp + 1, w, 0.0), axis=-1, keepdims=True)
            acc = acc + jnp.where(first_half, wa, wb) * o
        ob.append(acc.astype(BF16))
    mixed = _dot(oa_ref[...], wo_ref[:GLA_WIDTH, :]) + _dot(jnp.concatenate(ob, axis=1), wo_ref[GLA_WIDTH:, :])
    h = x_ref[...] + mixed

    nm = (h * lax.rsqrt(jnp.mean(h * h, axis=-1, keepdims=True) + EPS) * g2_ref[...]).astype(BF16)
    ff = None
    for c in range(D_FF // ff_chunk):
        cols = slice(c * ff_chunk, (c + 1) * ff_chunk)
        a = jnp.maximum(_dot(nm, w1_ref[:, cols]), 0.0)
        d = _dot((a * a).astype(BF16), w2_ref[cols, :])
        ff = d if ff is None else ff + d
    h = h + ff
    y_ref[...] = h * lax.rsqrt(jnp.mean(h * h, axis=-1, keepdims=True) + EPS) * gf_ref[...]


def _out(x2, oa, os_, ls, wo, g2, w1, w2, gf, *, seq, tm, ff_chunk):
    T = x2.shape[0]
    spb = seq // tm
    row = lambda n: pl.BlockSpec((tm, n), lambda i: (i, 0))
    strided = lambda d, n: pl.BlockSpec((None, d, tm // d, n), lambda i: (i // spb, 0, i % spb, 0))
    const = lambda a: pl.BlockSpec(a.shape, lambda i: (0,) * a.ndim, pipeline_mode=pl.Buffered(1))
    W = DSA_WIDTH
    return pl.pallas_call(
        functools.partial(_out_kernel, ff_chunk=ff_chunk),
        grid=(T // tm,),
        in_specs=[row(D_MODEL), row(GLA_WIDTH), row(W), strided(4, W), strided(16, W),
                  row(LANES), strided(4, LANES), strided(16, LANES),
                  const(wo), const(g2), const(w1), const(w2), const(gf)],
        out_specs=row(D_MODEL),
        out_shape=jax.ShapeDtypeStruct((T, D_MODEL), F32),
        scratch_shapes=[pltpu.VMEM((2 * (W // LANES), tm, LANES), F32),
                        pltpu.VMEM((2, tm, LANES), F32)],
        compiler_params=pltpu.CompilerParams(dimension_semantics=("arbitrary",),
                                             vmem_limit_bytes=VMEM_LIMIT),
        name="out_mlp",
    )(x2, oa, *os_, *ls, wo, g2, w1, w2, gf)


def kernel(x, attn_norm_g, w_in, gla_gate_w2, gla_gate_b, gla_norm_g, rel_bias, w_out, mlp_norm_g,
           w_ff1, w_ff2, final_norm_g):
    batch, seq, _ = x.shape
    assert seq % (DSA_PATTERN[-1][1] * DSA_BLOCK) == 0
    T = batch * seq
    x2 = x.reshape(T, D_MODEL)

    w = w_in[0]
    splits = np.cumsum([GLA_QK, GLA_QK, GLA_WIDTH, GLA_WIDTH, GLA_RANK, DSA_WIDTH, DSA_WIDTH, DSA_WIDTH])[:-1]
    wq, wk, wv, wr, wg, wdq, wdk, wdv = jnp.split(w, [int(s) for s in splits], axis=1)
    wq = (wq * GLA_DK ** -0.5).astype(BF16)
    wkt = wk.T.astype(BF16)
    wv = wv.astype(BF16)
    wr = wr.astype(BF16)
    wg = jnp.pad(wg, ((0, 0), (0, RANK_PAD - GLA_RANK))).astype(BF16)
    wgt = wg.T
    w2 = jnp.pad(gla_gate_w2[0], ((0, RANK_PAD - GLA_RANK), (0, 0))).astype(BF16)
    w2t = w2.T
    gb = gla_gate_b[0].astype(F32).reshape(1, GLA_QK)
    gbt = gb.reshape(GLA_QK, 1)
    wd = jnp.concatenate([wdq * DSA_DH ** -0.5, wdk, wdv], axis=1).astype(BF16)

    gq, gv, gr, kt, glog, glogt, d1, d4, d16 = _proj(
        x2, attn_norm_g[0].reshape(1, D_MODEL).astype(F32), wq, wv, wr, wkt, wg, wgt, w2, w2t, gb, gbt, wd,
        batch=batch, seq=seq, tm=512)

    o_a = _gla(gq, kt, gv, gr, glog, glogt, gla_norm_g[0].reshape(1, GLA_WIDTH).astype(F32),
               batch=batch, seq=seq, tg=512)

    bias = _bias_tables(rel_bias)
    os_, ls = [], []
    for branch, xd in enumerate((d1.reshape(batch, 1, seq, 3 * DSA_WIDTH), d4, d16)):
        o, lse = _dsa_branch(xd, bias, branch, tr_max=512)
        os_.append(o)
        ls.append(lse)
    os_[0] = os_[0].reshape(T, DSA_WIDTH)
    ls[0] = ls[0].reshape(T, LANES)

    y = _out(x2, o_a, os_, ls, w_out[0].astype(BF16), mlp_norm_g[0].reshape(1, D_MODEL).astype(F32),
             w_ff1[0].astype(BF16), w_ff2[0].astype(BF16), final_norm_g.reshape(1, D_MODEL).astype(F32),
             seq=seq, tm=512, ff_chunk=1024)
    return y.reshape(batch, seq, D_MODEL)
```

```python
import functools
import math

import numpy as np
import jax
import jax.numpy as jnp
from jax import lax
from jax.experimental import pallas as pl
from jax.experimental.pallas import tpu as pltpu

D_MODEL = 1024
GLA_WIDTH = 512
GLA_HEADS = 4
GLA_DK = 64
GLA_DV = 128
GLA_QK = GLA_HEADS * GLA_DK
GLA_RANK = 16
GLA_TAU = 16.0
GLA_CHUNK = 64
DSA_WIDTH = 512
DSA_HEADS = 8
DSA_DH = 64
DSA_PATTERN = ((128, 1), (512, 4), (2048, 16))
DSA_BLOCK = 128
REL_BUCKETS = 32
REL_MAX_DIST = 2048
D_FF = 4096
EPS = 1e-6
NEG = -1e30

LANES = 128
RANK_PAD = LANES
VMEM_LIMIT = 56 * 1024 * 1024

F32 = jnp.float32
BF16 = jnp.bfloat16

_NT = (((1,), (1,)), ((), ()))


def _dot(a, b):
    return jnp.dot(a, b, preferred_element_type=F32)


def _dot_nt(a, b):
    return lax.dot_general(a, b, _NT, preferred_element_type=F32)


def _split_hi_lo(x):
    hi = x.astype(BF16)
    lo = (x - hi.astype(F32)).astype(BF16)
    return hi, lo


def _log_sigmoid(x):
    return jnp.minimum(x, 0.0) - jnp.log1p(jnp.exp(-jnp.abs(x)))


def _bucket_tables():
    max_exact = REL_BUCKETS // 2
    L = DSA_BLOCK
    steps = L + np.arange(L)[:, None] - np.arange(2 * L)[None, :]
    tables = []
    for window, dilation in DSA_PATTERN:
        span = window // dilation
        in_band = (steps >= 0) & (steps <= span)
        n = np.maximum(steps * dilation, 0)
        large = max_exact + (np.log(np.maximum(n, 1) / max_exact)
                             / math.log(REL_MAX_DIST / max_exact)
                             * (REL_BUCKETS - max_exact)).astype(np.int32)
        large = np.minimum(large, REL_BUCKETS - 1)
        bucket = np.where(n < max_exact, n, large).astype(np.int32)
        tables.append(np.where(in_band, bucket, -1).astype(np.int32))
    return np.stack(tables)


def _bias_kernel(rb_ref, bt_ref, out_ref):
    h = pl.program_id(1)
    bt = bt_ref[0]
    acc = jnp.full(bt.shape, NEG, F32)
    for b in range(REL_BUCKETS):
        acc = jnp.where(bt == b, rb_ref[b, h], acc)
    out_ref[0, 0] = acc


def _bias_tables(rel_bias):
    nb = len(DSA_PATTERN)
    L = DSA_BLOCK
    return pl.pallas_call(
        _bias_kernel,
        grid=(nb, DSA_HEADS),
        in_specs=[pl.BlockSpec(memory_space=pltpu.SMEM),
                  pl.BlockSpec((1, L, 2 * L), lambda d, h: (d, 0, 0))],
        out_specs=pl.BlockSpec((1, 1, L, 2 * L), lambda d, h: (d, h, 0, 0)),
        out_shape=jax.ShapeDtypeStruct((nb, DSA_HEADS, L, 2 * L), F32),
        name="dsa_bias",
    )(rel_bias.astype(F32), jnp.asarray(_bucket_tables()))


def _proj_kernel(x_ref, g_ref, wq_ref, wv_ref, wr_ref, wkt_ref, wg_ref, wgt_ref, w2_ref, w2t_ref,
                 b_ref, bt_ref, wd_ref,
                 gq_ref, gv_ref, gr_ref, kt_ref, glog_ref, glogt_ref, d1_ref, d4_ref, d16_ref, dsc_ref):
    x = x_ref[...]
    ms = jnp.mean(x * x, axis=-1, keepdims=True)
    nx = (x * lax.rsqrt(ms + EPS) * g_ref[...]).astype(BF16)

    gq_ref[...] = _dot(nx, wq_ref[...]).astype(BF16)
    gv_ref[...] = _dot(nx, wv_ref[...]).astype(BF16)
    gr_ref[...] = _dot(nx, wr_ref[...]).astype(BF16)
    kt_ref[0] = _dot_nt(wkt_ref[...], nx).astype(BF16)

    glow = _dot(nx, wg_ref[...]).astype(BF16)
    pre = _dot(glow, w2_ref[...]) + b_ref[...]
    glog_ref[...] = _log_sigmoid(pre) * (1.0 / GLA_TAU)
    glow_t = _dot_nt(wgt_ref[...], nx).astype(BF16)
    pre_t = _dot(w2t_ref[...], glow_t) + bt_ref[...]
    glogt_ref[0] = _log_sigmoid(pre_t) * (1.0 / GLA_TAU)

    tm = x.shape[0]
    ncol = 3 * DSA_WIDTH // LANES
    for j in range(ncol // 2):
        res = _dot(nx, wd_ref[:, 2 * j * LANES:(2 * j + 2) * LANES])
        d1_ref[:, 2 * j * LANES:(2 * j + 2) * LANES] = res.astype(BF16)
        dsc_ref[2 * j] = res[:, :LANES]
        dsc_ref[2 * j + 1] = res[:, LANES:]
    for d, out_ref in ((4, d4_ref), (16, d16_ref)):
        for r in range(d):
            for c in range(ncol):
                out_ref[r, :, c * LANES:(c + 1) * LANES] = (
                    dsc_ref[c, pl.ds(r, tm // d, stride=d), :].astype(BF16))


def _proj(x2, g, wq, wv, wr, wkt, wg, wgt, w2, w2t, b, bt, wd, *, batch, seq, tm):
    T = batch * seq
    spb = seq // tm
    full = lambda a: pl.BlockSpec(a.shape, lambda i: (0,) * a.ndim)
    row = lambda n: pl.BlockSpec((tm, n), lambda i: (i, 0))
    colt = pl.BlockSpec((1, GLA_QK, tm), lambda i: (i // spb, 0, i % spb))
    W3 = 3 * DSA_WIDTH
    strided = lambda d: pl.BlockSpec((None, d, tm // d, W3), lambda i: (i // spb, 0, i % spb, 0))
    return pl.pallas_call(
        _proj_kernel,
        grid=(T // tm,),
        in_specs=[row(D_MODEL)] + [full(a) for a in (g, wq, wv, wr, wkt, wg, wgt, w2, w2t, b, bt, wd)],
        out_specs=[row(GLA_QK), row(GLA_WIDTH), row(GLA_WIDTH), colt, row(GLA_QK), colt,
                   row(W3), strided(4), strided(16)],
        out_shape=[jax.ShapeDtypeStruct((T, GLA_QK), BF16),
                   jax.ShapeDtypeStruct((T, GLA_WIDTH), BF16),
                   jax.ShapeDtypeStruct((T, GLA_WIDTH), BF16),
                   jax.ShapeDtypeStruct((batch, GLA_QK, seq), BF16),
                   jax.ShapeDtypeStruct((T, GLA_QK), F32),
                   jax.ShapeDtypeStruct((batch, GLA_QK, seq), F32),
                   jax.ShapeDtypeStruct((T, W3), BF16),
                   jax.ShapeDtypeStruct((batch, 4, seq // 4, W3), BF16),
                   jax.ShapeDtypeStruct((batch, 16, seq // 16, W3), BF16)],
        scratch_shapes=[pltpu.VMEM((W3 // LANES, tm, LANES), F32)],
        compiler_params=pltpu.CompilerParams(dimension_semantics=("arbitrary",),
                                             vmem_limit_bytes=VMEM_LIMIT),
        name="proj",
    )(x2, g, wq, wv, wr, wkt, wg, wgt, w2, w2t, b, bt, wd)


def _gla_kernel(gq_ref, kt_ref, gv_ref, gr_ref, glog_ref, glogt_ref, gn_ref, o_ref, s_ref, *, pairs):
    C = GLA_CHUNK
    P = 2 * C

    @pl.when(pl.program_id(1) == 0)
    def _():
        s_ref[...] = jnp.zeros_like(s_ref)

    ri = lax.broadcasted_iota(jnp.int32, (P, P), 0)
    ci = lax.broadcasted_iota(jnp.int32, (P, P), 1)
    same_chunk = (ri < C) == (ci < C)
    causal = same_chunk & (ci <= ri)
    low = jnp.where(causal, 1.0, 0.0).astype(BF16)
    upp = jnp.where(same_chunk & (ri <= ci), 1.0, 0.0).astype(BF16)
    lane = lax.broadcasted_iota(jnp.int32, (P, LANES), 1)
    first_half = lane < C
    lane_t = lax.broadcasted_iota(jnp.int32, (GLA_QK, P), 1)
    first_t = lane_t < C

    for p in range(pairs):
        rows = slice(p * P, (p + 1) * P)
        g_hi, g_lo = _split_hi_lo(glog_ref[rows, :])
        b = _dot(low, g_hi) + _dot(low, g_lo)
        qd = (gq_ref[rows, :].astype(F32) * jnp.exp(b)).astype(BF16)

        gt = glogt_ref[0, :, rows]
        gt_hi, gt_lo = _split_hi_lo(gt)
        bt = _dot(gt_hi, upp) + _dot(gt_lo, upp)
        tot_a = jnp.sum(jnp.where(first_t, gt, 0.0), axis=-1, keepdims=True)
        tot_b = jnp.sum(jnp.where(first_t, 0.0, gt), axis=-1, keepdims=True)
        tot = jnp.where(first_t, tot_a, tot_b)
        kt = kt_ref[0, :, rows].astype(F32)
        kinv_t = (kt * jnp.exp(-bt)).astype(BF16)
        kend_t = kt * jnp.exp(tot - bt)
        kend_a = jnp.where(first_t, kend_t, 0.0).astype(BF16)
        kend_b = jnp.where(first_t, 0.0, kend_t).astype(BF16)
        dec_a = jnp.exp(tot_a)
        dec_b = jnp.exp(tot_b)

        for h in range(GLA_HEADS):
            hp = h // 2
            grp = slice(hp * LANES, (hp + 1) * LANES)
            own = first_half if h % 2 == 0 else ~first_half
            qm = jnp.where(own, qd[:, grp], jnp.zeros((), BF16))
            hk = slice(h * GLA_DK, (h + 1) * GLA_DK)
            hv = slice(h * GLA_DV, (h + 1) * GLA_DV)
            v = gv_ref[rows, hv]

            att = _dot(qm, kinv_t[grp, :])
            att = jnp.where(causal, att, 0.0).astype(BF16)
            o = _dot(att, v)

            s_pair = s_ref[grp, :]
            s_a = s_ref[hk, :]
            s_b = dec_a[hk, :] * s_a + _dot(kend_a[hk, :], v)
            s_ref[hk, :] = dec_b[hk, :] * s_b + _dot(kend_b[hk, :], v)
            if h % 2 == 0:
                s_pair_b = jnp.concatenate([s_b, s_pair[GLA_DK:, :]], axis=0)
            else:
                s_pair_b = jnp.concatenate([s_pair[:GLA_DK, :], s_b], axis=0)
            o_a = _dot(qm[:C, :], s_pair.astype(BF16))
            o_b = _dot(qm[C:, :], s_pair_b.astype(BF16))
            o = o + jnp.concatenate([o_a, o_b], axis=0)

            o = o * lax.rsqrt(jnp.mean(o * o, axis=-1, keepdims=True) + EPS) * gn_ref[:, hv]
            r = gr_ref[rows, hv].astype(F32)
            o_ref[rows, hv] = (o * (r * jax.nn.sigmoid(r))).astype(BF16)


def _gla(gq, kt, gv, gr, glog, glogt, gn, *, batch, seq, tg):
    T = batch * seq
    spb = seq // tg
    row = lambda n: pl.BlockSpec((tg, n), lambda b, i: (b * spb + i, 0))
    colt = pl.BlockSpec((1, GLA_QK, tg), lambda b, i: (b, 0, i))
    return pl.pallas_call(
        functools.partial(_gla_kernel, pairs=tg // (2 * GLA_CHUNK)),
        grid=(batch, spb),
        in_specs=[row(GLA_QK), colt, row(GLA_WIDTH), row(GLA_WIDTH), row(GLA_QK), colt,
                  pl.BlockSpec((1, GLA_WIDTH), lambda b, i: (0, 0))],
        out_specs=row(GLA_WIDTH),
        out_shape=jax.ShapeDtypeStruct((T, GLA_WIDTH), BF16),
        scratch_shapes=[pltpu.VMEM((GLA_QK, GLA_DV), F32)],
        compiler_params=pltpu.CompilerParams(dimension_semantics=("arbitrary", "arbitrary"),
                                             vmem_limit_bytes=VMEM_LIMIT),
        name="gla",
    )(gq, kt, gv, gr, glog, glogt, gn)


def _dsa_kernel(q_ref, k_ref, v_ref, kh_ref, vh_ref, bias_ref, o_ref, lse_ref, *, blocks):
    L = DSA_BLOCK
    first_tile = pl.program_id(2) == 0
    lane = lax.broadcasted_iota(jnp.int32, (L, LANES), 1)
    first_half = lane < DSA_DH
    prev_cols = lax.broadcasted_iota(jnp.int32, (L, 2 * L), 1) < L

    for blk in range(blocks):
        rows = slice(blk * L, (blk + 1) * L)
        lse_tile = jnp.zeros((L, LANES), F32)
        for hp in range(DSA_HEADS // 2):
            grp = slice(hp * LANES, (hp + 1) * LANES)
            qp = q_ref[rows, grp]
            if blk == 0:
                kcat = jnp.concatenate([kh_ref[:, grp], k_ref[rows, grp]], axis=0)
                vcat = jnp.concatenate([vh_ref[:, grp], v_ref[rows, grp]], axis=0)
            else:
                both = slice((blk - 1) * L, (blk + 1) * L)
                kcat = k_ref[both, grp]
                vcat = v_ref[both, grp]
            outs = []
            for hh in range(2):
                h = 2 * hp + hh
                own = first_half if hh == 0 else ~first_half
                qm = jnp.where(own, qp, jnp.zeros((), BF16))
                bias = bias_ref[h]
                if blk == 0:
                    bias = jnp.where(prev_cols & first_tile, NEG, bias)
                s = _dot_nt(qm, kcat) + bias
                m = jnp.max(s, axis=-1, keepdims=True)
                p = jnp.exp(s - m)
                den = jnp.sum(p, axis=-1, keepdims=True)
                pv = _dot(p.astype(BF16), vcat)
                outs.append(pv / den)
                lse_tile = jnp.where(lane == h, m + jnp.log(den), lse_tile)
            o_ref[rows, grp] = jnp.where(first_half, outs[0], outs[1]).astype(BF16)
        lse_ref[rows, :] = lse_tile


def _dsa_branch(x, bias, branch, *, tr_max):
    batch, d, n, _ = x.shape
    tr = min(tr_max, n)
    hb = tr // DSA_BLOCK
    W = DSA_WIDTH
    blk = lambda part: pl.BlockSpec((None, None, tr, W), lambda b, r, i: (b, r, i, part))
    halo = lambda part: pl.BlockSpec((None, None, DSA_BLOCK, W),
                                     lambda b, r, i: (b, r, jnp.maximum(i * hb - 1, 0), part))
    return pl.pallas_call(
        functools.partial(_dsa_kernel, blocks=hb),
        grid=(batch, d, n // tr),
        in_specs=[blk(0), blk(1), blk(2), halo(1), halo(2),
                  pl.BlockSpec((None, DSA_HEADS, DSA_BLOCK, 2 * DSA_BLOCK), lambda b, r, i: (branch, 0, 0, 0))],
        out_specs=[pl.BlockSpec((None, None, tr, W), lambda b, r, i: (b, r, i, 0)),
                   pl.BlockSpec((None, None, tr, LANES), lambda b, r, i: (b, r, i, 0))],
        out_shape=[jax.ShapeDtypeStruct((batch, d, n, W), BF16),
                   jax.ShapeDtypeStruct((batch, d, n, LANES), F32)],
        compiler_params=pltpu.CompilerParams(dimension_semantics=("arbitrary",) * 3,
                                             vmem_limit_bytes=VMEM_LIMIT),
        name=f"dsa_d{d}",
    )(x, x, x, x, x, bias)


def _out_kernel(x_ref, oa_ref, o1_ref, o4_ref, o16_ref, l1_ref, l4_ref, l16_ref,
                wo_ref, g2_ref, w1_ref, w2_ref, gf_ref, y_ref, osc_ref, lsc_ref, ob_ref, *, ff_chunk):
    tm = x_ref.shape[0]
    npair = DSA_HEADS // 2
    lane = lax.broadcasted_iota(jnp.int32, (tm, LANES), 1)
    first_half = lane < DSA_DH

    @pl.when(pl.program_id(0) == 0)
    def _():
        ob_ref[...] = jnp.zeros_like(ob_ref)

    mixed = _dot(oa_ref[...], wo_ref[:GLA_WIDTH, :]) + _dot(ob_ref[...], wo_ref[GLA_WIDTH:, :])
    h = x_ref[...] + mixed
    nm = (h * lax.rsqrt(jnp.mean(h * h, axis=-1, keepdims=True) + EPS) * g2_ref[...]).astype(BF16)
    ff = None
    for c in range(D_FF // ff_chunk):
        cols = slice(c * ff_chunk, (c + 1) * ff_chunk)
        a = jnp.maximum(_dot(nm, w1_ref[:, cols]), 0.0)
        d = _dot((a * a).astype(BF16), w2_ref[cols, :])
        ff = d if ff is None else ff + d
    h = h + ff
    y_ref[...] = h * lax.rsqrt(jnp.mean(h * h, axis=-1, keepdims=True) + EPS) * gf_ref[...]

    for j, (d, o_ref, l_ref) in enumerate(((4, o4_ref, l4_ref), (16, o16_ref, l16_ref))):
        for r in range(d):
            lsc_ref[j, pl.ds(r, tm // d, stride=d), :] = l_ref[r]
            for hp in range(npair):
                osc_ref[j * npair + hp, pl.ds(r, tm // d, stride=d), :] = (
                    o_ref[r, :, hp * LANES:(hp + 1) * LANES].astype(F32))

    l1, l2, l3 = l1_ref[...], lsc_ref[0], lsc_ref[1]
    m = jnp.maximum(jnp.maximum(l1, l2), l3)
    e1, e2, e3 = jnp.exp(l1 - m), jnp.exp(l2 - m), jnp.exp(l3 - m)
    inv = 1.0 / (e1 + e2 + e3)
    ws = (e1 * inv, e2 * inv, e3 * inv)

    for hp in range(npair):
        grp = slice(hp * LANES, (hp + 1) * LANES)
        branch_o = (o1_ref[:, grp].astype(F32), osc_ref[hp], osc_ref[npair + hp])
        acc = jnp.zeros((tm, LANES), F32)
        for w, o in zip(ws, branch_o):
            wa = jnp.sum(jnp.where(lane == 2 * hp, w, 0.0), axis=-1, keepdims=True)
            wb = jnp.sum(jnp.where(lane == 2 * hp + 1, w, 0.0), axis=-1, keepdims=True)
            acc = acc + jnp.where(first_half, wa, wb) * o
        ob_ref[:, grp] = acc.astype(BF16)


def _out(x2, oa, os_, ls, wo, g2, w1, w2, gf, *, seq, tm, ff_chunk):
    T = x2.shape[0]
    spb = seq // tm
    nt = T // tm
    prev = lambda s: jnp.maximum(s - 1, 0)
    cur = lambda s: jnp.minimum(s, nt - 1)
    row_prev = lambda n: pl.BlockSpec((tm, n), lambda s: (prev(s), 0))
    row_cur = lambda n: pl.BlockSpec((tm, n), lambda s: (cur(s), 0))
    strided = lambda d, n: pl.BlockSpec((None, d, tm // d, n), lambda s: (cur(s) // spb, 0, cur(s) % spb, 0))
    const = lambda a: pl.BlockSpec(a.shape, lambda s: (0,) * a.ndim, pipeline_mode=pl.Buffered(1))
    W = DSA_WIDTH
    return pl.pallas_call(
        functools.partial(_out_kernel, ff_chunk=ff_chunk),
        grid=(nt + 1,),
        in_specs=[row_prev(D_MODEL), row_prev(GLA_WIDTH), row_cur(W), strided(4, W), strided(16, W),
                  row_cur(LANES), strided(4, LANES), strided(16, LANES),
                  const(wo), const(g2), const(w1), const(w2), const(gf)],
        out_specs=row_prev(D_MODEL),
        out_shape=jax.ShapeDtypeStruct((T, D_MODEL), F32),
        scratch_shapes=[pltpu.VMEM((2 * (W // LANES), tm, LANES), F32),
                        pltpu.VMEM((2, tm, LANES), F32),
                        pltpu.VMEM((tm, W), BF16)],
        compiler_params=pltpu.CompilerParams(dimension_semantics=("arbitrary",),
                                             vmem_limit_bytes=VMEM_LIMIT),
        name="out_mlp",
    )(x2, oa, *os_, *ls, wo, g2, w1, w2, gf)


def kernel(x, attn_norm_g, w_in, gla_gate_w2, gla_gate_b, gla_norm_g, rel_bias, w_out, mlp_norm_g,
           w_ff1, w_ff2, final_norm_g):
    batch, seq, _ = x.shape
    assert seq % (DSA_PATTERN[-1][1] * DSA_BLOCK) == 0
    T = batch * seq
    x2 = x.reshape(T, D_MODEL)

    w = w_in[0]
    splits = np.cumsum([GLA_QK, GLA_QK, GLA_WIDTH, GLA_WIDTH, GLA_RANK, DSA_WIDTH, DSA_WIDTH, DSA_WIDTH])[:-1]
    wq, wk, wv, wr, wg, wdq, wdk, wdv = jnp.split(w, [int(s) for s in splits], axis=1)
    wq = (wq * GLA_DK ** -0.5).astype(BF16)
    wkt = wk.T.astype(BF16)
    wv = wv.astype(BF16)
    wr = wr.astype(BF16)
    wg = jnp.pad(wg, ((0, 0), (0, RANK_PAD - GLA_RANK))).astype(BF16)
    wgt = wg.T
    w2 = jnp.pad(gla_gate_w2[0], ((0, RANK_PAD - GLA_RANK), (0, 0))).astype(BF16)
    w2t = w2.T
    gb = gla_gate_b[0].astype(F32).reshape(1, GLA_QK)
    gbt = gb.reshape(GLA_QK, 1)
    wd = jnp.concatenate([wdq * DSA_DH ** -0.5, wdk, wdv], axis=1).astype(BF16)

    gq, gv, gr, kt, glog, glogt, d1, d4, d16 = _proj(
        x2, attn_norm_g[0].reshape(1, D_MODEL).astype(F32), wq, wv, wr, wkt, wg, wgt, w2, w2t, gb, gbt, wd,
        batch=batch, seq=seq, tm=512)

    o_a = _gla(gq, kt, gv, gr, glog, glogt, gla_norm_g[0].reshape(1, GLA_WIDTH).astype(F32),
               batch=batch, seq=seq, tg=512)

    bias = _bias_tables(rel_bias)
    os_, ls = [], []
    for branch, xd in enumerate((d1.reshape(batch, 1, seq, 3 * DSA_WIDTH), d4, d16)):
        o, lse = _dsa_branch(xd, bias, branch, tr_max=512)
        os_.append(o)
        ls.append(lse)
    os_[0] = os_[0].reshape(T, DSA_WIDTH)
    ls[0] = ls[0].reshape(T, LANES)

    y = _out(x2, o_a, os_, ls, w_out[0].astype(BF16), mlp_norm_g[0].reshape(1, D_MODEL).astype(F32),
             w_ff1[0].astype(BF16), w_ff2[0].astype(BF16), final_norm_g.reshape(1, D_MODEL).astype(F32),
             seq=seq, tm=512, ff_chunk=1024)
    return y.reshape(batch, seq, D_MODEL)
```

```python
import functools
import math

import numpy as np
import jax
import jax.numpy as jnp
from jax import lax
from jax.experimental import pallas as pl
from jax.experimental.pallas import tpu as pltpu

D_MODEL = 1024
GLA_WIDTH = 512
GLA_HEADS = 4
GLA_DK = 64
GLA_DV = 128
GLA_QK = GLA_HEADS * GLA_DK
GLA_RANK = 16
GLA_TAU = 16.0
GLA_CHUNK = 64
DSA_WIDTH = 512
DSA_HEADS = 8
DSA_DH = 64
DSA_PATTERN = ((128, 1), (512, 4), (2048, 16))
DSA_BLOCK = 128
REL_BUCKETS = 32
REL_MAX_DIST = 2048
D_FF = 4096
EPS = 1e-6
NEG = -1e30

LANES = 128
RANK_PAD = LANES
VMEM_LIMIT = 56 * 1024 * 1024

F32 = jnp.float32
BF16 = jnp.bfloat16

_NT = (((1,), (1,)), ((), ()))


def _dot(a, b):
    return jnp.dot(a, b, preferred_element_type=F32)


def _dot_nt(a, b):
    return lax.dot_general(a, b, _NT, preferred_element_type=F32)


def _split_hi_lo(x):
    hi = x.astype(BF16)
    lo = (x - hi.astype(F32)).astype(BF16)
    return hi, lo


def _log_sigmoid(x):
    return jnp.minimum(x, 0.0) - jnp.log1p(jnp.exp(-jnp.abs(x)))


def _bucket_tables():
    max_exact = REL_BUCKETS // 2
    L = DSA_BLOCK
    steps = L + np.arange(L)[:, None] - np.arange(2 * L)[None, :]
    tables = []
    for window, dilation in DSA_PATTERN:
        span = window // dilation
        in_band = (steps >= 0) & (steps <= span)
        n = np.maximum(steps * dilation, 0)
        large = max_exact + (np.log(np.maximum(n, 1) / max_exact)
                             / math.log(REL_MAX_DIST / max_exact)
                             * (REL_BUCKETS - max_exact)).astype(np.int32)
        large = np.minimum(large, REL_BUCKETS - 1)
        bucket = np.where(n < max_exact, n, large).astype(np.int32)
        tables.append(np.where(in_band, bucket, -1).astype(np.int32))
    return np.stack(tables)


def _bias_kernel(rb_ref, bt_ref, out_ref):
    h = pl.program_id(1)
    bt = bt_ref[0]
    acc = jnp.full(bt.shape, NEG, F32)
    for b in range(REL_BUCKETS):
        acc = jnp.where(bt == b, rb_ref[b, h], acc)
    out_ref[0, 0] = acc


def _bias_tables(rel_bias):
    nb = len(DSA_PATTERN)
    L = DSA_BLOCK
    return pl.pallas_call(
        _bias_kernel,
        grid=(nb, DSA_HEADS),
        in_specs=[pl.BlockSpec(memory_space=pltpu.SMEM),
                  pl.BlockSpec((1, L, 2 * L), lambda d, h: (d, 0, 0))],
        out_specs=pl.BlockSpec((1, 1, L, 2 * L), lambda d, h: (d, h, 0, 0)),
        out_shape=jax.ShapeDtypeStruct((nb, DSA_HEADS, L, 2 * L), F32),
        name="dsa_bias",
    )(rel_bias.astype(F32), jnp.asarray(_bucket_tables()))


def _proj_kernel(x_ref, g_ref, wq_ref, wv_ref, wr_ref, wk_ref, wg_ref, w2_ref, b_ref, wd_ref,
                 gq_ref, gv_ref, gr_ref, d1_ref, kt_ref, glog_ref, glogt_ref, d4_ref, d16_ref,
                 dsc_ref, t4_ref):
    tm = x_ref.shape[0]
    ncol = 3 * DSA_WIDTH // LANES
    n4 = tm // 4
    n16 = tm // 16

    x = x_ref[...]
    ms = jnp.mean(x * x, axis=-1, keepdims=True)
    nx = (x * lax.rsqrt(ms + EPS) * g_ref[...]).astype(BF16)

    kt_ref[0] = _dot(nx, wk_ref[...]).T.astype(BF16)
    glow = _dot(nx, wg_ref[...]).astype(BF16)
    glog = _log_sigmoid(_dot(glow, w2_ref[...]) + b_ref[...]) * (1.0 / GLA_TAU)
    glog_ref[...] = glog
    glogt_ref[0] = glog.T

    gq_ref[...] = _dot(nx, wq_ref[...]).astype(BF16)
    gv_ref[...] = _dot(nx, wv_ref[...]).astype(BF16)
    gr_ref[...] = _dot(nx, wr_ref[...]).astype(BF16)

    for j in range(ncol // 2):
        res = _dot(nx, wd_ref[:, 2 * j * LANES:(2 * j + 2) * LANES])
        d1_ref[:, 2 * j * LANES:(2 * j + 2) * LANES] = res.astype(BF16)
        dsc_ref[2 * j] = res[:, :LANES]
        dsc_ref[2 * j + 1] = res[:, LANES:]
        for c in (2 * j, 2 * j + 1):
            cols = slice(c * LANES, (c + 1) * LANES)
            for r4 in range(4):
                sub = dsc_ref[c, pl.ds(r4, n4, stride=4), :]
                d4_ref[r4, :, cols] = sub.astype(BF16)
                t4_ref[c, r4 * n4:(r4 + 1) * n4, :] = sub
            for r4 in range(4):
                for r2 in range(4):
                    d16_ref[r4 + 4 * r2, :, cols] = (
                        t4_ref[c, pl.ds(r4 * n4 + r2, n16, stride=4), :].astype(BF16))


def _proj(x2, g, wq, wv, wr, wk, wg, w2, b, wd, *, batch, seq, tm):
    T = batch * seq
    spb = seq // tm
    nt = T // tm
    full = lambda a: pl.BlockSpec(a.shape, lambda s: (0,) * a.ndim)
    row_cur = lambda n: pl.BlockSpec((tm, n), lambda s: (s, 0))
    row_prev = row_cur
    colt = pl.BlockSpec((1, GLA_QK, tm), lambda s: (s // spb, 0, s % spb))
    W3 = 3 * DSA_WIDTH
    strided = lambda d: pl.BlockSpec((None, d, tm // d, W3), lambda s: (s // spb, 0, s % spb, 0))
    return pl.pallas_call(
        _proj_kernel,
        grid=(nt,),
        in_specs=[row_cur(D_MODEL)] + [full(a) for a in (g, wq, wv, wr, wk, wg, w2, b, wd)],
        out_specs=[row_cur(GLA_QK), row_cur(GLA_WIDTH), row_cur(GLA_WIDTH), row_cur(W3),
                   colt, row_prev(GLA_QK), colt, strided(4), strided(16)],
        out_shape=[jax.ShapeDtypeStruct((T, GLA_QK), BF16),
                   jax.ShapeDtypeStruct((T, GLA_WIDTH), BF16),
                   jax.ShapeDtypeStruct((T, GLA_WIDTH), BF16),
                   jax.ShapeDtypeStruct((T, W3), BF16),
                   jax.ShapeDtypeStruct((batch, GLA_QK, seq), BF16),
                   jax.ShapeDtypeStruct((T, GLA_QK), F32),
                   jax.ShapeDtypeStruct((batch, GLA_QK, seq), F32),
                   jax.ShapeDtypeStruct((batch, 4, seq // 4, W3), BF16),
                   jax.ShapeDtypeStruct((batch, 16, seq // 16, W3), BF16)],
        scratch_shapes=[pltpu.VMEM((W3 // LANES, tm, LANES), F32)] * 2,
        compiler_params=pltpu.CompilerParams(dimension_semantics=("arbitrary",),
                                             vmem_limit_bytes=VMEM_LIMIT),
        name="proj",
    )(x2, g, wq, wv, wr, wk, wg, w2, b, wd)


def _gla_kernel(gq_ref, kt_ref, gv_ref, gr_ref, glog_ref, glogt_ref, gn_ref, o_ref, s_ref, *, pairs):
    C = GLA_CHUNK
    P = 2 * C

    @pl.when(pl.program_id(1) == 0)
    def _():
        s_ref[...] = jnp.zeros_like(s_ref)

    ri = lax.broadcasted_iota(jnp.int32, (P, P), 0)
    ci = lax.broadcasted_iota(jnp.int32, (P, P), 1)
    same_chunk = (ri < C) == (ci < C)
    causal = same_chunk & (ci <= ri)
    low = jnp.where(causal, 1.0, 0.0).astype(BF16)
    upp = jnp.where(same_chunk & (ri <= ci), 1.0, 0.0).astype(BF16)
    lane = lax.broadcasted_iota(jnp.int32, (P, LANES), 1)
    first_half = lane < C
    lane_t = lax.broadcasted_iota(jnp.int32, (GLA_QK, P), 1)
    first_t = lane_t < C

    for p in range(pairs):
        rows = slice(p * P, (p + 1) * P)
        g_hi, g_lo = _split_hi_lo(glog_ref[rows, :])
        b = _dot(low, g_hi) + _dot(low, g_lo)
        qd = (gq_ref[rows, :].astype(F32) * jnp.exp(b)).astype(BF16)

        gt = glogt_ref[0, :, rows]
        gt_hi, gt_lo = _split_hi_lo(gt)
        bt = _dot(gt_hi, upp) + _dot(gt_lo, upp)
        tot_a = jnp.sum(jnp.where(first_t, gt, 0.0), axis=-1, keepdims=True)
        tot_b = jnp.sum(jnp.where(first_t, 0.0, gt), axis=-1, keepdims=True)
        tot = jnp.where(first_t, tot_a, tot_b)
        kt = kt_ref[0, :, rows].astype(F32)
        kinv_t = (kt * jnp.exp(-bt)).astype(BF16)
        kend_t = kt * jnp.exp(tot - bt)
        kend_a = jnp.where(first_t, kend_t, 0.0).astype(BF16)
        kend_b = jnp.where(first_t, 0.0, kend_t).astype(BF16)
        dec_a = jnp.exp(tot_a)
        dec_b = jnp.exp(tot_b)

        for h in range(GLA_HEADS):
            hp = h // 2
            grp = slice(hp * LANES, (hp + 1) * LANES)
            own = first_half if h % 2 == 0 else ~first_half
            qm = jnp.where(own, qd[:, grp], jnp.zeros((), BF16))
            hk = slice(h * GLA_DK, (h + 1) * GLA_DK)
            hv = slice(h * GLA_DV, (h + 1) * GLA_DV)
            v = gv_ref[rows, hv]

            att = _dot(qm, kinv_t[grp, :])
            att = jnp.where(causal, att, 0.0).astype(BF16)
            o = _dot(att, v)

            s_pair = s_ref[grp, :]
            s_a = s_ref[hk, :]
            s_b = dec_a[hk, :] * s_a + _dot(kend_a[hk, :], v)
            s_ref[hk, :] = dec_b[hk, :] * s_b + _dot(kend_b[hk, :], v)
            if h % 2 == 0:
                s_pair_b = jnp.concatenate([s_b, s_pair[GLA_DK:, :]], axis=0)
            else:
                s_pair_b = jnp.concatenate([s_pair[:GLA_DK, :], s_b], axis=0)
            o_a = _dot(qm[:C, :], s_pair.astype(BF16))
            o_b = _dot(qm[C:, :], s_pair_b.astype(BF16))
            o = o + jnp.concatenate([o_a, o_b], axis=0)

            o = o * lax.rsqrt(jnp.mean(o * o, axis=-1, keepdims=True) + EPS) * gn_ref[:, hv]
            r = gr_ref[rows, hv].astype(F32)
            o_ref[rows, hv] = (o * (r * jax.nn.sigmoid(r))).astype(BF16)


def _gla(gq, kt, gv, gr, glog, glogt, gn, *, batch, seq, tg):
    T = batch * seq
    spb = seq // tg
    row = lambda n: pl.BlockSpec((tg, n), lambda b, i: (b * spb + i, 0))
    colt = pl.BlockSpec((1, GLA_QK, tg), lambda b, i: (b, 0, i))
    return pl.pallas_call(
        functools.partial(_gla_kernel, pairs=tg // (2 * GLA_CHUNK)),
        grid=(batch, spb),
        in_specs=[row(GLA_QK), colt, row(GLA_WIDTH), row(GLA_WIDTH), row(GLA_QK), colt,
                  pl.BlockSpec((1, GLA_WIDTH), lambda b, i: (0, 0))],
        out_specs=row(GLA_WIDTH),
        out_shape=jax.ShapeDtypeStruct((T, GLA_WIDTH), BF16),
        scratch_shapes=[pltpu.VMEM((GLA_QK, GLA_DV), F32)],
        compiler_params=pltpu.CompilerParams(dimension_semantics=("arbitrary", "arbitrary"),
                                             vmem_limit_bytes=VMEM_LIMIT),
        name="gla",
    )(gq, kt, gv, gr, glog, glogt, gn)


def _dsa_kernel(q_ref, k_ref, v_ref, kh_ref, vh_ref, bias_ref, o_ref, lse_ref, *, blocks):
    L = DSA_BLOCK
    first_tile = pl.program_id(2) == 0
    lane = lax.broadcasted_iota(jnp.int32, (L, LANES), 1)
    first_half = lane < DSA_DH
    prev_cols = lax.broadcasted_iota(jnp.int32, (L, 2 * L), 1) < L

    for blk in range(blocks):
        rows = slice(blk * L, (blk + 1) * L)
        lse_tile = jnp.zeros((L, LANES), F32)
        for hp in range(DSA_HEADS // 2):
            grp = slice(hp * LANES, (hp + 1) * LANES)
            qp = q_ref[rows, grp]
            if blk == 0:
                kcat = jnp.concatenate([kh_ref[:, grp], k_ref[rows, grp]], axis=0)
                vcat = jnp.concatenate([vh_ref[:, grp], v_ref[rows, grp]], axis=0)
            else:
                both = slice((blk - 1) * L, (blk + 1) * L)
                kcat = k_ref[both, grp]
                vcat = v_ref[both, grp]
            outs = []
            for hh in range(2):
                h = 2 * hp + hh
                own = first_half if hh == 0 else ~first_half
                qm = jnp.where(own, qp, jnp.zeros((), BF16))
                bias = bias_ref[h]
                if blk == 0:
                    bias = jnp.where(prev_cols & first_tile, NEG, bias)
                s = _dot_nt(qm, kcat) + bias
                m = jnp.max(s, axis=-1, keepdims=True)
                p = jnp.exp(s - m)
                den = jnp.sum(p, axis=-1, keepdims=True)
                pv = _dot(p.astype(BF16), vcat)
                outs.append(pv / den)
                lse_tile = jnp.where(lane == h, m + jnp.log(den), lse_tile)
            o_ref[rows, grp] = jnp.where(first_half, outs[0], outs[1]).astype(BF16)
        lse_ref[rows, :] = lse_tile


def _dsa_branch(x, bias, branch, *, tr_max):
    batch, d, n, _ = x.shape
    tr = min(tr_max, n)
    hb = tr // DSA_BLOCK
    W = DSA_WIDTH
    blk = lambda part: pl.BlockSpec((None, None, tr, W), lambda b, r, i: (b, r, i, part))
    halo = lambda part: pl.BlockSpec((None, None, DSA_BLOCK, W),
                                     lambda b, r, i: (b, r, jnp.maximum(i * hb - 1, 0), part))
    return pl.pallas_call(
        functools.partial(_dsa_kernel, blocks=hb),
        grid=(batch, d, n // tr),
        in_specs=[blk(0), blk(1), blk(2), halo(1), halo(2),
                  pl.BlockSpec((None, DSA_HEADS, DSA_BLOCK, 2 * DSA_BLOCK), lambda b, r, i: (branch, 0, 0, 0))],
        out_specs=[pl.BlockSpec((None, None, tr, W), lambda b, r, i: (b, r, i, 0)),
                   pl.BlockSpec((None, None, tr, LANES), lambda b, r, i: (b, r, i, 0))],
        out_shape=[jax.ShapeDtypeStruct((batch, d, n, W), BF16),
                   jax.ShapeDtypeStruct((batch, d, n, LANES), F32)],
        compiler_params=pltpu.CompilerParams(dimension_semantics=("arbitrary",) * 3,
                                             vmem_limit_bytes=VMEM_LIMIT),
        name=f"dsa_d{d}",
    )(x, x, x, x, x, bias)


def _out_kernel(x_ref, oa_ref, o1_ref, o4_ref, o16_ref, l1_ref, l4_ref, l16_ref,
                wo_ref, g2_ref, w1_ref, w2_ref, gf_ref, y_ref, osc_ref, lsc_ref, ob_ref, *, ff_chunk):
    tm = x_ref.shape[0]
    npair = DSA_HEADS // 2
    lane = lax.broadcasted_iota(jnp.int32, (tm, LANES), 1)
    first_half = lane < DSA_DH

    @pl.when(pl.program_id(0) == 0)
    def _():
        ob_ref[...] = jnp.zeros_like(ob_ref)

    mixed = _dot(oa_ref[...], wo_ref[:GLA_WIDTH, :]) + _dot(ob_ref[...], wo_ref[GLA_WIDTH:, :])
    h = x_ref[...] + mixed
    nm = (h * lax.rsqrt(jnp.mean(h * h, axis=-1, keepdims=True) + EPS) * g2_ref[...]).astype(BF16)
    ff = None
    for c in range(D_FF // ff_chunk):
        cols = slice(c * ff_chunk, (c + 1) * ff_chunk)
        a = jnp.maximum(_dot(nm, w1_ref[:, cols]), 0.0)
        d = _dot((a * a).astype(BF16), w2_ref[cols, :])
        ff = d if ff is None else ff + d
    h = h + ff
    y_ref[...] = h * lax.rsqrt(jnp.mean(h * h, axis=-1, keepdims=True) + EPS) * gf_ref[...]

    for j, (d, o_ref, l_ref) in enumerate(((4, o4_ref, l4_ref), (16, o16_ref, l16_ref))):
        for r in range(d):
            lsc_ref[j, pl.ds(r, tm // d, stride=d), :] = l_ref[r]
            for hp in range(npair):
                osc_ref[j * npair + hp, pl.ds(r, tm // d, stride=d), :] = (
                    o_ref[r, :, hp * LANES:(hp + 1) * LANES].astype(F32))

    l1, l2, l3 = l1_ref[...], lsc_ref[0], lsc_ref[1]
    m = jnp.maximum(jnp.maximum(l1, l2), l3)
    e1, e2, e3 = jnp.exp(l1 - m), jnp.exp(l2 - m), jnp.exp(l3 - m)
    inv = 1.0 / (e1 + e2 + e3)
    ws = (e1 * inv, e2 * inv, e3 * inv)

    for hp in range(npair):
        grp = slice(hp * LANES, (hp + 1) * LANES)
        branch_o = (o1_ref[:, grp].astype(F32), osc_ref[hp], osc_ref[npair + hp])
        acc = jnp.zeros((tm, LANES), F32)
        for w, o in zip(ws, branch_o):
            wa = jnp.sum(jnp.where(lane == 2 * hp, w, 0.0), axis=-1, keepdims=True)
            wb = jnp.sum(jnp.where(lane == 2 * hp + 1, w, 0.0), axis=-1, keepdims=True)
            acc = acc + jnp.where(first_half, wa, wb) * o
        ob_ref[:, grp] = acc.astype(BF16)


def _out(x2, oa, os_, ls, wo, g2, w1, w2, gf, *, seq, tm, ff_chunk):
    T = x2.shape[0]
    spb = seq // tm
    nt = T // tm
    prev = lambda s: jnp.maximum(s - 1, 0)
    cur = lambda s: jnp.minimum(s, nt - 1)
    row_prev = lambda n: pl.BlockSpec((tm, n), lambda s: (prev(s), 0))
    row_cur = lambda n: pl.BlockSpec((tm, n), lambda s: (cur(s), 0))
    strided = lambda d, n: pl.BlockSpec((None, d, tm // d, n), lambda s: (cur(s) // spb, 0, cur(s) % spb, 0))
    const = lambda a: pl.BlockSpec(a.shape, lambda s: (0,) * a.ndim, pipeline_mode=pl.Buffered(1))
    W = DSA_WIDTH
    return pl.pallas_call(
        functools.partial(_out_kernel, ff_chunk=ff_chunk),
        grid=(nt + 1,),
        in_specs=[row_prev(D_MODEL), row_prev(GLA_WIDTH), row_cur(W), strided(4, W), strided(16, W),
                  row_cur(LANES), strided(4, LANES), strided(16, LANES),
                  const(wo), const(g2), const(w1), const(w2), const(gf)],
        out_specs=row_prev(D_MODEL),
        out_shape=jax.ShapeDtypeStruct((T, D_MODEL), F32),
        scratch_shapes=[pltpu.VMEM((2 * (W // LANES), tm, LANES), F32),
                        pltpu.VMEM((2, tm, LANES), F32),
                        pltpu.VMEM((tm, W), BF16)],
        compiler_params=pltpu.CompilerParams(dimension_semantics=("arbitrary",),
                                             vmem_limit_bytes=VMEM_LIMIT),
        name="out_mlp",
    )(x2, oa, *os_, *ls, wo, g2, w1, w2, gf)


def kernel(x, attn_norm_g, w_in, gla_gate_w2, gla_gate_b, gla_norm_g, rel_bias, w_out, mlp_norm_g,
           w_ff1, w_ff2, final_norm_g):
    batch, seq, _ = x.shape
    assert seq % (DSA_PATTERN[-1][1] * DSA_BLOCK) == 0
    T = batch * seq
    x2 = x.reshape(T, D_MODEL)

    w = w_in[0]
    splits = np.cumsum([GLA_QK, GLA_QK, GLA_WIDTH, GLA_WIDTH, GLA_RANK, DSA_WIDTH, DSA_WIDTH, DSA_WIDTH])[:-1]
    wq, wk, wv, wr, wg, wdq, wdk, wdv = jnp.split(w, [int(s) for s in splits], axis=1)
    wq = (wq * GLA_DK ** -0.5).astype(BF16)
    wk = wk.astype(BF16)
    wv = wv.astype(BF16)
    wr = wr.astype(BF16)
    wg = jnp.pad(wg, ((0, 0), (0, RANK_PAD - GLA_RANK))).astype(BF16)
    w2 = jnp.pad(gla_gate_w2[0], ((0, RANK_PAD - GLA_RANK), (0, 0))).astype(BF16)
    gb = gla_gate_b[0].astype(F32).reshape(1, GLA_QK)
    wd = jnp.concatenate([wdq * DSA_DH ** -0.5, wdk, wdv], axis=1).astype(BF16)

    gq, gv, gr, d1, kt, glog, glogt, d4, d16 = _proj(
        x2, attn_norm_g[0].reshape(1, D_MODEL).astype(F32), wq, wv, wr, wk, wg, w2, gb, wd,
        batch=batch, seq=seq, tm=512)

    o_a = _gla(gq, kt, gv, gr, glog, glogt, gla_norm_g[0].reshape(1, GLA_WIDTH).astype(F32),
               batch=batch, seq=seq, tg=512)

    bias = _bias_tables(rel_bias)
    os_, ls = [], []
    for branch, xd in enumerate((d1.reshape(batch, 1, seq, 3 * DSA_WIDTH), d4, d16)):
        o, lse = _dsa_branch(xd, bias, branch, tr_max=512)
        os_.append(o)
        ls.append(lse)
    os_[0] = os_[0].reshape(T, DSA_WIDTH)
    ls[0] = ls[0].reshape(T, LANES)

    y = _out(x2, o_a, os_, ls, w_out[0].astype(BF16), mlp_norm_g[0].reshape(1, D_MODEL).astype(F32),
             w_ff1[0].astype(BF16), w_ff2[0].astype(BF16), final_norm_g.reshape(1, D_MODEL).astype(F32),
             seq=seq, tm=512, ff_chunk=1024)
    return y.reshape(batch, seq, D_MODEL)
```

```python
import functools
import math

import numpy as np
import jax
import jax.numpy as jnp
from jax import lax
from jax.experimental import pallas as pl
from jax.experimental.pallas import tpu as pltpu

D_MODEL = 1024
GLA_WIDTH = 512
GLA_HEADS = 4
GLA_DK = 64
GLA_DV = 128
GLA_QK = GLA_HEADS * GLA_DK
GLA_RANK = 16
GLA_TAU = 16.0
GLA_CHUNK = 64
DSA_WIDTH = 512
DSA_HEADS = 8
DSA_DH = 64
DSA_PATTERN = ((128, 1), (512, 4), (2048, 16))
DSA_BLOCK = 128
REL_BUCKETS = 32
REL_MAX_DIST = 2048
D_FF = 4096
EPS = 1e-6
NEG = -1e30
LOG2E = math.log2(math.e)

LANES = 128
RANK_PAD = LANES
VMEM_LIMIT = 56 * 1024 * 1024

F32 = jnp.float32
BF16 = jnp.bfloat16

_NT = (((1,), (1,)), ((), ()))


def _dot(a, b):
    return jnp.dot(a, b, preferred_element_type=F32)


def _dot_nt(a, b):
    return lax.dot_general(a, b, _NT, preferred_element_type=F32)


def _split_hi_lo(x):
    hi = x.astype(BF16)
    lo = (x - hi.astype(F32)).astype(BF16)
    return hi, lo


def _log_sigmoid(x):
    return jnp.minimum(x, 0.0) - jnp.log1p(jnp.exp(-jnp.abs(x)))


def _bucket_tables():
    max_exact = REL_BUCKETS // 2
    L = DSA_BLOCK
    steps = L + np.arange(L)[:, None] - np.arange(2 * L)[None, :]
    tables = []
    for window, dilation in DSA_PATTERN:
        span = window // dilation
        in_band = (steps >= 0) & (steps <= span)
        n = np.maximum(steps * dilation, 0)
        large = max_exact + (np.log(np.maximum(n, 1) / max_exact)
                             / math.log(REL_MAX_DIST / max_exact)
                             * (REL_BUCKETS - max_exact)).astype(np.int32)
        large = np.minimum(large, REL_BUCKETS - 1)
        bucket = np.where(n < max_exact, n, large).astype(np.int32)
        tables.append(np.where(in_band, bucket, -1).astype(np.int32))
    return np.stack(tables)


def _bias_kernel(rb_ref, bt_ref, out_ref):
    h = pl.program_id(1)
    bt = bt_ref[0]
    acc = jnp.full(bt.shape, NEG, F32)
    for b in range(REL_BUCKETS):
        acc = jnp.where(bt == b, rb_ref[b, h] * LOG2E, acc)
    out_ref[0, 0] = acc


def _bias_tables(rel_bias):
    nb = len(DSA_PATTERN)
    L = DSA_BLOCK
    return pl.pallas_call(
        _bias_kernel,
        grid=(nb, DSA_HEADS),
        in_specs=[pl.BlockSpec(memory_space=pltpu.SMEM),
                  pl.BlockSpec((1, L, 2 * L), lambda d, h: (d, 0, 0))],
        out_specs=pl.BlockSpec((1, 1, L, 2 * L), lambda d, h: (d, h, 0, 0)),
        out_shape=jax.ShapeDtypeStruct((nb, DSA_HEADS, L, 2 * L), F32),
        name="dsa_bias",
    )(rel_bias.astype(F32), jnp.asarray(_bucket_tables()))


def _proj_kernel(x_ref, g_ref, wq_ref, wv_ref, wr_ref, wk_ref, wg_ref, w2_ref, b_ref, wd_ref,
                 gq_ref, gv_ref, gr_ref, d1_ref, kt_ref, glog_ref, glogt_ref, d4_ref, d16_ref,
                 dsc_ref, t4_ref):
    tm = x_ref.shape[0]
    ncol = 3 * DSA_WIDTH // LANES
    n4 = tm // 4
    n16 = tm // 16

    x = x_ref[...]
    ms = jnp.mean(x * x, axis=-1, keepdims=True)
    nx = (x * lax.rsqrt(ms + EPS) * g_ref[...]).astype(BF16)

    kt_ref[0] = _dot(nx, wk_ref[...]).T.astype(BF16)
    glow = _dot(nx, wg_ref[...]).astype(BF16)
    glog = _log_sigmoid(_dot(glow, w2_ref[...]) + b_ref[...]) * (1.0 / GLA_TAU)
    glog_ref[...] = glog
    glogt_ref[0] = glog.T

    gq_ref[...] = _dot(nx, wq_ref[...]).astype(BF16)
    gv_ref[...] = _dot(nx, wv_ref[...]).astype(BF16)
    gr_ref[...] = _dot(nx, wr_ref[...]).astype(BF16)

    for j in range(ncol // 2):
        res = _dot(nx, wd_ref[:, 2 * j * LANES:(2 * j + 2) * LANES])
        d1_ref[:, 2 * j * LANES:(2 * j + 2) * LANES] = res.astype(BF16)
        dsc_ref[2 * j] = res[:, :LANES]
        dsc_ref[2 * j + 1] = res[:, LANES:]
        for c in (2 * j, 2 * j + 1):
            cols = slice(c * LANES, (c + 1) * LANES)
            for r4 in range(4):
                sub = dsc_ref[c, pl.ds(r4, n4, stride=4), :]
                d4_ref[r4, :, cols] = sub.astype(BF16)
                t4_ref[c, r4 * n4:(r4 + 1) * n4, :] = sub
            for r4 in range(4):
                for r2 in range(4):
                    d16_ref[r4 + 4 * r2, :, cols] = (
                        t4_ref[c, pl.ds(r4 * n4 + r2, n16, stride=4), :].astype(BF16))


def _proj(x2, g, wq, wv, wr, wk, wg, w2, b, wd, *, batch, seq, tm):
    T = batch * seq
    spb = seq // tm
    nt = T // tm
    full = lambda a: pl.BlockSpec(a.shape, lambda s: (0,) * a.ndim)
    row_cur = lambda n: pl.BlockSpec((tm, n), lambda s: (s, 0))
    row_prev = row_cur
    colt = pl.BlockSpec((1, GLA_QK, tm), lambda s: (s // spb, 0, s % spb))
    W3 = 3 * DSA_WIDTH
    strided = lambda d: pl.BlockSpec((None, d, tm // d, W3), lambda s: (s // spb, 0, s % spb, 0))
    return pl.pallas_call(
        _proj_kernel,
        grid=(nt,),
        in_specs=[row_cur(D_MODEL)] + [full(a) for a in (g, wq, wv, wr, wk, wg, w2, b, wd)],
        out_specs=[row_cur(GLA_QK), row_cur(GLA_WIDTH), row_cur(GLA_WIDTH), row_cur(W3),
                   colt, row_prev(GLA_QK), colt, strided(4), strided(16)],
        out_shape=[jax.ShapeDtypeStruct((T, GLA_QK), BF16),
                   jax.ShapeDtypeStruct((T, GLA_WIDTH), BF16),
                   jax.ShapeDtypeStruct((T, GLA_WIDTH), BF16),
                   jax.ShapeDtypeStruct((T, W3), BF16),
                   jax.ShapeDtypeStruct((batch, GLA_QK, seq), BF16),
                   jax.ShapeDtypeStruct((T, GLA_QK), F32),
                   jax.ShapeDtypeStruct((batch, GLA_QK, seq), F32),
                   jax.ShapeDtypeStruct((batch, 4, seq // 4, W3), BF16),
                   jax.ShapeDtypeStruct((batch, 16, seq // 16, W3), BF16)],
        scratch_shapes=[pltpu.VMEM((W3 // LANES, tm, LANES), F32)] * 2,
        compiler_params=pltpu.CompilerParams(dimension_semantics=("arbitrary",),
                                             vmem_limit_bytes=VMEM_LIMIT),
        name="proj",
    )(x2, g, wq, wv, wr, wk, wg, w2, b, wd)


def _gla_kernel(gq_ref, kt_ref, gv_ref, gr_ref, glog_ref, glogt_ref, gn_ref, o_ref, s_ref, *, pairs):
    C = GLA_CHUNK
    P = 2 * C

    @pl.when(pl.program_id(1) == 0)
    def _():
        s_ref[...] = jnp.zeros_like(s_ref)

    ri = lax.broadcasted_iota(jnp.int32, (P, P), 0)
    ci = lax.broadcasted_iota(jnp.int32, (P, P), 1)
    same_chunk = (ri < C) == (ci < C)
    causal = same_chunk & (ci <= ri)
    low = jnp.where(causal, 1.0, 0.0).astype(BF16)
    upp = jnp.where(same_chunk & (ri <= ci), 1.0, 0.0).astype(BF16)
    lane = lax.broadcasted_iota(jnp.int32, (P, LANES), 1)
    first_half = lane < C
    lane_t = lax.broadcasted_iota(jnp.int32, (GLA_QK, P), 1)
    first_t = lane_t < C

    for p in range(pairs):
        rows = slice(p * P, (p + 1) * P)
        g_hi, g_lo = _split_hi_lo(glog_ref[rows, :])
        b = _dot(low, g_hi) + _dot(low, g_lo)
        qd = (gq_ref[rows, :].astype(F32) * jnp.exp(b)).astype(BF16)

        gt = glogt_ref[0, :, rows]
        gt_hi, gt_lo = _split_hi_lo(gt)
        bt = _dot(gt_hi, upp) + _dot(gt_lo, upp)
        tot_a = jnp.sum(jnp.where(first_t, gt, 0.0), axis=-1, keepdims=True)
        tot_b = jnp.sum(jnp.where(first_t, 0.0, gt), axis=-1, keepdims=True)
        tot = jnp.where(first_t, tot_a, tot_b)
        kt = kt_ref[0, :, rows].astype(F32)
        kinv_t = (kt * jnp.exp(-bt)).astype(BF16)
        kend_t = kt * jnp.exp(tot - bt)
        kend_a = jnp.where(first_t, kend_t, 0.0).astype(BF16)
        kend_b = jnp.where(first_t, 0.0, kend_t).astype(BF16)
        dec_a = jnp.exp(tot_a)
        dec_b = jnp.exp(tot_b)

        for h in range(GLA_HEADS):
            hp = h // 2
            grp = slice(hp * LANES, (hp + 1) * LANES)
            own = first_half if h % 2 == 0 else ~first_half
            qm = jnp.where(own, qd[:, grp], jnp.zeros((), BF16))
            hk = slice(h * GLA_DK, (h + 1) * GLA_DK)
            hv = slice(h * GLA_DV, (h + 1) * GLA_DV)
            v = gv_ref[rows, hv]

            att = _dot(qm, kinv_t[grp, :])
            att = jnp.where(causal, att, 0.0).astype(BF16)
            o = _dot(att, v)

            s_pair = s_ref[grp, :]
            s_a = s_ref[hk, :]
            s_b = dec_a[hk, :] * s_a + _dot(kend_a[hk, :], v)
            s_ref[hk, :] = dec_b[hk, :] * s_b + _dot(kend_b[hk, :], v)
            if h % 2 == 0:
                s_pair_b = jnp.concatenate([s_b, s_pair[GLA_DK:, :]], axis=0)
            else:
                s_pair_b = jnp.concatenate([s_pair[:GLA_DK, :], s_b], axis=0)
            o_a = _dot(qm[:C, :], s_pair.astype(BF16))
            o_b = _dot(qm[C:, :], s_pair_b.astype(BF16))
            o = o + jnp.concatenate([o_a, o_b], axis=0)

            o = o * lax.rsqrt(jnp.mean(o * o, axis=-1, keepdims=True) + EPS) * gn_ref[:, hv]
            r = gr_ref[rows, hv].astype(F32)
            o_ref[rows, hv] = (o * (r * jax.nn.sigmoid(r))).astype(BF16)


def _gla(gq, kt, gv, gr, glog, glogt, gn, *, batch, seq, tg):
    T = batch * seq
    spb = seq // tg
    row = lambda n: pl.BlockSpec((tg, n), lambda b, i: (b * spb + i, 0))
    colt = pl.BlockSpec((1, GLA_QK, tg), lambda b, i: (b, 0, i))
    return pl.pallas_call(
        functools.partial(_gla_kernel, pairs=tg // (2 * GLA_CHUNK)),
        grid=(batch, spb),
        in_specs=[row(GLA_QK), colt, row(GLA_WIDTH), row(GLA_WIDTH), row(GLA_QK), colt,
                  pl.BlockSpec((1, GLA_WIDTH), lambda b, i: (0, 0))],
        out_specs=row(GLA_WIDTH),
        out_shape=jax.ShapeDtypeStruct((T, GLA_WIDTH), BF16),
        scratch_shapes=[pltpu.VMEM((GLA_QK, GLA_DV), F32)],
        compiler_params=pltpu.CompilerParams(dimension_semantics=("arbitrary", "arbitrary"),
                                             vmem_limit_bytes=VMEM_LIMIT),
        name="gla",
    )(gq, kt, gv, gr, glog, glogt, gn)


def _kt_kernel(k_ref, o_ref, *, blocks):
    L = DSA_BLOCK
    for j in range(blocks):
        for c in range(DSA_WIDTH // LANES):
            cols = slice(c * LANES, (c + 1) * LANES)
            o_ref[j, cols, :] = k_ref[j * L:(j + 1) * L, cols].astype(F32).T.astype(BF16)


def _key_transpose(x, *, tr_max):
    batch, d, n, _ = x.shape
    tr = min(tr_max, n)
    hb = tr // DSA_BLOCK
    W = DSA_WIDTH
    return pl.pallas_call(
        functools.partial(_kt_kernel, blocks=hb),
        grid=(batch, d, n // tr),
        in_specs=[pl.BlockSpec((None, None, tr, W), lambda b, r, i: (b, r, i, 1))],
        out_specs=pl.BlockSpec((None, hb, None, W, DSA_BLOCK), lambda b, r, i: (b, i, r, 0, 0)),
        out_shape=jax.ShapeDtypeStruct((batch, n // DSA_BLOCK, d, W, DSA_BLOCK), BF16),
        compiler_params=pltpu.CompilerParams(dimension_semantics=("arbitrary",) * 3,
                                             vmem_limit_bytes=VMEM_LIMIT),
        name=f"dsa_kt_d{d}",
    )(x)


def _dsa_kernel(q_ref, kt_ref, v_ref, kth_ref, vh_ref, bias_ref, o_ref, st_ref, *, blocks):
    L = DSA_BLOCK
    first_tile = pl.program_id(2) == 0
    lane = lax.broadcasted_iota(jnp.int32, (L, LANES), 1)
    first_half = lane < DSA_DH
    prev_cols = lax.broadcasted_iota(jnp.int32, (L, 2 * L), 1) < L

    for blk in range(blocks):
        rows = slice(blk * L, (blk + 1) * L)
        st_tile = jnp.zeros((L, LANES), F32)
        for hp in range(DSA_HEADS // 2):
            grp = slice(hp * LANES, (hp + 1) * LANES)
            qp = q_ref[rows, grp]
            if blk == 0:
                kcat_t = jnp.concatenate([kth_ref[grp, :], kt_ref[0, grp, :]], axis=1)
                vcat = jnp.concatenate([vh_ref[:, grp], v_ref[rows, grp]], axis=0)
            else:
                kcat_t = jnp.concatenate([kt_ref[blk - 1, grp, :], kt_ref[blk, grp, :]], axis=1)
                vcat = v_ref[(blk - 1) * L:(blk + 1) * L, grp]
            outs = []
            for hh in range(2):
                h = 2 * hp + hh
                own = first_half if hh == 0 else ~first_half
                qm = jnp.where(own, qp, jnp.zeros((), BF16))
                bias = bias_ref[h]
                if blk == 0:
                    bias = jnp.where(prev_cols & first_tile, NEG, bias)
                s = _dot(qm, kcat_t) + bias
                m = jnp.max(s, axis=-1, keepdims=True)
                p = jnp.exp2(s - m)
                den = jnp.sum(p, axis=-1, keepdims=True)
                outs.append(_dot(p.astype(BF16), vcat))
                st_tile = jnp.where(lane == h, m, jnp.where(lane == DSA_HEADS + h, den, st_tile))
            o_ref[rows, grp] = jnp.where(first_half, outs[0], outs[1]).astype(BF16)
        st_ref[rows, :] = st_tile


def _dsa_branch(x, bias, branch, *, tr_max):
    batch, d, n, _ = x.shape
    tr = min(tr_max, n)
    hb = tr // DSA_BLOCK
    W = DSA_WIDTH
    kt = _key_transpose(x, tr_max=tr_max)
    prev_blk = lambda i: jnp.maximum(i * hb - 1, 0)
    blk = lambda part: pl.BlockSpec((None, None, tr, W), lambda b, r, i: (b, r, i, part))
    return pl.pallas_call(
        functools.partial(_dsa_kernel, blocks=hb),
        grid=(batch, d, n // tr),
        in_specs=[blk(0),
                  pl.BlockSpec((None, hb, None, W, DSA_BLOCK), lambda b, r, i: (b, i, r, 0, 0)),
                  blk(2),
                  pl.BlockSpec((None, None, None, W, DSA_BLOCK), lambda b, r, i: (b, prev_blk(i), r, 0, 0)),
                  pl.BlockSpec((None, None, DSA_BLOCK, W), lambda b, r, i: (b, r, prev_blk(i), 2)),
                  pl.BlockSpec((None, DSA_HEADS, DSA_BLOCK, 2 * DSA_BLOCK), lambda b, r, i: (branch, 0, 0, 0))],
        out_specs=[pl.BlockSpec((None, None, tr, W), lambda b, r, i: (b, r, i, 0)),
                   pl.BlockSpec((None, None, tr, LANES), lambda b, r, i: (b, r, i, 0))],
        out_shape=[jax.ShapeDtypeStruct((batch, d, n, W), BF16),
                   jax.ShapeDtypeStruct((batch, d, n, LANES), F32)],
        compiler_params=pltpu.CompilerParams(dimension_semantics=("arbitrary",) * 3,
                                             vmem_limit_bytes=VMEM_LIMIT),
        name=f"dsa_d{d}",
    )(x, kt, x, kt, x, bias)


def _out_kernel(x_ref, oa_ref, o1_ref, o4_ref, o16_ref, l1_ref, l4_ref, l16_ref,
                wo_ref, g2_ref, w1_ref, w2_ref, gf_ref, y_ref, osc_ref, lsc_ref, ob_ref, *, ff_chunk):
    tm = x_ref.shape[0]
    npair = DSA_HEADS // 2
    lane = lax.broadcasted_iota(jnp.int32, (tm, LANES), 1)
    first_half = lane < DSA_DH

    @pl.when(pl.program_id(0) == 0)
    def _():
        ob_ref[...] = jnp.zeros_like(ob_ref)

    mixed = _dot(oa_ref[...], wo_ref[:GLA_WIDTH, :]) + _dot(ob_ref[...], wo_ref[GLA_WIDTH:, :])
    h = x_ref[...] + mixed
    nm = (h * lax.rsqrt(jnp.mean(h * h, axis=-1, keepdims=True) + EPS) * g2_ref[...]).astype(BF16)
    ff = None
    for c in range(D_FF // ff_chunk):
        cols = slice(c * ff_chunk, (c + 1) * ff_chunk)
        a = jnp.maximum(_dot(nm, w1_ref[:, cols]), 0.0)
        d = _dot((a * a).astype(BF16), w2_ref[cols, :])
        ff = d if ff is None else ff + d
    h = h + ff
    y_ref[...] = h * lax.rsqrt(jnp.mean(h * h, axis=-1, keepdims=True) + EPS) * gf_ref[...]

    for j, (d, o_ref, l_ref) in enumerate(((4, o4_ref, l4_ref), (16, o16_ref, l16_ref))):
        for r in range(d):
            lsc_ref[j, pl.ds(r, tm // d, stride=d), :] = l_ref[r]
            for hp in range(npair):
                osc_ref[j * npair + hp, pl.ds(r, tm // d, stride=d), :] = (
                    o_ref[r, :, hp * LANES:(hp + 1) * LANES].astype(F32))

    sts = (l1_ref[...], lsc_ref[0], lsc_ref[1])
    m = jnp.maximum(jnp.maximum(sts[0], sts[1]), sts[2])
    es = [jnp.exp2(st - m) for st in sts]
    total = sum(e * pltpu.roll(st, LANES - DSA_HEADS, axis=1) for e, st in zip(es, sts))
    inv = 1.0 / jnp.where(lane < DSA_HEADS, total, 1.0)
    ws = [e * inv for e in es]

    for hp in range(npair):
        grp = slice(hp * LANES, (hp + 1) * LANES)
        branch_o = (o1_ref[:, grp].astype(F32), osc_ref[hp], osc_ref[npair + hp])
        acc = jnp.zeros((tm, LANES), F32)
        for w, o in zip(ws, branch_o):
            wa = jnp.sum(jnp.where(lane == 2 * hp, w, 0.0), axis=-1, keepdims=True)
            wb = jnp.sum(jnp.where(lane == 2 * hp + 1, w, 0.0), axis=-1, keepdims=True)
            acc = acc + jnp.where(first_half, wa, wb) * o
        ob_ref[:, grp] = acc.astype(BF16)


def _out(x2, oa, os_, ls, wo, g2, w1, w2, gf, *, seq, tm, ff_chunk):
    T = x2.shape[0]
    spb = seq // tm
    nt = T // tm
    prev = lambda s: jnp.maximum(s - 1, 0)
    cur = lambda s: jnp.minimum(s, nt - 1)
    row_prev = lambda n: pl.BlockSpec((tm, n), lambda s: (prev(s), 0))
    row_cur = lambda n: pl.BlockSpec((tm, n), lambda s: (cur(s), 0))
    strided = lambda d, n: pl.BlockSpec((None, d, tm // d, n), lambda s: (cur(s) // spb, 0, cur(s) % spb, 0))
    const = lambda a: pl.BlockSpec(a.shape, lambda s: (0,) * a.ndim, pipeline_mode=pl.Buffered(1))
    W = DSA_WIDTH
    return pl.pallas_call(
        functools.partial(_out_kernel, ff_chunk=ff_chunk),
        grid=(nt + 1,),
        in_specs=[row_prev(D_MODEL), row_prev(GLA_WIDTH), row_cur(W), strided(4, W), strided(16, W),
                  row_cur(LANES), strided(4, LANES), strided(16, LANES),
                  const(wo), const(g2), const(w1), const(w2), const(gf)],
        out_specs=row_prev(D_MODEL),
        out_shape=jax.ShapeDtypeStruct((T, D_MODEL), F32),
        scratch_shapes=[pltpu.VMEM((2 * (W // LANES), tm, LANES), F32),
                        pltpu.VMEM((2, tm, LANES), F32),
                        pltpu.VMEM((tm, W), BF16)],
        compiler_params=pltpu.CompilerParams(dimension_semantics=("arbitrary",),
                                             vmem_limit_bytes=VMEM_LIMIT),
        name="out_mlp",
    )(x2, oa, *os_, *ls, wo, g2, w1, w2, gf)


def kernel(x, attn_norm_g, w_in, gla_gate_w2, gla_gate_b, gla_norm_g, rel_bias, w_out, mlp_norm_g,
           w_ff1, w_ff2, final_norm_g):
    batch, seq, _ = x.shape
    assert seq % (DSA_PATTERN[-1][1] * DSA_BLOCK) == 0
    T = batch * seq
    x2 = x.reshape(T, D_MODEL)

    w = w_in[0]
    splits = np.cumsum([GLA_QK, GLA_QK, GLA_WIDTH, GLA_WIDTH, GLA_RANK, DSA_WIDTH, DSA_WIDTH, DSA_WIDTH])[:-1]
    wq, wk, wv, wr, wg, wdq, wdk, wdv = jnp.split(w, [int(s) for s in splits], axis=1)
    wq = (wq * GLA_DK ** -0.5).astype(BF16)
    wk = wk.astype(BF16)
    wv = wv.astype(BF16)
    wr = wr.astype(BF16)
    wg = jnp.pad(wg, ((0, 0), (0, RANK_PAD - GLA_RANK))).astype(BF16)
    w2 = jnp.pad(gla_gate_w2[0], ((0, RANK_PAD - GLA_RANK), (0, 0))).astype(BF16)
    gb = gla_gate_b[0].astype(F32).reshape(1, GLA_QK)
    wd = jnp.concatenate([wdq * (DSA_DH ** -0.5 * LOG2E), wdk, wdv], axis=1).astype(BF16)

    gq, gv, gr, d1, kt, glog, glogt, d4, d16 = _proj(
        x2, attn_norm_g[0].reshape(1, D_MODEL).astype(F32), wq, wv, wr, wk, wg, w2, gb, wd,
        batch=batch, seq=seq, tm=512)

    o_a = _gla(gq, kt, gv, gr, glog, glogt, gla_norm_g[0].reshape(1, GLA_WIDTH).astype(F32),
               batch=batch, seq=seq, tg=512)

    bias = _bias_tables(rel_bias)
    os_, ls = [], []
    for branch, xd in enumerate((d1.reshape(batch, 1, seq, 3 * DSA_WIDTH), d4, d16)):
        o, lse = _dsa_branch(xd, bias, branch, tr_max=512)
        os_.append(o)
        ls.append(lse)
    os_[0] = os_[0].reshape(T, DSA_WIDTH)
    ls[0] = ls[0].reshape(T, LANES)

    y = _out(x2, o_a, os_, ls, w_out[0].astype(BF16), mlp_norm_g[0].reshape(1, D_MODEL).astype(F32),
             w_ff1[0].astype(BF16), w_ff2[0].astype(BF16), final_norm_g.reshape(1, D_MODEL).astype(F32),
             seq=seq, tm=512, ff_chunk=1024)
    return y.reshape(batch, seq, D_MODEL)
```

```python
import functools
import math

import numpy as np
import jax
import jax.numpy as jnp
from jax import lax
from jax.experimental import pallas as pl
from jax.experimental.pallas import tpu as pltpu

D_MODEL = 1024
GLA_WIDTH = 512
GLA_HEADS = 4
GLA_DK = 64
GLA_DV = 128
GLA_QK = GLA_HEADS * GLA_DK
GLA_RANK = 16
GLA_TAU = 16.0
GLA_CHUNK = 64
DSA_WIDTH = 512
DSA_HEADS = 8
DSA_DH = 64
DSA_PATTERN = ((128, 1), (512, 4), (2048, 16))
DSA_BLOCK = 128
REL_BUCKETS = 32
REL_MAX_DIST = 2048
D_FF = 4096
EPS = 1e-6
NEG = -1e30
LOG2E = math.log2(math.e)

LANES = 128
RANK_PAD = LANES
VMEM_LIMIT = 56 * 1024 * 1024

F32 = jnp.float32
BF16 = jnp.bfloat16

_NT = (((1,), (1,)), ((), ()))


def _dot(a, b):
    return jnp.dot(a, b, preferred_element_type=F32)


def _dot_nt(a, b):
    return lax.dot_general(a, b, _NT, preferred_element_type=F32)


def _split_hi_lo(x):
    hi = x.astype(BF16)
    lo = (x - hi.astype(F32)).astype(BF16)
    return hi, lo


def _log_sigmoid(x):
    return jnp.minimum(x, 0.0) - jnp.log1p(jnp.exp(-jnp.abs(x)))


def _bucket_tables():
    max_exact = REL_BUCKETS // 2
    L = DSA_BLOCK
    steps = L + np.arange(L)[:, None] - np.arange(2 * L)[None, :]
    tables = []
    for window, dilation in DSA_PATTERN:
        span = window // dilation
        in_band = (steps >= 0) & (steps <= span)
        n = np.maximum(steps * dilation, 0)
        large = max_exact + (np.log(np.maximum(n, 1) / max_exact)
                             / math.log(REL_MAX_DIST / max_exact)
                             * (REL_BUCKETS - max_exact)).astype(np.int32)
        large = np.minimum(large, REL_BUCKETS - 1)
        bucket = np.where(n < max_exact, n, large).astype(np.int32)
        tables.append(np.where(in_band, bucket, -1).astype(np.int32))
    return np.stack(tables)


def _bias_kernel(rb_ref, bt_ref, out_ref):
    h = pl.program_id(1)
    bt = bt_ref[0]
    acc = jnp.full(bt.shape, NEG, F32)
    for b in range(REL_BUCKETS):
        acc = jnp.where(bt == b, rb_ref[b, h] * LOG2E, acc)
    out_ref[0, 0] = acc


def _bias_tables(rel_bias):
    nb = len(DSA_PATTERN)
    L = DSA_BLOCK
    return pl.pallas_call(
        _bias_kernel,
        grid=(nb, DSA_HEADS),
        in_specs=[pl.BlockSpec(memory_space=pltpu.SMEM),
                  pl.BlockSpec((1, L, 2 * L), lambda d, h: (d, 0, 0))],
        out_specs=pl.BlockSpec((1, 1, L, 2 * L), lambda d, h: (d, h, 0, 0)),
        out_shape=jax.ShapeDtypeStruct((nb, DSA_HEADS, L, 2 * L), F32),
        name="dsa_bias",
    )(rel_bias.astype(F32), jnp.asarray(_bucket_tables()))


def _proj_kernel(x_ref, g_ref, wq_ref, wv_ref, wr_ref, wk_ref, wg_ref, w2_ref, b_ref, wd_ref,
                 gq_ref, gv_ref, gr_ref, d1_ref, kt_ref, glog_ref, glogt_ref, d4_ref, d16_ref,
                 kt1_ref, kt4_ref, dsc_ref, t4_ref):
    tm = x_ref.shape[0]
    ncol = 3 * DSA_WIDTH // LANES
    kcols = range(DSA_WIDTH // LANES, 2 * DSA_WIDTH // LANES)
    n4 = tm // 4
    n16 = tm // 16

    x = x_ref[...]
    ms = jnp.mean(x * x, axis=-1, keepdims=True)
    nx = (x * lax.rsqrt(ms + EPS) * g_ref[...]).astype(BF16)

    kt_ref[0] = _dot(nx, wk_ref[...]).T.astype(BF16)
    glow = _dot(nx, wg_ref[...]).astype(BF16)
    glog = _log_sigmoid(_dot(glow, w2_ref[...]) + b_ref[...]) * (1.0 / GLA_TAU)
    glog_ref[...] = glog
    glogt_ref[0] = glog.T

    gq_ref[...] = _dot(nx, wq_ref[...]).astype(BF16)
    gv_ref[...] = _dot(nx, wv_ref[...]).astype(BF16)
    gr_ref[...] = _dot(nx, wr_ref[...]).astype(BF16)

    for j in range(ncol // 2):
        res = _dot(nx, wd_ref[:, 2 * j * LANES:(2 * j + 2) * LANES])
        d1_ref[:, 2 * j * LANES:(2 * j + 2) * LANES] = res.astype(BF16)
        dsc_ref[2 * j] = res[:, :LANES]
        dsc_ref[2 * j + 1] = res[:, LANES:]
        for c in (2 * j, 2 * j + 1):
            cols = slice(c * LANES, (c + 1) * LANES)
            krows = slice((c - kcols[0]) * LANES, (c - kcols[0] + 1) * LANES)
            if c in kcols:
                for jb in range(tm // DSA_BLOCK):
                    blk_rows = slice(jb * DSA_BLOCK, (jb + 1) * DSA_BLOCK)
                    kt1_ref[jb, krows, :] = dsc_ref[c, blk_rows, :].T.astype(BF16)
            for r4 in range(4):
                sub = dsc_ref[c, pl.ds(r4, n4, stride=4), :]
                d4_ref[r4, :, cols] = sub.astype(BF16)
                t4_ref[c, r4 * n4:(r4 + 1) * n4, :] = sub
                if c in kcols:
                    kt4_ref[r4, krows, :] = sub.T.astype(BF16)
            for r4 in range(4):
                for r2 in range(4):
                    d16_ref[r4 + 4 * r2, :, cols] = (
                        t4_ref[c, pl.ds(r4 * n4 + r2, n16, stride=4), :].astype(BF16))


def _proj(x2, g, wq, wv, wr, wk, wg, w2, b, wd, *, batch, seq, tm):
    T = batch * seq
    spb = seq // tm
    nt = T // tm
    full = lambda a: pl.BlockSpec(a.shape, lambda s: (0,) * a.ndim)
    row_cur = lambda n: pl.BlockSpec((tm, n), lambda s: (s, 0))
    row_prev = row_cur
    colt = pl.BlockSpec((1, GLA_QK, tm), lambda s: (s // spb, 0, s % spb))
    W3 = 3 * DSA_WIDTH
    strided = lambda d: pl.BlockSpec((None, d, tm // d, W3), lambda s: (s // spb, 0, s % spb, 0))
    assert tm == 4 * DSA_BLOCK
    W, L = DSA_WIDTH, DSA_BLOCK
    kt1_spec = pl.BlockSpec((None, tm // L, None, W, L), lambda s: (s // spb, s % spb, 0, 0, 0))
    kt4_spec = pl.BlockSpec((None, None, 4, W, L), lambda s: (s // spb, s % spb, 0, 0, 0))
    return pl.pallas_call(
        _proj_kernel,
        grid=(nt,),
        in_specs=[row_cur(D_MODEL)] + [full(a) for a in (g, wq, wv, wr, wk, wg, w2, b, wd)],
        out_specs=[row_cur(GLA_QK), row_cur(GLA_WIDTH), row_cur(GLA_WIDTH), row_cur(W3),
                   colt, row_prev(GLA_QK), colt, strided(4), strided(16), kt1_spec, kt4_spec],
        out_shape=[jax.ShapeDtypeStruct((T, GLA_QK), BF16),
                   jax.ShapeDtypeStruct((T, GLA_WIDTH), BF16),
                   jax.ShapeDtypeStruct((T, GLA_WIDTH), BF16),
                   jax.ShapeDtypeStruct((T, W3), BF16),
                   jax.ShapeDtypeStruct((batch, GLA_QK, seq), BF16),
                   jax.ShapeDtypeStruct((T, GLA_QK), F32),
                   jax.ShapeDtypeStruct((batch, GLA_QK, seq), F32),
                   jax.ShapeDtypeStruct((batch, 4, seq // 4, W3), BF16),
                   jax.ShapeDtypeStruct((batch, 16, seq // 16, W3), BF16),
                   jax.ShapeDtypeStruct((batch, seq // L, 1, W, L), BF16),
                   jax.ShapeDtypeStruct((batch, seq // (4 * L), 4, W, L), BF16)],
        scratch_shapes=[pltpu.VMEM((W3 // LANES, tm, LANES), F32)] * 2,
        compiler_params=pltpu.CompilerParams(dimension_semantics=("arbitrary",),
                                             vmem_limit_bytes=VMEM_LIMIT),
        name="proj",
    )(x2, g, wq, wv, wr, wk, wg, w2, b, wd)


def _gla_kernel(gq_ref, kt_ref, gv_ref, gr_ref, glog_ref, glogt_ref, gn_ref, o_ref, s_ref, *, pairs):
    C = GLA_CHUNK
    P = 2 * C

    @pl.when(pl.program_id(1) == 0)
    def _():
        s_ref[...] = jnp.zeros_like(s_ref)

    ri = lax.broadcasted_iota(jnp.int32, (P, P), 0)
    ci = lax.broadcasted_iota(jnp.int32, (P, P), 1)
    same_chunk = (ri < C) == (ci < C)
    causal = same_chunk & (ci <= ri)
    low = jnp.where(causal, 1.0, 0.0).astype(BF16)
    upp = jnp.where(same_chunk & (ri <= ci), 1.0, 0.0).astype(BF16)
    lane = lax.broadcasted_iota(jnp.int32, (P, LANES), 1)
    first_half = lane < C
    lane_t = lax.broadcasted_iota(jnp.int32, (GLA_QK, P), 1)
    first_t = lane_t < C

    for p in range(pairs):
        rows = slice(p * P, (p + 1) * P)
        g_hi, g_lo = _split_hi_lo(glog_ref[rows, :])
        b = _dot(low, g_hi) + _dot(low, g_lo)
        qd = (gq_ref[rows, :].astype(F32) * jnp.exp(b)).astype(BF16)

        gt = glogt_ref[0, :, rows]
        gt_hi, gt_lo = _split_hi_lo(gt)
        bt = _dot(gt_hi, upp) + _dot(gt_lo, upp)
        tot_a = jnp.sum(jnp.where(first_t, gt, 0.0), axis=-1, keepdims=True)
        tot_b = jnp.sum(jnp.where(first_t, 0.0, gt), axis=-1, keepdims=True)
        tot = jnp.where(first_t, tot_a, tot_b)
        kt = kt_ref[0, :, rows].astype(F32)
        kinv_t = (kt * jnp.exp(-bt)).astype(BF16)
        kend_t = kt * jnp.exp(tot - bt)
        kend_a = jnp.where(first_t, kend_t, 0.0).astype(BF16)
        kend_b = jnp.where(first_t, 0.0, kend_t).astype(BF16)
        dec_a = jnp.exp(tot_a)
        dec_b = jnp.exp(tot_b)

        for h in range(GLA_HEADS):
            hp = h // 2
            grp = slice(hp * LANES, (hp + 1) * LANES)
            own = first_half if h % 2 == 0 else ~first_half
            qm = jnp.where(own, qd[:, grp], jnp.zeros((), BF16))
            hk = slice(h * GLA_DK, (h + 1) * GLA_DK)
            hv = slice(h * GLA_DV, (h + 1) * GLA_DV)
            v = gv_ref[rows, hv]

            att = _dot(qm, kinv_t[grp, :])
            att = jnp.where(causal, att, 0.0).astype(BF16)
            o = _dot(att, v)

            s_pair = s_ref[grp, :]
            s_a = s_ref[hk, :]
            s_b = dec_a[hk, :] * s_a + _dot(kend_a[hk, :], v)
            s_ref[hk, :] = dec_b[hk, :] * s_b + _dot(kend_b[hk, :], v)
            if h % 2 == 0:
                s_pair_b = jnp.concatenate([s_b, s_pair[GLA_DK:, :]], axis=0)
            else:
                s_pair_b = jnp.concatenate([s_pair[:GLA_DK, :], s_b], axis=0)
            o_a = _dot(qm[:C, :], s_pair.astype(BF16))
            o_b = _dot(qm[C:, :], s_pair_b.astype(BF16))
            o = o + jnp.concatenate([o_a, o_b], axis=0)

            o = o * lax.rsqrt(jnp.mean(o * o, axis=-1, keepdims=True) + EPS) * gn_ref[:, hv]
            r = gr_ref[rows, hv].astype(F32)
            o_ref[rows, hv] = (o * (r * jax.nn.sigmoid(r))).astype(BF16)


def _gla(gq, kt, gv, gr, glog, glogt, gn, *, batch, seq, tg):
    T = batch * seq
    spb = seq // tg
    row = lambda n: pl.BlockSpec((tg, n), lambda b, i: (b * spb + i, 0))
    colt = pl.BlockSpec((1, GLA_QK, tg), lambda b, i: (b, 0, i))
    return pl.pallas_call(
        functools.partial(_gla_kernel, pairs=tg // (2 * GLA_CHUNK)),
        grid=(batch, spb),
        in_specs=[row(GLA_QK), colt, row(GLA_WIDTH), row(GLA_WIDTH), row(GLA_QK), colt,
                  pl.BlockSpec((1, GLA_WIDTH), lambda b, i: (0, 0))],
        out_specs=row(GLA_WIDTH),
        out_shape=jax.ShapeDtypeStruct((T, GLA_WIDTH), BF16),
        scratch_shapes=[pltpu.VMEM((GLA_QK, GLA_DV), F32)],
        compiler_params=pltpu.CompilerParams(dimension_semantics=("arbitrary", "arbitrary"),
                                             vmem_limit_bytes=VMEM_LIMIT),
        name="gla",
    )(gq, kt, gv, gr, glog, glogt, gn)


def _dsa_kernel(q_ref, k_ref, v_ref, kh_ref, vh_ref, bias_ref, o_ref, st_ref, *, blocks, keys_transposed):
    L = DSA_BLOCK
    first_tile = pl.program_id(2) == 0
    lane = lax.broadcasted_iota(jnp.int32, (L, LANES), 1)
    first_half = lane < DSA_DH
    prev_cols = lax.broadcasted_iota(jnp.int32, (L, 2 * L), 1) < L

    for blk in range(blocks):
        rows = slice(blk * L, (blk + 1) * L)
        both = slice((blk - 1) * L, (blk + 1) * L)
        st_tile = jnp.zeros((L, LANES), F32)
        for hp in range(DSA_HEADS // 2):
            grp = slice(hp * LANES, (hp + 1) * LANES)
            qp = q_ref[rows, grp]
            if keys_transposed:
                k_prev = kh_ref[grp, :] if blk == 0 else k_ref[blk - 1, grp, :]
                kcat = jnp.concatenate([k_prev, k_ref[blk, grp, :]], axis=1)
            elif blk == 0:
                kcat = jnp.concatenate([kh_ref[:, grp], k_ref[rows, grp]], axis=0)
            else:
                kcat = k_ref[both, grp]
            if blk == 0:
                vcat = jnp.concatenate([vh_ref[:, grp], v_ref[rows, grp]], axis=0)
            else:
                vcat = v_ref[both, grp]
            outs = []
            for hh in range(2):
                h = 2 * hp + hh
                own = first_half if hh == 0 else ~first_half
                qm = jnp.where(own, qp, jnp.zeros((), BF16))
                bias = bias_ref[h]
                if blk == 0:
                    bias = jnp.where(prev_cols & first_tile, NEG, bias)
                s = (_dot(qm, kcat) if keys_transposed else _dot_nt(qm, kcat)) + bias
                m = jnp.max(s, axis=-1, keepdims=True)
                p = jnp.exp2(s - m)
                den = jnp.sum(p, axis=-1, keepdims=True)
                outs.append(_dot(p.astype(BF16), vcat))
                st_tile = jnp.where(lane == h, m, jnp.where(lane == DSA_HEADS + h, den, st_tile))
            o_ref[rows, grp] = jnp.where(first_half, outs[0], outs[1]).astype(BF16)
        st_ref[rows, :] = st_tile


def _dsa_branch(x, kt, bias, branch, *, tr_max):
    batch, d, n, _ = x.shape
    tr = min(tr_max, n)
    hb = tr // DSA_BLOCK
    W = DSA_WIDTH
    prev_blk = lambda i: jnp.maximum(i * hb - 1, 0)
    blk = lambda part: pl.BlockSpec((None, None, tr, W), lambda b, r, i: (b, r, i, part))
    halo = lambda part: pl.BlockSpec((None, None, DSA_BLOCK, W), lambda b, r, i: (b, r, prev_blk(i), part))
    if kt is None:
        k_arg, k_spec, kh_spec = x, blk(1), halo(1)
    else:
        k_arg = kt
        k_spec = pl.BlockSpec((None, hb, None, W, DSA_BLOCK), lambda b, r, i: (b, i, r, 0, 0))
        kh_spec = pl.BlockSpec((None, None, None, W, DSA_BLOCK), lambda b, r, i: (b, prev_blk(i), r, 0, 0))
    return pl.pallas_call(
        functools.partial(_dsa_kernel, blocks=hb, keys_transposed=kt is not None),
        grid=(batch, d, n // tr),
        in_specs=[blk(0), k_spec, blk(2), kh_spec, halo(2),
                  pl.BlockSpec((None, DSA_HEADS, DSA_BLOCK, 2 * DSA_BLOCK), lambda b, r, i: (branch, 0, 0, 0))],
        out_specs=[pl.BlockSpec((None, None, tr, W), lambda b, r, i: (b, r, i, 0)),
                   pl.BlockSpec((None, None, tr, LANES), lambda b, r, i: (b, r, i, 0))],
        out_shape=[jax.ShapeDtypeStruct((batch, d, n, W), BF16),
                   jax.ShapeDtypeStruct((batch, d, n, LANES), F32)],
        compiler_params=pltpu.CompilerParams(dimension_semantics=("arbitrary",) * 3,
                                             vmem_limit_bytes=VMEM_LIMIT),
        name=f"dsa_d{d}",
    )(x, k_arg, x, k_arg, x, bias)


def _out_kernel(x_ref, oa_ref, o1_ref, o4_ref, o16_ref, l1_ref, l4_ref, l16_ref,
                wo_ref, g2_ref, w1_ref, w2_ref, gf_ref, y_ref, osc_ref, lsc_ref, ob_ref, *, ff_chunk):
    tm = x_ref.shape[0]
    npair = DSA_HEADS // 2
    lane = lax.broadcasted_iota(jnp.int32, (tm, LANES), 1)
    first_half = lane < DSA_DH

    @pl.when(pl.program_id(0) == 0)
    def _():
        ob_ref[...] = jnp.zeros_like(ob_ref)

    mixed = _dot(oa_ref[...], wo_ref[:GLA_WIDTH, :]) + _dot(ob_ref[...], wo_ref[GLA_WIDTH:, :])
    h = x_ref[...] + mixed
    nm = (h * lax.rsqrt(jnp.mean(h * h, axis=-1, keepdims=True) + EPS) * g2_ref[...]).astype(BF16)
    ff = None
    for c in range(D_FF // ff_chunk):
        cols = slice(c * ff_chunk, (c + 1) * ff_chunk)
        a = jnp.maximum(_dot(nm, w1_ref[:, cols]), 0.0)
        d = _dot((a * a).astype(BF16), w2_ref[cols, :])
        ff = d if ff is None else ff + d
    h = h + ff
    y_ref[...] = h * lax.rsqrt(jnp.mean(h * h, axis=-1, keepdims=True) + EPS) * gf_ref[...]

    for j, (d, o_ref, l_ref) in enumerate(((4, o4_ref, l4_ref), (16, o16_ref, l16_ref))):
        for r in range(d):
            lsc_ref[j, pl.ds(r, tm // d, stride=d), :] = l_ref[r]
            for hp in range(npair):
                osc_ref[j * npair + hp, pl.ds(r, tm // d, stride=d), :] = (
                    o_ref[r, :, hp * LANES:(hp + 1) * LANES].astype(F32))

    sts = (l1_ref[...], lsc_ref[0], lsc_ref[1])
    m = jnp.maximum(jnp.maximum(sts[0], sts[1]), sts[2])
    es = [jnp.exp2(st - m) for st in sts]
    total = sum(e * pltpu.roll(st, LANES - DSA_HEADS, axis=1) for e, st in zip(es, sts))
    inv = 1.0 / jnp.where(lane < DSA_HEADS, total, 1.0)
    ws = [e * inv for e in es]

    for hp in range(npair):
        grp = slice(hp * LANES, (hp + 1) * LANES)
        branch_o = (o1_ref[:, grp].astype(F32), osc_ref[hp], osc_ref[npair + hp])
        acc = jnp.zeros((tm, LANES), F32)
        for w, o in zip(ws, branch_o):
            wa = jnp.sum(jnp.where(lane == 2 * hp, w, 0.0), axis=-1, keepdims=True)
            wb = jnp.sum(jnp.where(lane == 2 * hp + 1, w, 0.0), axis=-1, keepdims=True)
            acc = acc + jnp.where(first_half, wa, wb) * o
        ob_ref[:, grp] = acc.astype(BF16)


def _out(x2, oa, os_, ls, wo, g2, w1, w2, gf, *, seq, tm, ff_chunk):
    T = x2.shape[0]
    spb = seq // tm
    nt = T // tm
    prev = lambda s: jnp.maximum(s - 1, 0)
    cur = lambda s: jnp.minimum(s, nt - 1)
    row_prev = lambda n: pl.BlockSpec((tm, n), lambda s: (prev(s), 0))
    row_cur = lambda n: pl.BlockSpec((tm, n), lambda s: (cur(s), 0))
    strided = lambda d, n: pl.BlockSpec((None, d, tm // d, n), lambda s: (cur(s) // spb, 0, cur(s) % spb, 0))
    const = lambda a: pl.BlockSpec(a.shape, lambda s: (0,) * a.ndim, pipeline_mode=pl.Buffered(1))
    W = DSA_WIDTH
    return pl.pallas_call(
        functools.partial(_out_kernel, ff_chunk=ff_chunk),
        grid=(nt + 1,),
        in_specs=[row_prev(D_MODEL), row_prev(GLA_WIDTH), row_cur(W), strided(4, W), strided(16, W),
                  row_cur(LANES), strided(4, LANES), strided(16, LANES),
                  const(wo), const(g2), const(w1), const(w2), const(gf)],
        out_specs=row_prev(D_MODEL),
        out_shape=jax.ShapeDtypeStruct((T, D_MODEL), F32),
        scratch_shapes=[pltpu.VMEM((2 * (W // LANES), tm, LANES), F32),
                        pltpu.VMEM((2, tm, LANES), F32),
                        pltpu.VMEM((tm, W), BF16)],
        compiler_params=pltpu.CompilerParams(dimension_semantics=("arbitrary",),
                                             vmem_limit_bytes=VMEM_LIMIT),
        name="out_mlp",
    )(x2, oa, *os_, *ls, wo, g2, w1, w2, gf)


def kernel(x, attn_norm_g, w_in, gla_gate_w2, gla_gate_b, gla_norm_g, rel_bias, w_out, mlp_norm_g,
           w_ff1, w_ff2, final_norm_g):
    batch, seq, _ = x.shape
    assert seq % (DSA_PATTERN[-1][1] * DSA_BLOCK) == 0
    T = batch * seq
    x2 = x.reshape(T, D_MODEL)

    w = w_in[0]
    splits = np.cumsum([GLA_QK, GLA_QK, GLA_WIDTH, GLA_WIDTH, GLA_RANK, DSA_WIDTH, DSA_WIDTH, DSA_WIDTH])[:-1]
    wq, wk, wv, wr, wg, wdq, wdk, wdv = jnp.split(w, [int(s) for s in splits], axis=1)
    wq = (wq * GLA_DK ** -0.5).astype(BF16)
    wk = wk.astype(BF16)
    wv = wv.astype(BF16)
    wr = wr.astype(BF16)
    wg = jnp.pad(wg, ((0, 0), (0, RANK_PAD - GLA_RANK))).astype(BF16)
    w2 = jnp.pad(gla_gate_w2[0], ((0, RANK_PAD - GLA_RANK), (0, 0))).astype(BF16)
    gb = gla_gate_b[0].astype(F32).reshape(1, GLA_QK)
    wd = jnp.concatenate([wdq * (DSA_DH ** -0.5 * LOG2E), wdk, wdv], axis=1).astype(BF16)

    gq, gv, gr, d1, kt, glog, glogt, d4, d16, kt1, kt4 = _proj(
        x2, attn_norm_g[0].reshape(1, D_MODEL).astype(F32), wq, wv, wr, wk, wg, w2, gb, wd,
        batch=batch, seq=seq, tm=512)

    o_a = _gla(gq, kt, gv, gr, glog, glogt, gla_norm_g[0].reshape(1, GLA_WIDTH).astype(F32),
               batch=batch, seq=seq, tg=512)

    bias = _bias_tables(rel_bias)
    os_, ls = [], []
    branches = ((d1.reshape(batch, 1, seq, 3 * DSA_WIDTH), kt1), (d4, kt4), (d16, None))
    for branch, (xd, ktd) in enumerate(branches):
        o, lse = _dsa_branch(xd, ktd, bias, branch, tr_max=512)
        os_.append(o)
        ls.append(lse)
    os_[0] = os_[0].reshape(T, DSA_WIDTH)
    ls[0] = ls[0].reshape(T, LANES)

    y = _out(x2, o_a, os_, ls, w_out[0].astype(BF16), mlp_norm_g[0].reshape(1, D_MODEL).astype(F32),
             w_ff1[0].astype(BF16), w_ff2[0].astype(BF16), final_norm_g.reshape(1, D_MODEL).astype(F32),
             seq=seq, tm=512, ff_chunk=1024)
    return y.reshape(batch, seq, D_MODEL)
```

```python
import functools
import math

import numpy as np
import jax
import jax.numpy as jnp
from jax import lax
from jax.experimental import pallas as pl
from jax.experimental.pallas import tpu as pltpu

D_MODEL = 1024
GLA_WIDTH = 512
GLA_HEADS = 4
GLA_DK = 64
GLA_DV = 128
GLA_QK = GLA_HEADS * GLA_DK
GLA_RANK = 16
GLA_TAU = 16.0
GLA_CHUNK = 64
DSA_WIDTH = 512
DSA_HEADS = 8
DSA_DH = 64
DSA_PATTERN = ((128, 1), (512, 4), (2048, 16))
DSA_BLOCK = 128
REL_BUCKETS = 32
REL_MAX_DIST = 2048
D_FF = 4096
EPS = 1e-6
NEG = -1e30
LOG2E = math.log2(math.e)

LANES = 128
RANK_PAD = LANES
VMEM_LIMIT = 56 * 1024 * 1024

F32 = jnp.float32
BF16 = jnp.bfloat16

_NT = (((1,), (1,)), ((), ()))


def _dot(a, b):
    return jnp.dot(a, b, preferred_element_type=F32)


def _dot_nt(a, b):
    return lax.dot_general(a, b, _NT, preferred_element_type=F32)


def _split_hi_lo(x):
    hi = x.astype(BF16)
    lo = (x - hi.astype(F32)).astype(BF16)
    return hi, lo


def _log_sigmoid(x):
    return jnp.minimum(x, 0.0) - jnp.log1p(jnp.exp(-jnp.abs(x)))


def _bucket_tables():
    max_exact = REL_BUCKETS // 2
    L = DSA_BLOCK
    steps = L + np.arange(L)[:, None] - np.arange(2 * L)[None, :]
    tables = []
    for window, dilation in DSA_PATTERN:
        span = window // dilation
        in_band = (steps >= 0) & (steps <= span)
        n = np.maximum(steps * dilation, 0)
        large = max_exact + (np.log(np.maximum(n, 1) / max_exact)
                             / math.log(REL_MAX_DIST / max_exact)
                             * (REL_BUCKETS - max_exact)).astype(np.int32)
        large = np.minimum(large, REL_BUCKETS - 1)
        bucket = np.where(n < max_exact, n, large).astype(np.int32)
        tables.append(np.where(in_band, bucket, -1).astype(np.int32))
    return np.stack(tables)


def _bias_kernel(rb_ref, bt_ref, out_ref):
    h = pl.program_id(1)
    bt = bt_ref[0]
    acc = jnp.full(bt.shape, NEG, F32)
    for b in range(REL_BUCKETS):
        acc = jnp.where(bt == b, rb_ref[b, h] * LOG2E, acc)
    out_ref[0, 0] = acc


def _bias_tables(rel_bias):
    nb = len(DSA_PATTERN)
    L = DSA_BLOCK
    return pl.pallas_call(
        _bias_kernel,
        grid=(nb, DSA_HEADS),
        in_specs=[pl.BlockSpec(memory_space=pltpu.SMEM),
                  pl.BlockSpec((1, L, 2 * L), lambda d, h: (d, 0, 0))],
        out_specs=pl.BlockSpec((1, 1, L, 2 * L), lambda d, h: (d, h, 0, 0)),
        out_shape=jax.ShapeDtypeStruct((nb, DSA_HEADS, L, 2 * L), F32),
        name="dsa_bias",
    )(rel_bias.astype(F32), jnp.asarray(_bucket_tables()))


def _proj_kernel(x_ref, g_ref, wq_ref, wv_ref, wr_ref, wk_ref, wg_ref, w2_ref, b_ref, wd_ref,
                 gq_ref, gv_ref, gr_ref, d1_ref, kt_ref, glog_ref, glogt_ref, d4_ref, d16_ref,
                 kt1_ref, kt4_ref, dsc_ref, t4_ref):
    tm = x_ref.shape[0]
    ncol = 3 * DSA_WIDTH // LANES
    kcols = range(DSA_WIDTH // LANES, 2 * DSA_WIDTH // LANES)
    n4 = tm // 4
    n16 = tm // 16

    x = x_ref[...]
    ms = jnp.mean(x * x, axis=-1, keepdims=True)
    nx = (x * lax.rsqrt(ms + EPS) * g_ref[...]).astype(BF16)

    kt_ref[0] = _dot(nx, wk_ref[...]).T.astype(BF16)
    glow = _dot(nx, wg_ref[...]).astype(BF16)
    glog = _log_sigmoid(_dot(glow, w2_ref[...]) + b_ref[...]) * (LOG2E / GLA_TAU)
    glog_ref[...] = glog
    glogt_ref[0] = glog.T

    gq_ref[...] = _dot(nx, wq_ref[...]).astype(BF16)
    gv_ref[...] = _dot(nx, wv_ref[...]).astype(BF16)
    gr_ref[...] = _dot(nx, wr_ref[...]).astype(BF16)

    for j in range(ncol // 2):
        res = _dot(nx, wd_ref[:, 2 * j * LANES:(2 * j + 2) * LANES])
        d1_ref[:, 2 * j * LANES:(2 * j + 2) * LANES] = res.astype(BF16)
        dsc_ref[2 * j] = res[:, :LANES]
        dsc_ref[2 * j + 1] = res[:, LANES:]
        for c in (2 * j, 2 * j + 1):
            cols = slice(c * LANES, (c + 1) * LANES)
            krows = slice((c - kcols[0]) * LANES, (c - kcols[0] + 1) * LANES)
            if c in kcols:
                for jb in range(tm // DSA_BLOCK):
                    blk_rows = slice(jb * DSA_BLOCK, (jb + 1) * DSA_BLOCK)
                    kt1_ref[jb, krows, :] = dsc_ref[c, blk_rows, :].T.astype(BF16)
            for r4 in range(4):
                sub = dsc_ref[c, pl.ds(r4, n4, stride=4), :]
                d4_ref[r4, :, cols] = sub.astype(BF16)
                t4_ref[c, r4 * n4:(r4 + 1) * n4, :] = sub
                if c in kcols:
                    kt4_ref[r4, krows, :] = sub.T.astype(BF16)
            for r4 in range(4):
                for r2 in range(4):
                    d16_ref[r4 + 4 * r2, :, cols] = (
                        t4_ref[c, pl.ds(r4 * n4 + r2, n16, stride=4), :].astype(BF16))


def _proj(x2, g, wq, wv, wr, wk, wg, w2, b, wd, *, batch, seq, tm):
    T = batch * seq
    spb = seq // tm
    nt = T // tm
    full = lambda a: pl.BlockSpec(a.shape, lambda s: (0,) * a.ndim)
    row_cur = lambda n: pl.BlockSpec((tm, n), lambda s: (s, 0))
    row_prev = row_cur
    colt = pl.BlockSpec((1, GLA_QK, tm), lambda s: (s // spb, 0, s % spb))
    W3 = 3 * DSA_WIDTH
    strided = lambda d: pl.BlockSpec((None, d, tm // d, W3), lambda s: (s // spb, 0, s % spb, 0))
    assert tm == 4 * DSA_BLOCK
    W, L = DSA_WIDTH, DSA_BLOCK
    kt1_spec = pl.BlockSpec((None, tm // L, None, W, L), lambda s: (s // spb, s % spb, 0, 0, 0))
    kt4_spec = pl.BlockSpec((None, None, 4, W, L), lambda s: (s // spb, s % spb, 0, 0, 0))
    return pl.pallas_call(
        _proj_kernel,
        grid=(nt,),
        in_specs=[row_cur(D_MODEL)] + [full(a) for a in (g, wq, wv, wr, wk, wg, w2, b, wd)],
        out_specs=[row_cur(GLA_QK), row_cur(GLA_WIDTH), row_cur(GLA_WIDTH), row_cur(W3),
                   colt, row_prev(GLA_QK), colt, strided(4), strided(16), kt1_spec, kt4_spec],
        out_shape=[jax.ShapeDtypeStruct((T, GLA_QK), BF16),
                   jax.ShapeDtypeStruct((T, GLA_WIDTH), BF16),
                   jax.ShapeDtypeStruct((T, GLA_WIDTH), BF16),
                   jax.ShapeDtypeStruct((T, W3), BF16),
                   jax.ShapeDtypeStruct((batch, GLA_QK, seq), BF16),
                   jax.ShapeDtypeStruct((T, GLA_QK), F32),
                   jax.ShapeDtypeStruct((batch, GLA_QK, seq), F32),
                   jax.ShapeDtypeStruct((batch, 4, seq // 4, W3), BF16),
                   jax.ShapeDtypeStruct((batch, 16, seq // 16, W3), BF16),
                   jax.ShapeDtypeStruct((batch, seq // L, 1, W, L), BF16),
                   jax.ShapeDtypeStruct((batch, seq // (4 * L), 4, W, L), BF16)],
        scratch_shapes=[pltpu.VMEM((W3 // LANES, tm, LANES), F32)] * 2,
        compiler_params=pltpu.CompilerParams(dimension_semantics=("arbitrary",),
                                             vmem_limit_bytes=VMEM_LIMIT),
        name="proj",
    )(x2, g, wq, wv, wr, wk, wg, w2, b, wd)


def _gla_kernel(gq_ref, kt_ref, gv_ref, gr_ref, glog_ref, glogt_ref, gn_ref, o_ref, s_ref, *, pairs):
    C = GLA_CHUNK
    P = 2 * C

    @pl.when(pl.program_id(1) == 0)
    def _():
        s_ref[...] = jnp.zeros_like(s_ref)

    ri = lax.broadcasted_iota(jnp.int32, (P, P), 0)
    ci = lax.broadcasted_iota(jnp.int32, (P, P), 1)
    same_chunk = (ri < C) == (ci < C)
    causal = same_chunk & (ci <= ri)
    low = jnp.where(causal, 1.0, 0.0).astype(BF16)
    upp = jnp.where(same_chunk & (ri <= ci), 1.0, 0.0).astype(BF16)
    lane = lax.broadcasted_iota(jnp.int32, (P, LANES), 1)
    first_half = lane < C
    lane_t = lax.broadcasted_iota(jnp.int32, (GLA_QK, P), 1)
    first_t = lane_t < C

    zero16 = jnp.zeros((), BF16)
    state = [s_ref[h * GLA_DK:(h + 1) * GLA_DK, :] for h in range(GLA_HEADS)]

    heads = range(GLA_HEADS)
    hk = [slice(h * GLA_DK, (h + 1) * GLA_DK) for h in heads]
    hv = [slice(h * GLA_DV, (h + 1) * GLA_DV) for h in heads]
    grp = [slice((h // 2) * LANES, (h // 2 + 1) * LANES) for h in heads]

    def decay_stage(p):
        rows = slice(p * P, (p + 1) * P)
        g_hi, g_lo = _split_hi_lo(glog_ref[rows, :])
        b = _dot(low, g_hi) + _dot(low, g_lo)
        gt = glogt_ref[0, :, rows]
        gt_hi, gt_lo = _split_hi_lo(gt)
        bt = _dot(gt_hi, upp) + _dot(gt_lo, upp)
        tot_a = jnp.sum(jnp.where(first_t, gt, 0.0), axis=-1, keepdims=True)
        tot_b = jnp.sum(jnp.where(first_t, 0.0, gt), axis=-1, keepdims=True)
        return dict(rows=rows, b=b, bt=bt, dec_a=jnp.exp2(tot_a), dec_b=jnp.exp2(tot_b))

    def score_stage(c):
        rows = c["rows"]
        qd = (gq_ref[rows, :].astype(F32) * jnp.exp2(c["b"])).astype(BF16)
        kinv_t = (kt_ref[0, :, rows].astype(F32) * jnp.exp2(-c["bt"])).astype(BF16)
        kinv_a = jnp.where(first_t, kinv_t, zero16)
        kinv_b = jnp.where(first_t, zero16, kinv_t)
        c["v"] = [gv_ref[rows, hv[h]] for h in heads]
        c["qm"] = [jnp.where(first_half if h % 2 == 0 else ~first_half, qd[:, grp[h]], zero16) for h in heads]
        c["att"] = [_dot(c["qm"][h], kinv_t[grp[h], :]) for h in heads]
        c["upd_a"] = [_dot(kinv_a[hk[h], :], c["v"][h]) for h in heads]
        c["upd_b"] = [_dot(kinv_b[hk[h], :], c["v"][h]) for h in heads]

    def output_stage(c):
        o = []
        s_a = list(state)
        s_b = [c["dec_a"][hk[h], :] * (s_a[h] + c["upd_a"][h]) for h in heads]
        for h in heads:
            state[h] = c["dec_b"][hk[h], :] * (s_b[h] + c["upd_b"][h])
        for h in heads:
            att = jnp.where(causal, c["att"][h], 0.0).astype(BF16)
            pair_a = [s_a[h], s_a[h ^ 1]] if h % 2 == 0 else [s_a[h ^ 1], s_a[h]]
            pair_b = [s_b[h], s_a[h ^ 1]] if h % 2 == 0 else [s_a[h ^ 1], s_b[h]]
            o_a = _dot(c["qm"][h][:C, :], jnp.concatenate(pair_a, axis=0).astype(BF16))
            o_b = _dot(c["qm"][h][C:, :], jnp.concatenate(pair_b, axis=0).astype(BF16))
            o.append(_dot(att, c["v"][h]) + jnp.concatenate([o_a, o_b], axis=0))
        c["o"] = o

    def norm_stage(c):
        rows = c["rows"]
        for h in heads:
            o = c["o"][h]
            o = o * lax.rsqrt(jnp.mean(o * o, axis=-1, keepdims=True) + EPS) * gn_ref[:, hv[h]]
            r = gr_ref[rows, hv[h]].astype(F32)
            o_ref[rows, hv[h]] = (o * (r * jax.nn.sigmoid(r))).astype(BF16)

    ctx = {}
    for t in range(pairs + 3):
        if t < pairs:
            ctx[t] = decay_stage(t)
        if 0 <= t - 1 < pairs:
            score_stage(ctx[t - 1])
        if 0 <= t - 2 < pairs:
            output_stage(ctx[t - 2])
        if 0 <= t - 3 < pairs:
            norm_stage(ctx.pop(t - 3))

    for h in heads:
        s_ref[hk[h], :] = state[h]


def _gla(gq, kt, gv, gr, glog, glogt, gn, *, batch, seq, tg):
    T = batch * seq
    spb = seq // tg
    row = lambda n: pl.BlockSpec((tg, n), lambda b, i: (b * spb + i, 0))
    colt = pl.BlockSpec((1, GLA_QK, tg), lambda b, i: (b, 0, i))
    return pl.pallas_call(
        functools.partial(_gla_kernel, pairs=tg // (2 * GLA_CHUNK)),
        grid=(batch, spb),
        in_specs=[row(GLA_QK), colt, row(GLA_WIDTH), row(GLA_WIDTH), row(GLA_QK), colt,
                  pl.BlockSpec((1, GLA_WIDTH), lambda b, i: (0, 0))],
        out_specs=row(GLA_WIDTH),
        out_shape=jax.ShapeDtypeStruct((T, GLA_WIDTH), BF16),
        scratch_shapes=[pltpu.VMEM((GLA_QK, GLA_DV), F32)],
        compiler_params=pltpu.CompilerParams(dimension_semantics=("arbitrary", "arbitrary"),
                                             vmem_limit_bytes=VMEM_LIMIT),
        name="gla",
    )(gq, kt, gv, gr, glog, glogt, gn)


def _dsa_kernel(q_ref, k_ref, v_ref, kh_ref, vh_ref, bias_ref, o_ref, st_ref, *, blocks, keys_transposed):
    L = DSA_BLOCK
    first_tile = pl.program_id(2) == 0
    lane = lax.broadcasted_iota(jnp.int32, (L, LANES), 1)
    first_half = lane < DSA_DH
    prev_cols = lax.broadcasted_iota(jnp.int32, (L, 2 * L), 1) < L

    for blk in range(blocks):
        rows = slice(blk * L, (blk + 1) * L)
        both = slice((blk - 1) * L, (blk + 1) * L)
        st_tile = jnp.zeros((L, LANES), F32)
        for hp in range(DSA_HEADS // 2):
            grp = slice(hp * LANES, (hp + 1) * LANES)
            qp = q_ref[rows, grp]
            if keys_transposed:
                k_prev = kh_ref[grp, :] if blk == 0 else k_ref[blk - 1, grp, :]
                kcat = jnp.concatenate([k_prev, k_ref[blk, grp, :]], axis=1)
            elif blk == 0:
                kcat = jnp.concatenate([kh_ref[:, grp], k_ref[rows, grp]], axis=0)
            else:
                kcat = k_ref[both, grp]
            if blk == 0:
                vcat = jnp.concatenate([vh_ref[:, grp], v_ref[rows, grp]], axis=0)
            else:
                vcat = v_ref[both, grp]
            outs = []
            for hh in range(2):
                h = 2 * hp + hh
                own = first_half if hh == 0 else ~first_half
                qm = jnp.where(own, qp, jnp.zeros((), BF16))
                bias = bias_ref[h]
                if blk == 0:
                    bias = jnp.where(prev_cols & first_tile, NEG, bias)
                s = (_dot(qm, kcat) if keys_transposed else _dot_nt(qm, kcat)) + bias
                m = jnp.max(s, axis=-1, keepdims=True)
                p = jnp.exp2(s - m)
                den = jnp.sum(p, axis=-1, keepdims=True)
                outs.append(_dot(p.astype(BF16), vcat))
                st_tile = jnp.where(lane == h, m, jnp.where(lane == DSA_HEADS + h, den, st_tile))
            o_ref[rows, grp] = jnp.where(first_half, outs[0], outs[1]).astype(BF16)
        st_ref[rows, :] = st_tile


def _dsa_branch(x, kt, bias, branch, *, tr_max):
    batch, d, n, _ = x.shape
    tr = min(tr_max, n)
    hb = tr // DSA_BLOCK
    W = DSA_WIDTH
    prev_blk = lambda i: jnp.maximum(i * hb - 1, 0)
    blk = lambda part: pl.BlockSpec((None, None, tr, W), lambda b, r, i: (b, r, i, part))
    halo = lambda part: pl.BlockSpec((None, None, DSA_BLOCK, W), lambda b, r, i: (b, r, prev_blk(i), part))
    if kt is None:
        k_arg, k_spec, kh_spec = x, blk(1), halo(1)
    else:
        k_arg = kt
        k_spec = pl.BlockSpec((None, hb, None, W, DSA_BLOCK), lambda b, r, i: (b, i, r, 0, 0))
        kh_spec = pl.BlockSpec((None, None, None, W, DSA_BLOCK), lambda b, r, i: (b, prev_blk(i), r, 0, 0))
    return pl.pallas_call(
        functools.partial(_dsa_kernel, blocks=hb, keys_transposed=kt is not None),
        grid=(batch, d, n // tr),
        in_specs=[blk(0), k_spec, blk(2), kh_spec, halo(2),
                  pl.BlockSpec((None, DSA_HEADS, DSA_BLOCK, 2 * DSA_BLOCK), lambda b, r, i: (branch, 0, 0, 0))],
        out_specs=[pl.BlockSpec((None, None, tr, W), lambda b, r, i: (b, r, i, 0)),
                   pl.BlockSpec((None, None, tr, LANES), lambda b, r, i: (b, r, i, 0))],
        out_shape=[jax.ShapeDtypeStruct((batch, d, n, W), BF16),
                   jax.ShapeDtypeStruct((batch, d, n, LANES), F32)],
        compiler_params=pltpu.CompilerParams(dimension_semantics=("arbitrary",) * 3,
                                             vmem_limit_bytes=VMEM_LIMIT),
        name=f"dsa_d{d}",
    )(x, k_arg, x, k_arg, x, bias)


def _out_kernel(x_ref, oa_ref, o1_ref, o4_ref, o16_ref, l1_ref, l4_ref, l16_ref,
                wo_ref, g2_ref, w1_ref, w2_ref, gf_ref, y_ref, osc_ref, lsc_ref, ob_ref, *, ff_chunk):
    tm = x_ref.shape[0]
    npair = DSA_HEADS // 2
    lane = lax.broadcasted_iota(jnp.int32, (tm, LANES), 1)
    first_half = lane < DSA_DH

    @pl.when(pl.program_id(0) == 0)
    def _():
        ob_ref[...] = jnp.zeros_like(ob_ref)

    mixed = _dot(oa_ref[...], wo_ref[:GLA_WIDTH, :]) + _dot(ob_ref[...], wo_ref[GLA_WIDTH:, :])
    h = x_ref[...] + mixed
    nm = (h * lax.rsqrt(jnp.mean(h * h, axis=-1, keepdims=True) + EPS) * g2_ref[...]).astype(BF16)
    ff = None
    for c in range(D_FF // ff_chunk):
        cols = slice(c * ff_chunk, (c + 1) * ff_chunk)
        a = jnp.maximum(_dot(nm, w1_ref[:, cols]), 0.0)
        d = _dot((a * a).astype(BF16), w2_ref[cols, :])
        ff = d if ff is None else ff + d
    h = h + ff
    y_ref[...] = h * lax.rsqrt(jnp.mean(h * h, axis=-1, keepdims=True) + EPS) * gf_ref[...]

    for j, (d, o_ref, l_ref) in enumerate(((4, o4_ref, l4_ref), (16, o16_ref, l16_ref))):
        for r in range(d):
            lsc_ref[j, pl.ds(r, tm // d, stride=d), :] = l_ref[r]
            for hp in range(npair):
                osc_ref[j * npair + hp, pl.ds(r, tm // d, stride=d), :] = (
                    o_ref[r, :, hp * LANES:(hp + 1) * LANES].astype(F32))

    sts = (l1_ref[...], lsc_ref[0], lsc_ref[1])
    m = jnp.maximum(jnp.maximum(sts[0], sts[1]), sts[2])
    es = [jnp.exp2(st - m) for st in sts]
    total = sum(e * pltpu.roll(st, LANES - DSA_HEADS, axis=1) for e, st in zip(es, sts))
    inv = 1.0 / jnp.where(lane < DSA_HEADS, total, 1.0)
    ws = [e * inv for e in es]

    for hp in range(npair):
        grp = slice(hp * LANES, (hp + 1) * LANES)
        branch_o = (o1_ref[:, grp].astype(F32), osc_ref[hp], osc_ref[npair + hp])
        acc = jnp.zeros((tm, LANES), F32)
        for w, o in zip(ws, branch_o):
            wa = jnp.sum(jnp.where(lane == 2 * hp, w, 0.0), axis=-1, keepdims=True)
            wb = jnp.sum(jnp.where(lane == 2 * hp + 1, w, 0.0), axis=-1, keepdims=True)
            acc = acc + jnp.where(first_half, wa, wb) * o
        ob_ref[:, grp] = acc.astype(BF16)


def _out(x2, oa, os_, ls, wo, g2, w1, w2, gf, *, seq, tm, ff_chunk):
    T = x2.shape[0]
    spb = seq // tm
    nt = T // tm
    prev = lambda s: jnp.maximum(s - 1, 0)
    cur = lambda s: jnp.minimum(s, nt - 1)
    row_prev = lambda n: pl.BlockSpec((tm, n), lambda s: (prev(s), 0))
    row_cur = lambda n: pl.BlockSpec((tm, n), lambda s: (cur(s), 0))
    strided = lambda d, n: pl.BlockSpec((None, d, tm // d, n), lambda s: (cur(s) // spb, 0, cur(s) % spb, 0))
    const = lambda a: pl.BlockSpec(a.shape, lambda s: (0,) * a.ndim, pipeline_mode=pl.Buffered(1))
    W = DSA_WIDTH
    return pl.pallas_call(
        functools.partial(_out_kernel, ff_chunk=ff_chunk),
        grid=(nt + 1,),
        in_specs=[row_prev(D_MODEL), row_prev(GLA_WIDTH), row_cur(W), strided(4, W), strided(16, W),
                  row_cur(LANES), strided(4, LANES), strided(16, LANES),
                  const(wo), const(g2), const(w1), const(w2), const(gf)],
        out_specs=row_prev(D_MODEL),
        out_shape=jax.ShapeDtypeStruct((T, D_MODEL), F32),
        scratch_shapes=[pltpu.VMEM((2 * (W // LANES), tm, LANES), F32),
                        pltpu.VMEM((2, tm, LANES), F32),
                        pltpu.VMEM((tm, W), BF16)],
        compiler_params=pltpu.CompilerParams(dimension_semantics=("arbitrary",),
                                             vmem_limit_bytes=VMEM_LIMIT),
        name="out_mlp",
    )(x2, oa, *os_, *ls, wo, g2, w1, w2, gf)


def kernel(x, attn_norm_g, w_in, gla_gate_w2, gla_gate_b, gla_norm_g, rel_bias, w_out, mlp_norm_g,
           w_ff1, w_ff2, final_norm_g):
    batch, seq, _ = x.shape
    assert seq % (DSA_PATTERN[-1][1] * DSA_BLOCK) == 0
    T = batch * seq
    x2 = x.reshape(T, D_MODEL)

    w = w_in[0]
    splits = np.cumsum([GLA_QK, GLA_QK, GLA_WIDTH, GLA_WIDTH, GLA_RANK, DSA_WIDTH, DSA_WIDTH, DSA_WIDTH])[:-1]
    wq, wk, wv, wr, wg, wdq, wdk, wdv = jnp.split(w, [int(s) for s in splits], axis=1)
    wq = (wq * GLA_DK ** -0.5).astype(BF16)
    wk = wk.astype(BF16)
    wv = wv.astype(BF16)
    wr = wr.astype(BF16)
    wg = jnp.pad(wg, ((0, 0), (0, RANK_PAD - GLA_RANK))).astype(BF16)
    w2 = jnp.pad(gla_gate_w2[0], ((0, RANK_PAD - GLA_RANK), (0, 0))).astype(BF16)
    gb = gla_gate_b[0].astype(F32).reshape(1, GLA_QK)
    wd = jnp.concatenate([wdq * (DSA_DH ** -0.5 * LOG2E), wdk, wdv], axis=1).astype(BF16)

    gq, gv, gr, d1, kt, glog, glogt, d4, d16, kt1, kt4 = _proj(
        x2, attn_norm_g[0].reshape(1, D_MODEL).astype(F32), wq, wv, wr, wk, wg, w2, gb, wd,
        batch=batch, seq=seq, tm=512)

    o_a = _gla(gq, kt, gv, gr, glog, glogt, gla_norm_g[0].reshape(1, GLA_WIDTH).astype(F32),
               batch=batch, seq=seq, tg=512)

    bias = _bias_tables(rel_bias)
    os_, ls = [], []
    branches = ((d1.reshape(batch, 1, seq, 3 * DSA_WIDTH), kt1), (d4, kt4), (d16, None))
    for branch, (xd, ktd) in enumerate(branches):
        o, lse = _dsa_branch(xd, ktd, bias, branch, tr_max=512)
        os_.append(o)
        ls.append(lse)
    os_[0] = os_[0].reshape(T, DSA_WIDTH)
    ls[0] = ls[0].reshape(T, LANES)

    y = _out(x2, o_a, os_, ls, w_out[0].astype(BF16), mlp_norm_g[0].reshape(1, D_MODEL).astype(F32),
             w_ff1[0].astype(BF16), w_ff2[0].astype(BF16), final_norm_g.reshape(1, D_MODEL).astype(F32),
             seq=seq, tm=512, ff_chunk=1024)
    return y.reshape(batch, seq, D_MODEL)
```

```python
import functools
import math

import numpy as np
import jax
import jax.numpy as jnp
from jax import lax
from jax.experimental import pallas as pl
from jax.experimental.pallas import tpu as pltpu

D_MODEL = 1024
GLA_WIDTH = 512
GLA_HEADS = 4
GLA_DK = 64
GLA_DV = 128
GLA_QK = GLA_HEADS * GLA_DK
GLA_RANK = 16
GLA_TAU = 16.0
GLA_CHUNK = 64
DSA_WIDTH = 512
DSA_HEADS = 8
DSA_DH = 64
DSA_PATTERN = ((128, 1), (512, 4), (2048, 16))
DSA_BLOCK = 128
REL_BUCKETS = 32
REL_MAX_DIST = 2048
D_FF = 4096
EPS = 1e-6
NEG = -1e30
LOG2E = math.log2(math.e)

LANES = 128
RANK_PAD = LANES
VMEM_LIMIT = 56 * 1024 * 1024

F32 = jnp.float32
BF16 = jnp.bfloat16

_NT = (((1,), (1,)), ((), ()))


def _dot(a, b):
    return jnp.dot(a, b, preferred_element_type=F32)


def _dot_nt(a, b):
    return lax.dot_general(a, b, _NT, preferred_element_type=F32)


def _split_hi_lo(x):
    hi = x.astype(BF16)
    lo = (x - hi.astype(F32)).astype(BF16)
    return hi, lo


def _log_sigmoid(x):
    return jnp.minimum(x, 0.0) - jnp.log1p(jnp.exp(-jnp.abs(x)))


def _bucket_tables():
    max_exact = REL_BUCKETS // 2
    L = DSA_BLOCK
    steps = L + np.arange(L)[:, None] - np.arange(2 * L)[None, :]
    tables = []
    for window, dilation in DSA_PATTERN:
        span = window // dilation
        in_band = (steps >= 0) & (steps <= span)
        n = np.maximum(steps * dilation, 0)
        large = max_exact + (np.log(np.maximum(n, 1) / max_exact)
                             / math.log(REL_MAX_DIST / max_exact)
                             * (REL_BUCKETS - max_exact)).astype(np.int32)
        large = np.minimum(large, REL_BUCKETS - 1)
        bucket = np.where(n < max_exact, n, large).astype(np.int32)
        tables.append(np.where(in_band, bucket, -1).astype(np.int32))
    return np.stack(tables)


def _bias_kernel(rb_ref, bt_ref, out_ref):
    h = pl.program_id(1)
    bt = bt_ref[0]
    acc = jnp.full(bt.shape, NEG, F32)
    for b in range(REL_BUCKETS):
        acc = jnp.where(bt == b, rb_ref[b, h] * LOG2E, acc)
    out_ref[0, 0] = acc


def _bias_tables(rel_bias):
    nb = len(DSA_PATTERN)
    L = DSA_BLOCK
    return pl.pallas_call(
        _bias_kernel,
        grid=(nb, DSA_HEADS),
        in_specs=[pl.BlockSpec(memory_space=pltpu.SMEM),
                  pl.BlockSpec((1, L, 2 * L), lambda d, h: (d, 0, 0))],
        out_specs=pl.BlockSpec((1, 1, L, 2 * L), lambda d, h: (d, h, 0, 0)),
        out_shape=jax.ShapeDtypeStruct((nb, DSA_HEADS, L, 2 * L), F32),
        name="dsa_bias",
    )(rel_bias.astype(F32), jnp.asarray(_bucket_tables()))


def _proj_kernel(x_ref, g_ref, wq_ref, wv_ref, wr_ref, wk_ref, wg_ref, w2_ref, b_ref, wd_ref,
                 gq_ref, gv_ref, gr_ref, d1_ref, kt_ref, glog_ref, glogt_ref, d4_ref, d16_ref,
                 kt1_ref, kt4_ref, dsc_ref, t4_ref):
    tm = x_ref.shape[0]
    ncol = 3 * DSA_WIDTH // LANES
    kcols = range(DSA_WIDTH // LANES, 2 * DSA_WIDTH // LANES)
    n4 = tm // 4
    n16 = tm // 16

    x = x_ref[...]
    ms = jnp.mean(x * x, axis=-1, keepdims=True)
    nx = (x * lax.rsqrt(ms + EPS) * g_ref[...]).astype(BF16)

    kt_ref[0] = _dot(nx, wk_ref[...]).T.astype(BF16)
    glow = _dot(nx, wg_ref[...]).astype(BF16)
    glog = _log_sigmoid(_dot(glow, w2_ref[...]) + b_ref[...]) * (LOG2E / GLA_TAU)
    glog_ref[...] = glog
    glogt_ref[0] = glog.T

    gq_ref[...] = _dot(nx, wq_ref[...]).astype(BF16)
    gv_ref[...] = _dot(nx, wv_ref[...]).astype(BF16)
    gr_ref[...] = _dot(nx, wr_ref[...]).astype(BF16)

    for j in range(ncol // 2):
        res = _dot(nx, wd_ref[:, 2 * j * LANES:(2 * j + 2) * LANES])
        d1_ref[:, 2 * j * LANES:(2 * j + 2) * LANES] = res.astype(BF16)
        dsc_ref[2 * j] = res[:, :LANES]
        dsc_ref[2 * j + 1] = res[:, LANES:]
        for c in (2 * j, 2 * j + 1):
            cols = slice(c * LANES, (c + 1) * LANES)
            krows = slice((c - kcols[0]) * LANES, (c - kcols[0] + 1) * LANES)
            if c in kcols:
                for jb in range(tm // DSA_BLOCK):
                    blk_rows = slice(jb * DSA_BLOCK, (jb + 1) * DSA_BLOCK)
                    kt1_ref[jb, krows, :] = dsc_ref[c, blk_rows, :].T.astype(BF16)
            for r4 in range(4):
                sub = dsc_ref[c, pl.ds(r4, n4, stride=4), :]
                d4_ref[r4, :, cols] = sub.astype(BF16)
                t4_ref[c, r4 * n4:(r4 + 1) * n4, :] = sub
                if c in kcols:
                    kt4_ref[r4, krows, :] = sub.T.astype(BF16)
            for r4 in range(4):
                for r2 in range(4):
                    d16_ref[r4 + 4 * r2, :, cols] = (
                        t4_ref[c, pl.ds(r4 * n4 + r2, n16, stride=4), :].astype(BF16))


def _proj(x2, g, wq, wv, wr, wk, wg, w2, b, wd, *, batch, seq, tm):
    T = batch * seq
    spb = seq // tm
    nt = T // tm
    full = lambda a: pl.BlockSpec(a.shape, lambda s: (0,) * a.ndim)
    row_cur = lambda n: pl.BlockSpec((tm, n), lambda s: (s, 0))
    row_prev = row_cur
    colt = pl.BlockSpec((1, GLA_QK, tm), lambda s: (s // spb, 0, s % spb))
    W3 = 3 * DSA_WIDTH
    strided = lambda d: pl.BlockSpec((None, d, tm // d, W3), lambda s: (s // spb, 0, s % spb, 0))
    assert tm == 4 * DSA_BLOCK
    W, L = DSA_WIDTH, DSA_BLOCK
    kt1_spec = pl.BlockSpec((None, tm // L, None, W, L), lambda s: (s // spb, s % spb, 0, 0, 0))
    kt4_spec = pl.BlockSpec((None, None, 4, W, L), lambda s: (s // spb, s % spb, 0, 0, 0))
    return pl.pallas_call(
        _proj_kernel,
        grid=(nt,),
        in_specs=[row_cur(D_MODEL)] + [full(a) for a in (g, wq, wv, wr, wk, wg, w2, b, wd)],
        out_specs=[row_cur(GLA_QK), row_cur(GLA_WIDTH), row_cur(GLA_WIDTH), row_cur(W3),
                   colt, row_prev(GLA_QK), colt, strided(4), strided(16), kt1_spec, kt4_spec],
        out_shape=[jax.ShapeDtypeStruct((T, GLA_QK), BF16),
                   jax.ShapeDtypeStruct((T, GLA_WIDTH), BF16),
                   jax.ShapeDtypeStruct((T, GLA_WIDTH), BF16),
                   jax.ShapeDtypeStruct((T, W3), BF16),
                   jax.ShapeDtypeStruct((batch, GLA_QK, seq), BF16),
                   jax.ShapeDtypeStruct((T, GLA_QK), F32),
                   jax.ShapeDtypeStruct((batch, GLA_QK, seq), F32),
                   jax.ShapeDtypeStruct((batch, 4, seq // 4, W3), BF16),
                   jax.ShapeDtypeStruct((batch, 16, seq // 16, W3), BF16),
                   jax.ShapeDtypeStruct((batch, seq // L, 1, W, L), BF16),
                   jax.ShapeDtypeStruct((batch, seq // (4 * L), 4, W, L), BF16)],
        scratch_shapes=[pltpu.VMEM((W3 // LANES, tm, LANES), F32)] * 2,
        compiler_params=pltpu.CompilerParams(dimension_semantics=("arbitrary",),
                                             vmem_limit_bytes=VMEM_LIMIT),
        name="proj",
    )(x2, g, wq, wv, wr, wk, wg, w2, b, wd)


def _gla_kernel(gq_ref, kt_ref, gv_ref, gr_ref, glog_ref, glogt_ref, gn_ref, o_ref, s_ref, *, pairs):
    C = GLA_CHUNK
    P = 2 * C

    @pl.when(pl.program_id(1) == 0)
    def _():
        s_ref[...] = jnp.zeros_like(s_ref)

    ri = lax.broadcasted_iota(jnp.int32, (P, P), 0)
    ci = lax.broadcasted_iota(jnp.int32, (P, P), 1)
    same_chunk = (ri < C) == (ci < C)
    causal = same_chunk & (ci <= ri)
    low = jnp.where(causal, 1.0, 0.0).astype(BF16)
    upp = jnp.where(same_chunk & (ri <= ci), 1.0, 0.0).astype(BF16)
    lane = lax.broadcasted_iota(jnp.int32, (P, LANES), 1)
    first_half = lane < C
    lane_t = lax.broadcasted_iota(jnp.int32, (GLA_QK, P), 1)
    first_t = lane_t < C

    zero16 = jnp.zeros((), BF16)
    state = [s_ref[h * GLA_DK:(h + 1) * GLA_DK, :] for h in range(GLA_HEADS)]

    heads = range(GLA_HEADS)
    hk = [slice(h * GLA_DK, (h + 1) * GLA_DK) for h in heads]
    hv = [slice(h * GLA_DV, (h + 1) * GLA_DV) for h in heads]
    grp = [slice((h // 2) * LANES, (h // 2 + 1) * LANES) for h in heads]

    def decay_stage(p):
        rows = slice(p * P, (p + 1) * P)
        g_hi, g_lo = _split_hi_lo(glog_ref[rows, :])
        b = _dot(low, g_hi) + _dot(low, g_lo)
        gt = glogt_ref[0, :, rows]
        gt_hi, gt_lo = _split_hi_lo(gt)
        bt = _dot(gt_hi, upp) + _dot(gt_lo, upp)
        tot_a = jnp.sum(jnp.where(first_t, gt, 0.0), axis=-1, keepdims=True)
        tot_b = jnp.sum(jnp.where(first_t, 0.0, gt), axis=-1, keepdims=True)
        return dict(rows=rows, b=b, bt=bt, dec_a=jnp.exp2(tot_a), dec_b=jnp.exp2(tot_b))

    def score_stage(c):
        rows = c["rows"]
        qd = (gq_ref[rows, :].astype(F32) * jnp.exp2(c["b"])).astype(BF16)
        kinv_t = (kt_ref[0, :, rows].astype(F32) * jnp.exp2(-c["bt"])).astype(BF16)
        kinv_a = jnp.where(first_t, kinv_t, zero16)
        kinv_b = jnp.where(first_t, zero16, kinv_t)
        c["v"] = [gv_ref[rows, hv[h]] for h in heads]
        c["qm"] = [jnp.where(first_half if h % 2 == 0 else ~first_half, qd[:, grp[h]], zero16) for h in heads]
        c["att"] = [_dot(c["qm"][h], kinv_t[grp[h], :]) for h in heads]
        c["upd_a"] = [_dot(kinv_a[hk[h], :], c["v"][h]) for h in heads]
        c["upd_b"] = [_dot(kinv_b[hk[h], :], c["v"][h]) for h in heads]

    def output_stage(c):
        o = []
        s_a = list(state)
        s_b = [c["dec_a"][hk[h], :] * (s_a[h] + c["upd_a"][h]) for h in heads]
        for h in heads:
            state[h] = c["dec_b"][hk[h], :] * (s_b[h] + c["upd_b"][h])
        for h in heads:
            att = jnp.where(causal, c["att"][h], 0.0).astype(BF16)
            pair_a = [s_a[h], s_a[h ^ 1]] if h % 2 == 0 else [s_a[h ^ 1], s_a[h]]
            pair_b = [s_b[h], s_a[h ^ 1]] if h % 2 == 0 else [s_a[h ^ 1], s_b[h]]
            o_a = _dot(c["qm"][h][:C, :], jnp.concatenate(pair_a, axis=0).astype(BF16))
            o_b = _dot(c["qm"][h][C:, :], jnp.concatenate(pair_b, axis=0).astype(BF16))
            o.append(_dot(att, c["v"][h]) + jnp.concatenate([o_a, o_b], axis=0))
        c["o"] = o

    def norm_stage(c):
        rows = c["rows"]
        for h in heads:
            o = c["o"][h]
            o = o * lax.rsqrt(jnp.mean(o * o, axis=-1, keepdims=True) + EPS) * gn_ref[:, hv[h]]
            r = gr_ref[rows, hv[h]].astype(F32)
            o_ref[rows, hv[h]] = (o * (r * jax.nn.sigmoid(r))).astype(BF16)

    ctx = {}
    for t in range(pairs + 3):
        if t < pairs:
            ctx[t] = decay_stage(t)
        if 0 <= t - 1 < pairs:
            score_stage(ctx[t - 1])
        if 0 <= t - 2 < pairs:
            output_stage(ctx[t - 2])
        if 0 <= t - 3 < pairs:
            norm_stage(ctx.pop(t - 3))

    for h in heads:
        s_ref[hk[h], :] = state[h]


def _gla(gq, kt, gv, gr, glog, glogt, gn, *, batch, seq, tg):
    T = batch * seq
    spb = seq // tg
    row = lambda n: pl.BlockSpec((tg, n), lambda b, i: (b * spb + i, 0))
    colt = pl.BlockSpec((1, GLA_QK, tg), lambda b, i: (b, 0, i))
    return pl.pallas_call(
        functools.partial(_gla_kernel, pairs=tg // (2 * GLA_CHUNK)),
        grid=(batch, spb),
        in_specs=[row(GLA_QK), colt, row(GLA_WIDTH), row(GLA_WIDTH), row(GLA_QK), colt,
                  pl.BlockSpec((1, GLA_WIDTH), lambda b, i: (0, 0))],
        out_specs=row(GLA_WIDTH),
        out_shape=jax.ShapeDtypeStruct((T, GLA_WIDTH), BF16),
        scratch_shapes=[pltpu.VMEM((GLA_QK, GLA_DV), F32)],
        compiler_params=pltpu.CompilerParams(dimension_semantics=("arbitrary", "arbitrary"),
                                             vmem_limit_bytes=VMEM_LIMIT),
        name="gla",
    )(gq, kt, gv, gr, glog, glogt, gn)


def _dsa_kernel(q_ref, k_ref, v_ref, kh_ref, vh_ref, bias_ref, o_ref, st_ref, *, res, blocks, keys_transposed):
    L = DSA_BLOCK
    first_tile = pl.program_id(2) == 0
    lane = lax.broadcasted_iota(jnp.int32, (L, LANES), 1)
    first_half = lane < DSA_DH
    prev_cols = lax.broadcasted_iota(jnp.int32, (L, 2 * L), 1) < L

    st_tiles = {}
    pair_out = {}

    def score_stage(r, blk, h):
        rows = slice(blk * L, (blk + 1) * L)
        both = slice((blk - 1) * L, (blk + 1) * L)
        grp = slice((h // 2) * LANES, (h // 2 + 1) * LANES)
        if keys_transposed:
            k_prev = kh_ref[r, grp, :] if blk == 0 else k_ref[blk - 1, r, grp, :]
            kcat = jnp.concatenate([k_prev, k_ref[blk, r, grp, :]], axis=1)
        elif blk == 0:
            kcat = jnp.concatenate([kh_ref[r, :, grp], k_ref[r, rows, grp]], axis=0)
        else:
            kcat = k_ref[r, both, grp]
        own = first_half if h % 2 == 0 else ~first_half
        qm = jnp.where(own, q_ref[r, rows, grp], jnp.zeros((), BF16))
        return _dot(qm, kcat) if keys_transposed else _dot_nt(qm, kcat)

    def value_stage(r, blk, h, s):
        rows = slice(blk * L, (blk + 1) * L)
        grp = slice((h // 2) * LANES, (h // 2 + 1) * LANES)
        bias = bias_ref[h]
        if blk == 0:
            bias = jnp.where(prev_cols & first_tile, NEG, bias)
        s = s + bias
        m = jnp.max(s, axis=-1, keepdims=True)
        p = jnp.exp2(s - m)
        den = jnp.sum(p, axis=-1, keepdims=True)
        st_old = st_tiles.get((r, blk), jnp.zeros((L, LANES), F32))
        st_tiles[r, blk] = jnp.where(lane == h, m, jnp.where(lane == DSA_HEADS + h, den, st_old))
        if blk == 0:
            vcat = jnp.concatenate([vh_ref[r, :, grp], v_ref[r, rows, grp]], axis=0)
        else:
            vcat = v_ref[r, both_rows(blk), grp]
        out = _dot(p.astype(BF16), vcat)
        if h % 2 == 0:
            pair_out[r, blk] = out
        else:
            o_ref[r, rows, grp] = jnp.where(first_half, pair_out.pop((r, blk)), out).astype(BF16)
        if h == DSA_HEADS - 1:
            st_ref[r, rows, :] = st_tiles.pop((r, blk))

    def both_rows(blk):
        return slice((blk - 1) * L, (blk + 1) * L)

    items = [(r, blk, h) for r in range(res) for blk in range(blocks) for h in range(DSA_HEADS)]
    lag = 1
    pending = {}
    for t in range(len(items) + lag):
        if t < len(items):
            pending[t] = score_stage(*items[t])
        if t - lag >= 0:
            value_stage(*items[t - lag], pending.pop(t - lag))


def _dsa_branch(x, kt, bias, branch, *, rows_per_step):
    batch, d, n, _ = x.shape
    tr = min(rows_per_step, n)
    res = min(rows_per_step // tr, d)
    hb = tr // DSA_BLOCK
    W = DSA_WIDTH
    prev_blk = lambda i: jnp.maximum(i * hb - 1, 0)
    blk = lambda part: pl.BlockSpec((None, res, tr, W), lambda b, r, i: (b, r, i, part))
    halo = lambda part: pl.BlockSpec((None, res, DSA_BLOCK, W), lambda b, r, i: (b, r, prev_blk(i), part))
    if kt is None:
        k_arg, k_spec, kh_spec = x, blk(1), halo(1)
    else:
        k_arg = kt
        k_spec = pl.BlockSpec((None, hb, res, W, DSA_BLOCK), lambda b, r, i: (b, i, r, 0, 0))
        kh_spec = pl.BlockSpec((None, None, res, W, DSA_BLOCK), lambda b, r, i: (b, prev_blk(i), r, 0, 0))
    return pl.pallas_call(
        functools.partial(_dsa_kernel, res=res, blocks=hb, keys_transposed=kt is not None),
        grid=(batch, d // res, n // tr),
        in_specs=[blk(0), k_spec, blk(2), kh_spec, halo(2),
                  pl.BlockSpec((None, DSA_HEADS, DSA_BLOCK, 2 * DSA_BLOCK), lambda b, r, i: (branch, 0, 0, 0))],
        out_specs=[pl.BlockSpec((None, res, tr, W), lambda b, r, i: (b, r, i, 0)),
                   pl.BlockSpec((None, res, tr, LANES), lambda b, r, i: (b, r, i, 0))],
        out_shape=[jax.ShapeDtypeStruct((batch, d, n, W), BF16),
                   jax.ShapeDtypeStruct((batch, d, n, LANES), F32)],
        compiler_params=pltpu.CompilerParams(dimension_semantics=("arbitrary",) * 3,
                                             vmem_limit_bytes=VMEM_LIMIT),
        name=f"dsa_d{d}",
    )(x, k_arg, x, k_arg, x, bias)


def _out_kernel(x_ref, oa_ref, o1_ref, o4_ref, o16_ref, l1_ref, l4_ref, l16_ref,
                wo_ref, g2_ref, w1_ref, w2_ref, gf_ref, y_ref, osc_ref, lsc_ref, ob_ref, *, ff_chunk):
    tm = x_ref.shape[0]
    npair = DSA_HEADS // 2
    lane = lax.broadcasted_iota(jnp.int32, (tm, LANES), 1)
    first_half = lane < DSA_DH

    @pl.when(pl.program_id(0) == 0)
    def _():
        ob_ref[...] = jnp.zeros_like(ob_ref)

    mixed = _dot(oa_ref[...], wo_ref[:GLA_WIDTH, :]) + _dot(ob_ref[...], wo_ref[GLA_WIDTH:, :])
    h = x_ref[...] + mixed
    nm = (h * lax.rsqrt(jnp.mean(h * h, axis=-1, keepdims=True) + EPS) * g2_ref[...]).astype(BF16)
    ff = None
    for c in range(D_FF // ff_chunk):
        cols = slice(c * ff_chunk, (c + 1) * ff_chunk)
        a = jnp.maximum(_dot(nm, w1_ref[:, cols]), 0.0)
        d = _dot((a * a).astype(BF16), w2_ref[cols, :])
        ff = d if ff is None else ff + d
    h = h + ff
    y_ref[...] = h * lax.rsqrt(jnp.mean(h * h, axis=-1, keepdims=True) + EPS) * gf_ref[...]

    for j, (d, o_ref, l_ref) in enumerate(((4, o4_ref, l4_ref), (16, o16_ref, l16_ref))):
        for r in range(d):
            lsc_ref[j, pl.ds(r, tm // d, stride=d), :] = l_ref[r]
            for hp in range(npair):
                osc_ref[j * npair + hp, pl.ds(r, tm // d, stride=d), :] = (
                    o_ref[r, :, hp * LANES:(hp + 1) * LANES].astype(F32))

    sts = (l1_ref[...], lsc_ref[0], lsc_ref[1])
    m = jnp.maximum(jnp.maximum(sts[0], sts[1]), sts[2])
    es = [jnp.exp2(st - m) for st in sts]
    total = sum(e * pltpu.roll(st, LANES - DSA_HEADS, axis=1) for e, st in zip(es, sts))
    inv = 1.0 / jnp.where(lane < DSA_HEADS, total, 1.0)
    ws = [e * inv for e in es]

    for hp in range(npair):
        grp = slice(hp * LANES, (hp + 1) * LANES)
        branch_o = (o1_ref[:, grp].astype(F32), osc_ref[hp], osc_ref[npair + hp])
        acc = jnp.zeros((tm, LANES), F32)
        for w, o in zip(ws, branch_o):
            wa = jnp.sum(jnp.where(lane == 2 * hp, w, 0.0), axis=-1, keepdims=True)
            wb = jnp.sum(jnp.where(lane == 2 * hp + 1, w, 0.0), axis=-1, keepdims=True)
            acc = acc + jnp.where(first_half, wa, wb) * o
        ob_ref[:, grp] = acc.astype(BF16)


def _out(x2, oa, os_, ls, wo, g2, w1, w2, gf, *, seq, tm, ff_chunk):
    T = x2.shape[0]
    spb = seq // tm
    nt = T // tm
    prev = lambda s: jnp.maximum(s - 1, 0)
    cur = lambda s: jnp.minimum(s, nt - 1)
    row_prev = lambda n: pl.BlockSpec((tm, n), lambda s: (prev(s), 0))
    row_cur = lambda n: pl.BlockSpec((tm, n), lambda s: (cur(s), 0))
    strided = lambda d, n: pl.BlockSpec((None, d, tm // d, n), lambda s: (cur(s) // spb, 0, cur(s) % spb, 0))
    const = lambda a: pl.BlockSpec(a.shape, lambda s: (0,) * a.ndim, pipeline_mode=pl.Buffered(1))
    W = DSA_WIDTH
    return pl.pallas_call(
        functools.partial(_out_kernel, ff_chunk=ff_chunk),
        grid=(nt + 1,),
        in_specs=[row_prev(D_MODEL), row_prev(GLA_WIDTH), row_cur(W), strided(4, W), strided(16, W),
                  row_cur(LANES), strided(4, LANES), strided(16, LANES),
                  const(wo), const(g2), const(w1), const(w2), const(gf)],
        out_specs=row_prev(D_MODEL),
        out_shape=jax.ShapeDtypeStruct((T, D_MODEL), F32),
        scratch_shapes=[pltpu.VMEM((2 * (W // LANES), tm, LANES), F32),
                        pltpu.VMEM((2, tm, LANES), F32),
                        pltpu.VMEM((tm, W), BF16)],
        compiler_params=pltpu.CompilerParams(dimension_semantics=("arbitrary",),
                                             vmem_limit_bytes=VMEM_LIMIT),
        name="out_mlp",
    )(x2, oa, *os_, *ls, wo, g2, w1, w2, gf)


def kernel(x, attn_norm_g, w_in, gla_gate_w2, gla_gate_b, gla_norm_g, rel_bias, w_out, mlp_norm_g,
           w_ff1, w_ff2, final_norm_g):
    batch, seq, _ = x.shape
    assert seq % (DSA_PATTERN[-1][1] * DSA_BLOCK) == 0
    T = batch * seq
    x2 = x.reshape(T, D_MODEL)

    w = w_in[0]
    splits = np.cumsum([GLA_QK, GLA_QK, GLA_WIDTH, GLA_WIDTH, GLA_RANK, DSA_WIDTH, DSA_WIDTH, DSA_WIDTH])[:-1]
    wq, wk, wv, wr, wg, wdq, wdk, wdv = jnp.split(w, [int(s) for s in splits], axis=1)
    wq = (wq * GLA_DK ** -0.5).astype(BF16)
    wk = wk.astype(BF16)
    wv = wv.astype(BF16)
    wr = wr.astype(BF16)
    wg = jnp.pad(wg, ((0, 0), (0, RANK_PAD - GLA_RANK))).astype(BF16)
    w2 = jnp.pad(gla_gate_w2[0], ((0, RANK_PAD - GLA_RANK), (0, 0))).astype(BF16)
    gb = gla_gate_b[0].astype(F32).reshape(1, GLA_QK)
    wd = jnp.concatenate([wdq * (DSA_DH ** -0.5 * LOG2E), wdk, wdv], axis=1).astype(BF16)

    gq, gv, gr, d1, kt, glog, glogt, d4, d16, kt1, kt4 = _proj(
        x2, attn_norm_g[0].reshape(1, D_MODEL).astype(F32), wq, wv, wr, wk, wg, w2, gb, wd,
        batch=batch, seq=seq, tm=512)

    o_a = _gla(gq, kt, gv, gr, glog, glogt, gla_norm_g[0].reshape(1, GLA_WIDTH).astype(F32),
               batch=batch, seq=seq, tg=512)

    bias = _bias_tables(rel_bias)
    os_, ls = [], []
    branches = ((d1.reshape(batch, 1, seq, 3 * DSA_WIDTH), kt1), (d4, kt4), (d16, None))
    for branch, (xd, ktd) in enumerate(branches):
        o, lse = _dsa_branch(xd, ktd, bias, branch, rows_per_step=2048)
        os_.append(o)
        ls.append(lse)
    os_[0] = os_[0].reshape(T, DSA_WIDTH)
    ls[0] = ls[0].reshape(T, LANES)

    y = _out(x2, o_a, os_, ls, w_out[0].astype(BF16), mlp_norm_g[0].reshape(1, D_MODEL).astype(F32),
             w_ff1[0].astype(BF16), w_ff2[0].astype(BF16), final_norm_g.reshape(1, D_MODEL).astype(F32),
             seq=seq, tm=512, ff_chunk=1024)
    return y.reshape(batch, seq, D_MODEL)
```

```python
import functools
import math

import numpy as np
import jax
import jax.numpy as jnp
from jax import lax
from jax.experimental import pallas as pl
from jax.experimental.pallas import tpu as pltpu

D_MODEL = 1024
GLA_WIDTH = 512
GLA_HEADS = 4
GLA_DK = 64
GLA_DV = 128
GLA_QK = GLA_HEADS * GLA_DK
GLA_RANK = 16
GLA_TAU = 16.0
GLA_CHUNK = 64
DSA_WIDTH = 512
DSA_HEADS = 8
DSA_DH = 64
DSA_PATTERN = ((128, 1), (512, 4), (2048, 16))
DSA_BLOCK = 128
REL_BUCKETS = 32
REL_MAX_DIST = 2048
D_FF = 4096
EPS = 1e-6
NEG = -1e30
LOG2E = math.log2(math.e)

LANES = 128
RANK_PAD = LANES
VMEM_LIMIT = 56 * 1024 * 1024

F32 = jnp.float32
BF16 = jnp.bfloat16

_NT = (((1,), (1,)), ((), ()))


def _dot(a, b):
    return jnp.dot(a, b, preferred_element_type=F32)


def _dot_nt(a, b):
    return lax.dot_general(a, b, _NT, preferred_element_type=F32)


def _split_hi_lo(x):
    hi = x.astype(BF16)
    lo = (x - hi.astype(F32)).astype(BF16)
    return hi, lo


def _log_sigmoid(x):
    return jnp.minimum(x, 0.0) - jnp.log1p(jnp.exp(-jnp.abs(x)))


def _bucket_tables():
    max_exact = REL_BUCKETS // 2
    L = DSA_BLOCK
    steps = L + np.arange(L)[:, None] - np.arange(2 * L)[None, :]
    tables = []
    for window, dilation in DSA_PATTERN:
        span = window // dilation
        in_band = (steps >= 0) & (steps <= span)
        n = np.maximum(steps * dilation, 0)
        large = max_exact + (np.log(np.maximum(n, 1) / max_exact)
                             / math.log(REL_MAX_DIST / max_exact)
                             * (REL_BUCKETS - max_exact)).astype(np.int32)
        large = np.minimum(large, REL_BUCKETS - 1)
        bucket = np.where(n < max_exact, n, large).astype(np.int32)
        tables.append(np.where(in_band, bucket, -1).astype(np.int32))
    return np.stack(tables)


def _bias_kernel(rb_ref, bt_ref, out_ref):
    bt = bt_ref[0]

    def one_head(h, carry):
        acc = jnp.full(bt.shape, NEG, F32)
        for b in range(REL_BUCKETS):
            acc = jnp.where(bt == b, rb_ref[b, h] * LOG2E, acc)
        out_ref[0, h] = acc
        return carry

    lax.fori_loop(0, DSA_HEADS, one_head, 0)


def _bias_tables(rel_bias):
    nb = len(DSA_PATTERN)
    L = DSA_BLOCK
    return pl.pallas_call(
        _bias_kernel,
        grid=(nb,),
        in_specs=[pl.BlockSpec(memory_space=pltpu.SMEM),
                  pl.BlockSpec((1, L, 2 * L), lambda d: (d, 0, 0))],
        out_specs=pl.BlockSpec((1, DSA_HEADS, L, 2 * L), lambda d: (d, 0, 0, 0)),
        out_shape=jax.ShapeDtypeStruct((nb, DSA_HEADS, L, 2 * L), F32),
        name="dsa_bias",
    )(rel_bias.astype(F32), jnp.asarray(_bucket_tables()))


def _proj_kernel(x_ref, g_ref, wq_ref, wv_ref, wr_ref, wk_ref, wg_ref, w2_ref, b_ref, wd_ref,
                 gq_ref, gv_ref, gr_ref, d1_ref, kt_ref, glog_ref, glogt_ref, d4_ref, d16_ref,
                 kt1_ref, kt4_ref, dsc_ref, t4_ref):
    tm = x_ref.shape[0]
    ncol = 3 * DSA_WIDTH // LANES
    kcols = range(DSA_WIDTH // LANES, 2 * DSA_WIDTH // LANES)
    n4 = tm // 4
    n16 = tm // 16

    x = x_ref[...]
    ms = jnp.mean(x * x, axis=-1, keepdims=True)
    nx = (x * lax.rsqrt(ms + EPS) * g_ref[...]).astype(BF16)

    kt_ref[0] = _dot(nx, wk_ref[...]).T.astype(BF16)
    glow = _dot(nx, wg_ref[...]).astype(BF16)
    glog = _log_sigmoid(_dot(glow, w2_ref[...]) + b_ref[...]) * (LOG2E / GLA_TAU)
    glog_ref[...] = glog
    glogt_ref[0] = glog.T

    for j in range(ncol // 2):
        res = _dot(nx, wd_ref[:, 2 * j * LANES:(2 * j + 2) * LANES])
        halves = {2 * j: res[:, :LANES], 2 * j + 1: res[:, LANES:]}
        for c in halves:
            dsc_ref[c] = halves[c]
        for c in halves:
            cols = slice(c * LANES, (c + 1) * LANES)
            is_key = c in kcols
            krows = slice((c - kcols[0]) * LANES, (c - kcols[0] + 1) * LANES)
            qv = c if c < kcols[0] else c - len(kcols)
            qv_cols = slice(qv * LANES, (qv + 1) * LANES)
            if is_key:
                for jb in range(tm // DSA_BLOCK):
                    blk_rows = slice(jb * DSA_BLOCK, (jb + 1) * DSA_BLOCK)
                    kt1_ref[jb, krows, :] = dsc_ref[c, blk_rows, :].T.astype(BF16)
            else:
                d1_ref[:, qv_cols] = halves[c].astype(BF16)
            for r4 in range(4):
                sub = dsc_ref[c, pl.ds(r4, n4, stride=4), :]
                t4_ref[c, r4 * n4:(r4 + 1) * n4, :] = sub
                if is_key:
                    kt4_ref[r4, krows, :] = sub.T.astype(BF16)
                else:
                    d4_ref[r4, :, qv_cols] = sub.astype(BF16)
            for r4 in range(4):
                for r2 in range(4):
                    d16_ref[r4 + 4 * r2, :, cols] = (
                        t4_ref[c, pl.ds(r4 * n4 + r2, n16, stride=4), :].astype(BF16))

    gq_ref[...] = _dot(nx, wq_ref[...]).astype(BF16)
    gv_ref[...] = _dot(nx, wv_ref[...]).astype(BF16)
    gr_ref[...] = _dot(nx, wr_ref[...]).astype(BF16)


def _proj(x2, g, wq, wv, wr, wk, wg, w2, b, wd, *, batch, seq, tm):
    T = batch * seq
    spb = seq // tm
    nt = T // tm
    full = lambda a: pl.BlockSpec(a.shape, lambda s: (0,) * a.ndim)
    row_cur = lambda n: pl.BlockSpec((tm, n), lambda s: (s, 0))
    row_prev = row_cur
    colt = pl.BlockSpec((1, GLA_QK, tm), lambda s: (s // spb, 0, s % spb))
    W, L = DSA_WIDTH, DSA_BLOCK
    W2, W3 = 2 * W, 3 * W
    strided = lambda d, n: pl.BlockSpec((None, d, tm // d, n), lambda s: (s // spb, 0, s % spb, 0))
    assert tm == 4 * DSA_BLOCK
    kt1_spec = pl.BlockSpec((None, tm // L, None, W, L), lambda s: (s // spb, s % spb, 0, 0, 0))
    kt4_spec = pl.BlockSpec((None, None, 4, W, L), lambda s: (s // spb, s % spb, 0, 0, 0))
    return pl.pallas_call(
        _proj_kernel,
        grid=(nt,),
        in_specs=[row_cur(D_MODEL)] + [full(a) for a in (g, wq, wv, wr, wk, wg, w2, b, wd)],
        out_specs=[row_cur(GLA_QK), row_cur(GLA_WIDTH), row_cur(GLA_WIDTH), row_cur(W2),
                   colt, row_prev(GLA_QK), colt, strided(4, W2), strided(16, W3), kt1_spec, kt4_spec],
        out_shape=[jax.ShapeDtypeStruct((T, GLA_QK), BF16),
                   jax.ShapeDtypeStruct((T, GLA_WIDTH), BF16),
                   jax.ShapeDtypeStruct((T, GLA_WIDTH), BF16),
                   jax.ShapeDtypeStruct((T, W2), BF16),
                   jax.ShapeDtypeStruct((batch, GLA_QK, seq), BF16),
                   jax.ShapeDtypeStruct((T, GLA_QK), F32),
                   jax.ShapeDtypeStruct((batch, GLA_QK, seq), F32),
                   jax.ShapeDtypeStruct((batch, 4, seq // 4, W2), BF16),
                   jax.ShapeDtypeStruct((batch, 16, seq // 16, W3), BF16),
                   jax.ShapeDtypeStruct((batch, seq // L, 1, W, L), BF16),
                   jax.ShapeDtypeStruct((batch, seq // (4 * L), 4, W, L), BF16)],
        scratch_shapes=[pltpu.VMEM((W3 // LANES, tm, LANES), F32)] * 2,
        compiler_params=pltpu.CompilerParams(dimension_semantics=("arbitrary",),
                                             vmem_limit_bytes=VMEM_LIMIT),
        name="proj",
    )(x2, g, wq, wv, wr, wk, wg, w2, b, wd)


def _gla_kernel(gq_ref, kt_ref, gv_ref, gr_ref, glog_ref, glogt_ref, gn_ref, o_ref, s_ref, *, pairs):
    C = GLA_CHUNK
    P = 2 * C

    @pl.when(pl.program_id(1) == 0)
    def _():
        s_ref[...] = jnp.zeros_like(s_ref)

    ri = lax.broadcasted_iota(jnp.int32, (P, P), 0)
    ci = lax.broadcasted_iota(jnp.int32, (P, P), 1)
    same_chunk = (ri < C) == (ci < C)
    causal = same_chunk & (ci <= ri)
    low = jnp.where(causal, 1.0, 0.0).astype(BF16)
    upp = jnp.where(same_chunk & (ri <= ci), 1.0, 0.0).astype(BF16)
    lane = lax.broadcasted_iota(jnp.int32, (P, LANES), 1)
    first_half = lane < C
    lane_t = lax.broadcasted_iota(jnp.int32, (GLA_QK, P), 1)
    first_t = lane_t < C

    zero16 = jnp.zeros((), BF16)
    state = [s_ref[h * GLA_DK:(h + 1) * GLA_DK, :] for h in range(GLA_HEADS)]

    heads = range(GLA_HEADS)
    hk = [slice(h * GLA_DK, (h + 1) * GLA_DK) for h in heads]
    hv = [slice(h * GLA_DV, (h + 1) * GLA_DV) for h in heads]
    grp = [slice((h // 2) * LANES, (h // 2 + 1) * LANES) for h in heads]

    def decay_stage(p):
        rows = slice(p * P, (p + 1) * P)
        g_hi, g_lo = _split_hi_lo(glog_ref[rows, :])
        b = _dot(low, g_hi) + _dot(low, g_lo)
        gt = glogt_ref[0, :, rows]
        gt_hi, gt_lo = _split_hi_lo(gt)
        bt = _dot(gt_hi, upp) + _dot(gt_lo, upp)
        tot_a = jnp.sum(jnp.where(first_t, gt, 0.0), axis=-1, keepdims=True)
        tot_b = jnp.sum(jnp.where(first_t, 0.0, gt), axis=-1, keepdims=True)
        return dict(rows=rows, b=b, bt=bt, dec_a=jnp.exp2(tot_a), dec_b=jnp.exp2(tot_b))

    def score_stage(c):
        rows = c["rows"]
        qd = (gq_ref[rows, :].astype(F32) * jnp.exp2(c["b"])).astype(BF16)
        kinv_t = (kt_ref[0, :, rows].astype(F32) * jnp.exp2(-c["bt"])).astype(BF16)
        kinv_a = jnp.where(first_t, kinv_t, zero16)
        kinv_b = jnp.where(first_t, zero16, kinv_t)
        c["v"] = [gv_ref[rows, hv[h]] for h in heads]
        c["qm"] = [jnp.where(first_half if h % 2 == 0 else ~first_half, qd[:, grp[h]], zero16) for h in heads]
        c["att"] = [_dot(c["qm"][h], kinv_t[grp[h], :]) for h in heads]
        c["upd_a"] = [_dot(kinv_a[hk[h], :], c["v"][h]) for h in heads]
        c["upd_b"] = [_dot(kinv_b[hk[h], :], c["v"][h]) for h in heads]

    def output_stage(c):
        o = []
        s_a = list(state)
        s_b = [c["dec_a"][hk[h], :] * (s_a[h] + c["upd_a"][h]) for h in heads]
        for h in heads:
            state[h] = c["dec_b"][hk[h], :] * (s_b[h] + c["upd_b"][h])
        for h in heads:
            att = jnp.where(causal, c["att"][h], 0.0).astype(BF16)
            pair_a = [s_a[h], s_a[h ^ 1]] if h % 2 == 0 else [s_a[h ^ 1], s_a[h]]
            pair_b = [s_b[h], s_a[h ^ 1]] if h % 2 == 0 else [s_a[h ^ 1], s_b[h]]
            o_a = _dot(c["qm"][h][:C, :], jnp.concatenate(pair_a, axis=0).astype(BF16))
            o_b = _dot(c["qm"][h][C:, :], jnp.concatenate(pair_b, axis=0).astype(BF16))
            o.append(_dot(att, c["v"][h]) + jnp.concatenate([o_a, o_b], axis=0))
        c["o"] = o

    def norm_stage(c):
        rows = c["rows"]
        for h in heads:
            o = c["o"][h]
            o = o * lax.rsqrt(jnp.mean(o * o, axis=-1, keepdims=True) + EPS) * gn_ref[:, hv[h]]
            r = gr_ref[rows, hv[h]].astype(F32)
            o_ref[rows, hv[h]] = (o * (r * jax.nn.sigmoid(r))).astype(BF16)

    ctx = {}
    for t in range(pairs + 3):
        if t < pairs:
            ctx[t] = decay_stage(t)
        if 0 <= t - 1 < pairs:
            score_stage(ctx[t - 1])
        if 0 <= t - 2 < pairs:
            output_stage(ctx[t - 2])
        if 0 <= t - 3 < pairs:
            norm_stage(ctx.pop(t - 3))

    for h in heads:
        s_ref[hk[h], :] = state[h]


def _gla(gq, kt, gv, gr, glog, glogt, gn, *, batch, seq, tg):
    T = batch * seq
    spb = seq // tg
    row = lambda n: pl.BlockSpec((tg, n), lambda b, i: (b * spb + i, 0))
    colt = pl.BlockSpec((1, GLA_QK, tg), lambda b, i: (b, 0, i))
    return pl.pallas_call(
        functools.partial(_gla_kernel, pairs=tg // (2 * GLA_CHUNK)),
        grid=(batch, spb),
        in_specs=[row(GLA_QK), colt, row(GLA_WIDTH), row(GLA_WIDTH), row(GLA_QK), colt,
                  pl.BlockSpec((1, GLA_WIDTH), lambda b, i: (0, 0))],
        out_specs=row(GLA_WIDTH),
        out_shape=jax.ShapeDtypeStruct((T, GLA_WIDTH), BF16),
        scratch_shapes=[pltpu.VMEM((GLA_QK, GLA_DV), F32)],
        compiler_params=pltpu.CompilerParams(dimension_semantics=("arbitrary", "arbitrary"),
                                             vmem_limit_bytes=VMEM_LIMIT),
        name="gla",
    )(gq, kt, gv, gr, glog, glogt, gn)


def _dsa_kernel(q_ref, k_ref, v_ref, kh_ref, vh_ref, bias_ref, o_ref, st_ref, *, res, blocks, keys_transposed):
    L = DSA_BLOCK
    first_tile = pl.program_id(2) == 0
    lane = lax.broadcasted_iota(jnp.int32, (L, LANES), 1)
    first_half = lane < DSA_DH
    prev_cols = lax.broadcasted_iota(jnp.int32, (L, 2 * L), 1) < L

    st_tiles = {}
    pair_out = {}

    def score_stage(r, blk, h):
        rows = slice(blk * L, (blk + 1) * L)
        both = slice((blk - 1) * L, (blk + 1) * L)
        grp = slice((h // 2) * LANES, (h // 2 + 1) * LANES)
        if keys_transposed:
            k_prev = kh_ref[r, grp, :] if blk == 0 else k_ref[blk - 1, r, grp, :]
            kcat = jnp.concatenate([k_prev, k_ref[blk, r, grp, :]], axis=1)
        elif blk == 0:
            kcat = jnp.concatenate([kh_ref[r, :, grp], k_ref[r, rows, grp]], axis=0)
        else:
            kcat = k_ref[r, both, grp]
        own = first_half if h % 2 == 0 else ~first_half
        qm = jnp.where(own, q_ref[r, rows, grp], jnp.zeros((), BF16))
        return _dot(qm, kcat) if keys_transposed else _dot_nt(qm, kcat)

    def value_stage(r, blk, h, s):
        rows = slice(blk * L, (blk + 1) * L)
        grp = slice((h // 2) * LANES, (h // 2 + 1) * LANES)
        bias = bias_ref[h]
        if blk == 0:
            bias = jnp.where(prev_cols & first_tile, NEG, bias)
        s = s + bias
        m = jnp.max(s, axis=-1, keepdims=True)
        p = jnp.exp2(s - m)
        den = jnp.sum(p, axis=-1, keepdims=True)
        st_old = st_tiles.get((r, blk), jnp.zeros((L, LANES), F32))
        st_tiles[r, blk] = jnp.where(lane == h, m, jnp.where(lane == DSA_HEADS + h, den, st_old))
        if blk == 0:
            vcat = jnp.concatenate([vh_ref[r, :, grp], v_ref[r, rows, grp]], axis=0)
        else:
            vcat = v_ref[r, both_rows(blk), grp]
        out = _dot(p.astype(BF16), vcat)
        if h % 2 == 0:
            pair_out[r, blk] = out
        else:
            o_ref[r, rows, grp] = jnp.where(first_half, pair_out.pop((r, blk)), out).astype(BF16)
        if h == DSA_HEADS - 1:
            st_ref[r, rows, :] = st_tiles.pop((r, blk))

    def both_rows(blk):
        return slice((blk - 1) * L, (blk + 1) * L)

    items = [(r, blk, h) for r in range(res) for blk in range(blocks) for h in range(DSA_HEADS)]
    lag = 1
    pending = {}
    for t in range(len(items) + lag):
        if t < len(items):
            pending[t] = score_stage(*items[t])
        if t - lag >= 0:
            value_stage(*items[t - lag], pending.pop(t - lag))


def _dsa_branch(x, kt, bias, branch, *, rows_per_step):
    batch, d, n, _ = x.shape
    tr = min(rows_per_step, n)
    res = min(rows_per_step // tr, d)
    hb = tr // DSA_BLOCK
    W = DSA_WIDTH
    prev_blk = lambda i: jnp.maximum(i * hb - 1, 0)
    blk = lambda part: pl.BlockSpec((None, res, tr, W), lambda b, r, i: (b, r, i, part))
    halo = lambda part: pl.BlockSpec((None, res, DSA_BLOCK, W), lambda b, r, i: (b, r, prev_blk(i), part))
    if kt is None:
        v_part = 2
        k_arg, k_spec, kh_spec = x, blk(1), halo(1)
    else:
        v_part = 1
        k_arg = kt
        k_spec = pl.BlockSpec((None, hb, res, W, DSA_BLOCK), lambda b, r, i: (b, i, r, 0, 0))
        kh_spec = pl.BlockSpec((None, None, res, W, DSA_BLOCK), lambda b, r, i: (b, prev_blk(i), r, 0, 0))
    return pl.pallas_call(
        functools.partial(_dsa_kernel, res=res, blocks=hb, keys_transposed=kt is not None),
        grid=(batch, d // res, n // tr),
        in_specs=[blk(0), k_spec, blk(v_part), kh_spec, halo(v_part),
                  pl.BlockSpec((None, DSA_HEADS, DSA_BLOCK, 2 * DSA_BLOCK), lambda b, r, i: (branch, 0, 0, 0))],
        out_specs=[pl.BlockSpec((None, res, tr, W), lambda b, r, i: (b, r, i, 0)),
                   pl.BlockSpec((None, res, tr, LANES), lambda b, r, i: (b, r, i, 0))],
        out_shape=[jax.ShapeDtypeStruct((batch, d, n, W), BF16),
                   jax.ShapeDtypeStruct((batch, d, n, LANES), F32)],
        compiler_params=pltpu.CompilerParams(dimension_semantics=("arbitrary",) * 3,
                                             vmem_limit_bytes=VMEM_LIMIT),
        name=f"dsa_d{d}",
    )(x, k_arg, x, k_arg, x, bias)


def _out_kernel(x_ref, oa_ref, o1_ref, o4_ref, o16_ref, l1_ref, l4_ref, l16_ref,
                wo_ref, g2_ref, w1_ref, w2_ref, gf_ref, y_ref, osc_ref, lsc_ref, ob_ref, *, ff_chunk):
    tm = x_ref.shape[0]
    npair = DSA_HEADS // 2
    lane = lax.broadcasted_iota(jnp.int32, (tm, LANES), 1)
    first_half = lane < DSA_DH

    @pl.when(pl.program_id(0) == 0)
    def _():
        ob_ref[...] = jnp.zeros_like(ob_ref)

    mixed = _dot(oa_ref[...], wo_ref[:GLA_WIDTH, :]) + _dot(ob_ref[...], wo_ref[GLA_WIDTH:, :])
    h = x_ref[...] + mixed
    nm = (h * lax.rsqrt(jnp.mean(h * h, axis=-1, keepdims=True) + EPS) * g2_ref[...]).astype(BF16)
    ff = None
    for c in range(D_FF // ff_chunk):
        cols = slice(c * ff_chunk, (c + 1) * ff_chunk)
        a = jnp.maximum(_dot(nm, w1_ref[:, cols]), 0.0)
        d = _dot((a * a).astype(BF16), w2_ref[cols, :])
        ff = d if ff is None else ff + d
    h = h + ff
    y_ref[...] = h * lax.rsqrt(jnp.mean(h * h, axis=-1, keepdims=True) + EPS) * gf_ref[...]

    for j, (d, o_ref, l_ref) in enumerate(((4, o4_ref, l4_ref), (16, o16_ref, l16_ref))):
        for r in range(d):
            lsc_ref[j, pl.ds(r, tm // d, stride=d), :] = l_ref[r]
            for hp in range(npair):
                osc_ref[j * npair + hp, pl.ds(r, tm // d, stride=d), :] = (
                    o_ref[r, :, hp * LANES:(hp + 1) * LANES].astype(F32))

    sts = (l1_ref[...], lsc_ref[0], lsc_ref[1])
    m = jnp.maximum(jnp.maximum(sts[0], sts[1]), sts[2])
    es = [jnp.exp2(st - m) for st in sts]
    total = sum(e * pltpu.roll(st, LANES - DSA_HEADS, axis=1) for e, st in zip(es, sts))
    inv = 1.0 / jnp.where(lane < DSA_HEADS, total, 1.0)
    ws = [e * inv for e in es]

    for hp in range(npair):
        grp = slice(hp * LANES, (hp + 1) * LANES)
        branch_o = (o1_ref[:, grp].astype(F32), osc_ref[hp], osc_ref[npair + hp])
        acc = jnp.zeros((tm, LANES), F32)
        for w, o in zip(ws, branch_o):
            wa = jnp.sum(jnp.where(lane == 2 * hp, w, 0.0), axis=-1, keepdims=True)
            wb = jnp.sum(jnp.where(lane == 2 * hp + 1, w, 0.0), axis=-1, keepdims=True)
            acc = acc + jnp.where(first_half, wa, wb) * o
        ob_ref[:, grp] = acc.astype(BF16)


def _out(x2, oa, os_, ls, wo, g2, w1, w2, gf, *, seq, tm, ff_chunk):
    T = x2.shape[0]
    spb = seq // tm
    nt = T // tm
    prev = lambda s: jnp.maximum(s - 1, 0)
    cur = lambda s: jnp.minimum(s, nt - 1)
    row_prev = lambda n: pl.BlockSpec((tm, n), lambda s: (prev(s), 0))
    row_cur = lambda n: pl.BlockSpec((tm, n), lambda s: (cur(s), 0))
    strided = lambda d, n: pl.BlockSpec((None, d, tm // d, n), lambda s: (cur(s) // spb, 0, cur(s) % spb, 0))
    const = lambda a: pl.BlockSpec(a.shape, lambda s: (0,) * a.ndim, pipeline_mode=pl.Buffered(1))
    W = DSA_WIDTH
    return pl.pallas_call(
        functools.partial(_out_kernel, ff_chunk=ff_chunk),
        grid=(nt + 1,),
        in_specs=[row_prev(D_MODEL), row_prev(GLA_WIDTH), row_cur(W), strided(4, W), strided(16, W),
                  row_cur(LANES), strided(4, LANES), strided(16, LANES),
                  const(wo), const(g2), const(w1), const(w2), const(gf)],
        out_specs=row_prev(D_MODEL),
        out_shape=jax.ShapeDtypeStruct((T, D_MODEL), F32),
        scratch_shapes=[pltpu.VMEM((2 * (W // LANES), tm, LANES), F32),
                        pltpu.VMEM((2, tm, LANES), F32),
                        pltpu.VMEM((tm, W), BF16)],
        compiler_params=pltpu.CompilerParams(dimension_semantics=("arbitrary",),
                                             vmem_limit_bytes=VMEM_LIMIT),
        name="out_mlp",
    )(x2, oa, *os_, *ls, wo, g2, w1, w2, gf)


def kernel(x, attn_norm_g, w_in, gla_gate_w2, gla_gate_b, gla_norm_g, rel_bias, w_out, mlp_norm_g,
           w_ff1, w_ff2, final_norm_g):
    batch, seq, _ = x.shape
    assert seq % (DSA_PATTERN[-1][1] * DSA_BLOCK) == 0
    T = batch * seq
    x2 = x.reshape(T, D_MODEL)

    w = w_in[0]
    splits = np.cumsum([GLA_QK, GLA_QK, GLA_WIDTH, GLA_WIDTH, GLA_RANK, DSA_WIDTH, DSA_WIDTH, DSA_WIDTH])[:-1]
    wq, wk, wv, wr, wg, wdq, wdk, wdv = jnp.split(w, [int(s) for s in splits], axis=1)
    wq = (wq * GLA_DK ** -0.5).astype(BF16)
    wk = wk.astype(BF16)
    wv = wv.astype(BF16)
    wr = wr.astype(BF16)
    wg = jnp.pad(wg, ((0, 0), (0, RANK_PAD - GLA_RANK))).astype(BF16)
    w2 = jnp.pad(gla_gate_w2[0], ((0, RANK_PAD - GLA_RANK), (0, 0))).astype(BF16)
    gb = gla_gate_b[0].astype(F32).reshape(1, GLA_QK)
    wd = jnp.concatenate([wdq * (DSA_DH ** -0.5 * LOG2E), wdk, wdv], axis=1).astype(BF16)

    gq, gv, gr, d1, kt, glog, glogt, d4, d16, kt1, kt4 = _proj(
        x2, attn_norm_g[0].reshape(1, D_MODEL).astype(F32), wq, wv, wr, wk, wg, w2, gb, wd,
        batch=batch, seq=seq, tm=512)

    o_a = _gla(gq, kt, gv, gr, glog, glogt, gla_norm_g[0].reshape(1, GLA_WIDTH).astype(F32),
               batch=batch, seq=seq, tg=512)

    bias = _bias_tables(rel_bias)
    os_, ls = [], []
    branches = ((d1.reshape(batch, 1, seq, 2 * DSA_WIDTH), kt1), (d4, kt4), (d16, None))
    for branch, (xd, ktd) in enumerate(branches):
        o, lse = _dsa_branch(xd, ktd, bias, branch, rows_per_step=2048)
        os_.append(o)
        ls.append(lse)
    os_[0] = os_[0].reshape(T, DSA_WIDTH)
    ls[0] = ls[0].reshape(T, LANES)

    y = _out(x2, o_a, os_, ls, w_out[0].astype(BF16), mlp_norm_g[0].reshape(1, D_MODEL).astype(F32),
             w_ff1[0].astype(BF16), w_ff2[0].astype(BF16), final_norm_g.reshape(1, D_MODEL).astype(F32),
             seq=seq, tm=512, ff_chunk=1024)
    return y.reshape(batch, seq, D_MODEL)
```

```python
import functools
import math

import numpy as np
import jax
import jax.numpy as jnp
from jax import lax
from jax.experimental import pallas as pl
from jax.experimental.pallas import tpu as pltpu

D_MODEL = 1024
GLA_WIDTH = 512
GLA_HEADS = 4
GLA_DK = 64
GLA_DV = 128
GLA_QK = GLA_HEADS * GLA_DK
GLA_RANK = 16
GLA_TAU = 16.0
GLA_CHUNK = 64
DSA_WIDTH = 512
DSA_HEADS = 8
DSA_DH = 64
DSA_PATTERN = ((128, 1), (512, 4), (2048, 16))
DSA_BLOCK = 128
REL_BUCKETS = 32
REL_MAX_DIST = 2048
D_FF = 4096
EPS = 1e-6
NEG = -1e30
LOG2E = math.log2(math.e)

LANES = 128
RANK_PAD = LANES
VMEM_LIMIT = 56 * 1024 * 1024

F32 = jnp.float32
BF16 = jnp.bfloat16

_NT = (((1,), (1,)), ((), ()))


def _dot(a, b):
    return jnp.dot(a, b, preferred_element_type=F32)


def _dot_nt(a, b):
    return lax.dot_general(a, b, _NT, preferred_element_type=F32)


def _split_hi_lo(x):
    hi = x.astype(BF16)
    lo = (x - hi.astype(F32)).astype(BF16)
    return hi, lo


def _log_sigmoid(x):
    return jnp.minimum(x, 0.0) - jnp.log1p(jnp.exp(-jnp.abs(x)))


def _bucket_tables():
    max_exact = REL_BUCKETS // 2
    L = DSA_BLOCK
    steps = L + np.arange(L)[:, None] - np.arange(2 * L)[None, :]
    tables = []
    for window, dilation in DSA_PATTERN:
        span = window // dilation
        in_band = (steps >= 0) & (steps <= span)
        n = np.maximum(steps * dilation, 0)
        large = max_exact + (np.log(np.maximum(n, 1) / max_exact)
                             / math.log(REL_MAX_DIST / max_exact)
                             * (REL_BUCKETS - max_exact)).astype(np.int32)
        large = np.minimum(large, REL_BUCKETS - 1)
        bucket = np.where(n < max_exact, n, large).astype(np.int32)
        tables.append(np.where(in_band, bucket, -1).astype(np.int32))
    return np.stack(tables)


def _bias_kernel(rb_ref, bt_ref, out_ref):
    bt = bt_ref[0]

    def one_head(h, carry):
        acc = jnp.full(bt.shape, NEG, F32)
        for b in range(REL_BUCKETS):
            acc = jnp.where(bt == b, rb_ref[b, h] * LOG2E, acc)
        out_ref[0, h] = acc
        return carry

    lax.fori_loop(0, DSA_HEADS, one_head, 0)


def _bias_tables(rel_bias):
    nb = len(DSA_PATTERN)
    L = DSA_BLOCK
    return pl.pallas_call(
        _bias_kernel,
        grid=(nb,),
        in_specs=[pl.BlockSpec(memory_space=pltpu.SMEM),
                  pl.BlockSpec((1, L, 2 * L), lambda d: (d, 0, 0))],
        out_specs=pl.BlockSpec((1, DSA_HEADS, L, 2 * L), lambda d: (d, 0, 0, 0)),
        out_shape=jax.ShapeDtypeStruct((nb, DSA_HEADS, L, 2 * L), F32),
        name="dsa_bias",
    )(rel_bias.astype(F32), jnp.asarray(_bucket_tables()))


def _proj_kernel(x_ref, g_ref, wq_ref, wv_ref, wr_ref, wk_ref, wg_ref, w2_ref, b_ref, wd_ref,
                 gq_ref, gv_ref, gr_ref, d1_ref, kt_ref, glog_ref, glogt_ref, d4_ref, d16_ref,
                 kt1_ref, kt4_ref, dsc_ref, t4_ref):
    tm = x_ref.shape[0]
    ncol = 3 * DSA_WIDTH // LANES
    kcols = range(DSA_WIDTH // LANES, 2 * DSA_WIDTH // LANES)
    n4 = tm // 4
    n16 = tm // 16

    x = x_ref[...]
    ms = jnp.mean(x * x, axis=-1, keepdims=True)
    nx = (x * lax.rsqrt(ms + EPS) * g_ref[...]).astype(BF16)

    kt_ref[0] = _dot(nx, wk_ref[...]).T.astype(BF16)
    glow = _dot(nx, wg_ref[...]).astype(BF16)
    glog = _log_sigmoid(_dot(glow, w2_ref[...]) + b_ref[...]) * (LOG2E / GLA_TAU)
    glog_ref[...] = glog
    glogt_ref[0] = glog.T

    for j in range(ncol // 2):
        res = _dot(nx, wd_ref[:, 2 * j * LANES:(2 * j + 2) * LANES])
        halves = {2 * j: res[:, :LANES], 2 * j + 1: res[:, LANES:]}
        for c in halves:
            dsc_ref[c] = halves[c]
        for c in halves:
            cols = slice(c * LANES, (c + 1) * LANES)
            is_key = c in kcols
            krows = slice((c - kcols[0]) * LANES, (c - kcols[0] + 1) * LANES)
            qv = c if c < kcols[0] else c - len(kcols)
            qv_cols = slice(qv * LANES, (qv + 1) * LANES)
            if is_key:
                for jb in range(tm // DSA_BLOCK):
                    blk_rows = slice(jb * DSA_BLOCK, (jb + 1) * DSA_BLOCK)
                    kt1_ref[jb, krows, :] = dsc_ref[c, blk_rows, :].T.astype(BF16)
            else:
                d1_ref[:, qv_cols] = halves[c].astype(BF16)
            for r4 in range(4):
                sub = dsc_ref[c, pl.ds(r4, n4, stride=4), :]
                t4_ref[c, r4 * n4:(r4 + 1) * n4, :] = sub
                if is_key:
                    kt4_ref[r4, krows, :] = sub.T.astype(BF16)
                else:
                    d4_ref[r4, :, qv_cols] = sub.astype(BF16)
            for r4 in range(4):
                for r2 in range(4):
                    d16_ref[r4 + 4 * r2, :, cols] = (
                        t4_ref[c, pl.ds(r4 * n4 + r2, n16, stride=4), :].astype(BF16))

    gq_ref[...] = _dot(nx, wq_ref[...]).astype(BF16)
    gv_ref[...] = _dot(nx, wv_ref[...]).astype(BF16)
    gr_ref[...] = _dot(nx, wr_ref[...]).astype(BF16)


def _proj(x2, g, wq, wv, wr, wk, wg, w2, b, wd, *, batch, seq, tm):
    T = batch * seq
    spb = seq // tm
    nt = T // tm
    full = lambda a: pl.BlockSpec(a.shape, lambda s: (0,) * a.ndim)
    row_cur = lambda n: pl.BlockSpec((tm, n), lambda s: (s, 0))
    row_prev = row_cur
    colt = pl.BlockSpec((1, GLA_QK, tm), lambda s: (s // spb, 0, s % spb))
    W, L = DSA_WIDTH, DSA_BLOCK
    W2, W3 = 2 * W, 3 * W
    strided = lambda d, n: pl.BlockSpec((None, d, tm // d, n), lambda s: (s // spb, 0, s % spb, 0))
    assert tm == 4 * DSA_BLOCK
    kt1_spec = pl.BlockSpec((None, tm // L, None, W, L), lambda s: (s // spb, s % spb, 0, 0, 0))
    kt4_spec = pl.BlockSpec((None, None, 4, W, L), lambda s: (s // spb, s % spb, 0, 0, 0))
    return pl.pallas_call(
        _proj_kernel,
        grid=(nt,),
        in_specs=[row_cur(D_MODEL)] + [full(a) for a in (g, wq, wv, wr, wk, wg, w2, b, wd)],
        out_specs=[row_cur(GLA_QK), row_cur(GLA_WIDTH), row_cur(GLA_WIDTH), row_cur(W2),
                   colt, row_prev(GLA_QK), colt, strided(4, W2), strided(16, W3), kt1_spec, kt4_spec],
        out_shape=[jax.ShapeDtypeStruct((T, GLA_QK), BF16),
                   jax.ShapeDtypeStruct((T, GLA_WIDTH), BF16),
                   jax.ShapeDtypeStruct((T, GLA_WIDTH), BF16),
                   jax.ShapeDtypeStruct((T, W2), BF16),
                   jax.ShapeDtypeStruct((batch, GLA_QK, seq), BF16),
                   jax.ShapeDtypeStruct((T, GLA_QK), F32),
                   jax.ShapeDtypeStruct((batch, GLA_QK, seq), F32),
                   jax.ShapeDtypeStruct((batch, 4, seq // 4, W2), BF16),
                   jax.ShapeDtypeStruct((batch, 16, seq // 16, W3), BF16),
                   jax.ShapeDtypeStruct((batch, seq // L, 1, W, L), BF16),
                   jax.ShapeDtypeStruct((batch, seq // (4 * L), 4, W, L), BF16)],
        scratch_shapes=[pltpu.VMEM((W3 // LANES, tm, LANES), F32)] * 2,
        compiler_params=pltpu.CompilerParams(dimension_semantics=("arbitrary",),
                                             vmem_limit_bytes=VMEM_LIMIT),
        name="proj",
    )(x2, g, wq, wv, wr, wk, wg, w2, b, wd)


def _gla_kernel(gq_ref, kt_ref, gv_ref, gr_ref, glog_ref, glogt_ref, gn_ref, o_ref, s_ref, *, pairs):
    C = GLA_CHUNK
    P = 2 * C

    @pl.when(pl.program_id(1) == 0)
    def _():
        s_ref[...] = jnp.zeros_like(s_ref)

    ri = lax.broadcasted_iota(jnp.int32, (P, P), 0)
    ci = lax.broadcasted_iota(jnp.int32, (P, P), 1)
    same_chunk = (ri < C) == (ci < C)
    causal = same_chunk & (ci <= ri)
    low = jnp.where(causal, 1.0, 0.0).astype(BF16)
    upp = jnp.where(same_chunk & (ri <= ci), 1.0, 0.0).astype(BF16)
    lane = lax.broadcasted_iota(jnp.int32, (P, LANES), 1)
    first_half = lane < C
    lane_t = lax.broadcasted_iota(jnp.int32, (GLA_QK, P), 1)
    first_t = lane_t < C

    zero16 = jnp.zeros((), BF16)
    state = [s_ref[h * GLA_DK:(h + 1) * GLA_DK, :] for h in range(GLA_HEADS)]

    heads = range(GLA_HEADS)
    hk = [slice(h * GLA_DK, (h + 1) * GLA_DK) for h in heads]
    hv = [slice(h * GLA_DV, (h + 1) * GLA_DV) for h in heads]
    grp = [slice((h // 2) * LANES, (h // 2 + 1) * LANES) for h in heads]

    def decay_stage(p):
        rows = slice(p * P, (p + 1) * P)
        g_hi, g_lo = _split_hi_lo(glog_ref[rows, :])
        b = _dot(low, g_hi) + _dot(low, g_lo)
        gt = glogt_ref[0, :, rows]
        gt_hi, gt_lo = _split_hi_lo(gt)
        bt = _dot(gt_hi, upp) + _dot(gt_lo, upp)
        tot_a = jnp.sum(jnp.where(first_t, gt, 0.0), axis=-1, keepdims=True)
        tot_b = jnp.sum(jnp.where(first_t, 0.0, gt), axis=-1, keepdims=True)
        return dict(rows=rows, b=b, bt=bt, dec_a=jnp.exp2(tot_a), dec_b=jnp.exp2(tot_b))

    def score_stage(c):
        rows = c["rows"]
        qd = (gq_ref[rows, :].astype(F32) * jnp.exp2(c["b"])).astype(BF16)
        kinv_t = (kt_ref[0, :, rows].astype(F32) * jnp.exp2(-c["bt"])).astype(BF16)
        kinv_a = jnp.where(first_t, kinv_t, zero16)
        kinv_b = jnp.where(first_t, zero16, kinv_t)
        c["v"] = [gv_ref[rows, hv[h]] for h in heads]
        c["qm"] = [jnp.where(first_half if h % 2 == 0 else ~first_half, qd[:, grp[h]], zero16) for h in heads]
        c["att"] = [_dot(c["qm"][h], kinv_t[grp[h], :]) for h in heads]
        c["upd_a"] = [_dot(kinv_a[hk[h], :], c["v"][h]) for h in heads]
        c["upd_b"] = [_dot(kinv_b[hk[h], :], c["v"][h]) for h in heads]

    def output_stage(c):
        o = []
        s_a = list(state)
        s_b = [c["dec_a"][hk[h], :] * (s_a[h] + c["upd_a"][h]) for h in heads]
        for h in heads:
            state[h] = c["dec_b"][hk[h], :] * (s_b[h] + c["upd_b"][h])
        for h in heads:
            att = jnp.where(causal, c["att"][h], 0.0).astype(BF16)
            pair_a = [s_a[h], s_a[h ^ 1]] if h % 2 == 0 else [s_a[h ^ 1], s_a[h]]
            pair_b = [s_b[h], s_a[h ^ 1]] if h % 2 == 0 else [s_a[h ^ 1], s_b[h]]
            o_a = _dot(c["qm"][h][:C, :], jnp.concatenate(pair_a, axis=0).astype(BF16))
            o_b = _dot(c["qm"][h][C:, :], jnp.concatenate(pair_b, axis=0).astype(BF16))
            o.append(_dot(att, c["v"][h]) + jnp.concatenate([o_a, o_b], axis=0))
        c["o"] = o

    def norm_stage(c):
        rows = c["rows"]
        for h in heads:
            o = c["o"][h]
            o = o * lax.rsqrt(jnp.mean(o * o, axis=-1, keepdims=True) + EPS) * gn_ref[:, hv[h]]
            r = gr_ref[rows, hv[h]].astype(F32)
            o_ref[rows, hv[h]] = (o * (r * jax.nn.sigmoid(r))).astype(BF16)

    ctx = {}
    for t in range(pairs + 3):
        if t < pairs:
            ctx[t] = decay_stage(t)
        if 0 <= t - 1 < pairs:
            score_stage(ctx[t - 1])
        if 0 <= t - 2 < pairs:
            output_stage(ctx[t - 2])
        if 0 <= t - 3 < pairs:
            norm_stage(ctx.pop(t - 3))

    for h in heads:
        s_ref[hk[h], :] = state[h]


def _gla(gq, kt, gv, gr, glog, glogt, gn, *, batch, seq, tg):
    T = batch * seq
    spb = seq // tg
    row = lambda n: pl.BlockSpec((tg, n), lambda b, i: (b * spb + i, 0))
    colt = pl.BlockSpec((1, GLA_QK, tg), lambda b, i: (b, 0, i))
    return pl.pallas_call(
        functools.partial(_gla_kernel, pairs=tg // (2 * GLA_CHUNK)),
        grid=(batch, spb),
        in_specs=[row(GLA_QK), colt, row(GLA_WIDTH), row(GLA_WIDTH), row(GLA_QK), colt,
                  pl.BlockSpec((1, GLA_WIDTH), lambda b, i: (0, 0))],
        out_specs=row(GLA_WIDTH),
        out_shape=jax.ShapeDtypeStruct((T, GLA_WIDTH), BF16),
        scratch_shapes=[pltpu.VMEM((GLA_QK, GLA_DV), F32)],
        compiler_params=pltpu.CompilerParams(dimension_semantics=("arbitrary", "arbitrary"),
                                             vmem_limit_bytes=VMEM_LIMIT),
        name="gla",
    )(gq, kt, gv, gr, glog, glogt, gn)


def _dsa_kernel(q_ref, k_ref, v_ref, kh_ref, vh_ref, bias_ref, o_ref, st_ref, *, res, blocks, keys_transposed):
    L = DSA_BLOCK
    first_tile = pl.program_id(2) == 0
    lane = lax.broadcasted_iota(jnp.int32, (L, LANES), 1)
    first_half = lane < DSA_DH
    prev_cols = lax.broadcasted_iota(jnp.int32, (L, 2 * L), 1) < L

    st_tiles = {}
    pair_out = {}

    def score_stage(r, blk, h):
        rows = slice(blk * L, (blk + 1) * L)
        both = slice((blk - 1) * L, (blk + 1) * L)
        grp = slice((h // 2) * LANES, (h // 2 + 1) * LANES)
        if keys_transposed:
            k_prev = kh_ref[r, grp, :] if blk == 0 else k_ref[blk - 1, r, grp, :]
            kcat = jnp.concatenate([k_prev, k_ref[blk, r, grp, :]], axis=1)
        elif blk == 0:
            kcat = jnp.concatenate([kh_ref[r, :, grp], k_ref[r, rows, grp]], axis=0)
        else:
            kcat = k_ref[r, both, grp]
        own = first_half if h % 2 == 0 else ~first_half
        qm = jnp.where(own, q_ref[r, rows, grp], jnp.zeros((), BF16))
        return _dot(qm, kcat) if keys_transposed else _dot_nt(qm, kcat)

    def value_stage(r, blk, h, s):
        rows = slice(blk * L, (blk + 1) * L)
        grp = slice((h // 2) * LANES, (h // 2 + 1) * LANES)
        bias = bias_ref[h]
        if blk == 0:
            bias = jnp.where(prev_cols & first_tile, NEG, bias)
        s = s + bias
        m = jnp.max(s, axis=-1, keepdims=True)
        p = jnp.exp2(s - m)
        den = jnp.sum(p, axis=-1, keepdims=True)
        st_old = st_tiles.get((r, blk), jnp.zeros((L, LANES), F32))
        st_tiles[r, blk] = jnp.where(lane == h, m, jnp.where(lane == DSA_HEADS + h, den, st_old))
        if blk == 0:
            vcat = jnp.concatenate([vh_ref[r, :, grp], v_ref[r, rows, grp]], axis=0)
        else:
            vcat = v_ref[r, both_rows(blk), grp]
        out = _dot(p.astype(BF16), vcat)
        if h % 2 == 0:
            pair_out[r, blk] = out
        else:
            o_ref[r, rows, grp] = jnp.where(first_half, pair_out.pop((r, blk)), out).astype(BF16)
        if h == DSA_HEADS - 1:
            st_ref[r, rows, :] = st_tiles.pop((r, blk))

    def both_rows(blk):
        return slice((blk - 1) * L, (blk + 1) * L)

    items = [(r, blk, h) for r in range(res) for blk in range(blocks) for h in range(DSA_HEADS)]
    lag = 1
    pending = {}
    for t in range(len(items) + lag):
        if t < len(items):
            pending[t] = score_stage(*items[t])
        if t - lag >= 0:
            value_stage(*items[t - lag], pending.pop(t - lag))


def _dsa_branch(x, kt, bias, branch, *, rows_per_step):
    batch, d, n, _ = x.shape
    tr = min(rows_per_step, n)
    res = min(rows_per_step // tr, d)
    hb = tr // DSA_BLOCK
    W = DSA_WIDTH
    prev_blk = lambda i: jnp.maximum(i * hb - 1, 0)
    blk = lambda part: pl.BlockSpec((None, res, tr, W), lambda b, r, i: (b, r, i, part))
    halo = lambda part: pl.BlockSpec((None, res, DSA_BLOCK, W), lambda b, r, i: (b, r, prev_blk(i), part))
    if kt is None:
        v_part = 2
        k_arg, k_spec, kh_spec = x, blk(1), halo(1)
    else:
        v_part = 1
        k_arg = kt
        k_spec = pl.BlockSpec((None, hb, res, W, DSA_BLOCK), lambda b, r, i: (b, i, r, 0, 0))
        kh_spec = pl.BlockSpec((None, None, res, W, DSA_BLOCK), lambda b, r, i: (b, prev_blk(i), r, 0, 0))
    return pl.pallas_call(
        functools.partial(_dsa_kernel, res=res, blocks=hb, keys_transposed=kt is not None),
        grid=(batch, d // res, n // tr),
        in_specs=[blk(0), k_spec, blk(v_part), kh_spec, halo(v_part),
                  pl.BlockSpec((None, DSA_HEADS, DSA_BLOCK, 2 * DSA_BLOCK), lambda b, r, i: (branch, 0, 0, 0))],
        out_specs=[pl.BlockSpec((None, res, tr, W), lambda b, r, i: (b, r, i, 0)),
                   pl.BlockSpec((None, res, tr, LANES), lambda b, r, i: (b, r, i, 0))],
        out_shape=[jax.ShapeDtypeStruct((batch, d, n, W), BF16),
                   jax.ShapeDtypeStruct((batch, d, n, LANES), F32)],
        compiler_params=pltpu.CompilerParams(dimension_semantics=("arbitrary",) * 3,
                                             vmem_limit_bytes=VMEM_LIMIT),
        name=f"dsa_d{d}",
    )(x, k_arg, x, k_arg, x, bias)


def _out_kernel(x_ref, oa_ref, o1_ref, o4_ref, o16_ref, l1_ref, l4_ref, l16_ref,
                wo_ref, g2_ref, w1_ref, w2_ref, gf_ref, y_ref, osc_ref, lsc_ref, ob_ref, *, ff_chunk):
    tm = x_ref.shape[0]
    npair = DSA_HEADS // 2
    lane = lax.broadcasted_iota(jnp.int32, (tm, LANES), 1)
    first_half = lane < DSA_DH

    @pl.when(pl.program_id(0) == 0)
    def _():
        ob_ref[...] = jnp.zeros_like(ob_ref)

    mixed = _dot(oa_ref[...], wo_ref[:GLA_WIDTH, :]) + _dot(ob_ref[...], wo_ref[GLA_WIDTH:, :])
    h = x_ref[...] + mixed
    nm = (h * lax.rsqrt(jnp.mean(h * h, axis=-1, keepdims=True) + EPS) * g2_ref[...]).astype(BF16)
    ff = None
    for c in range(D_FF // ff_chunk):
        cols = slice(c * ff_chunk, (c + 1) * ff_chunk)
        a = jnp.maximum(_dot(nm, w1_ref[:, cols]), 0.0)
        d = _dot((a * a).astype(BF16), w2_ref[cols, :])
        ff = d if ff is None else ff + d
    h = h + ff
    y_ref[...] = h * lax.rsqrt(jnp.mean(h * h, axis=-1, keepdims=True) + EPS) * gf_ref[...]

    for j, (d, o_ref, l_ref) in enumerate(((4, o4_ref, l4_ref), (16, o16_ref, l16_ref))):
        for r in range(d):
            lsc_ref[j, pl.ds(r, tm // d, stride=d), :] = l_ref[r]
            for hp in range(npair):
                osc_ref[j * npair + hp, pl.ds(r, tm // d, stride=d), :] = (
                    o_ref[r, :, hp * LANES:(hp + 1) * LANES].astype(F32))

    sts = (l1_ref[...], lsc_ref[0], lsc_ref[1])
    m = jnp.maximum(jnp.maximum(sts[0], sts[1]), sts[2])
    es = [jnp.exp2(st - m) for st in sts]
    total = sum(e * pltpu.roll(st, LANES - DSA_HEADS, axis=1) for e, st in zip(es, sts))
    inv = 1.0 / jnp.where(lane < DSA_HEADS, total, 1.0)
    ws = [e * inv for e in es]

    for hp in range(npair):
        grp = slice(hp * LANES, (hp + 1) * LANES)
        branch_o = (o1_ref[:, grp].astype(F32), osc_ref[hp], osc_ref[npair + hp])
        acc = jnp.zeros((tm, LANES), F32)
        for w, o in zip(ws, branch_o):
            wa = jnp.sum(jnp.where(lane == 2 * hp, w, 0.0), axis=-1, keepdims=True)
            wb = jnp.sum(jnp.where(lane == 2 * hp + 1, w, 0.0), axis=-1, keepdims=True)
            acc = acc + jnp.where(first_half, wa, wb) * o
        ob_ref[:, grp] = acc.astype(BF16)


def _out(x2, oa, os_, ls, wo, g2, w1, w2, gf, *, seq, tm, ff_chunk):
    T = x2.shape[0]
    spb = seq // tm
    nt = T // tm
    prev = lambda s: jnp.maximum(s - 1, 0)
    cur = lambda s: jnp.minimum(s, nt - 1)
    row_prev = lambda n: pl.BlockSpec((tm, n), lambda s: (prev(s), 0))
    row_cur = lambda n: pl.BlockSpec((tm, n), lambda s: (cur(s), 0))
    strided = lambda d, n: pl.BlockSpec((None, d, tm // d, n), lambda s: (cur(s) // spb, 0, cur(s) % spb, 0))
    const = lambda a: pl.BlockSpec(a.shape, lambda s: (0,) * a.ndim, pipeline_mode=pl.Buffered(1))
    W = DSA_WIDTH
    return pl.pallas_call(
        functools.partial(_out_kernel, ff_chunk=ff_chunk),
        grid=(nt + 1,),
        in_specs=[row_prev(D_MODEL), row_prev(GLA_WIDTH), row_cur(W), strided(4, W), strided(16, W),
                  row_cur(LANES), strided(4, LANES), strided(16, LANES),
                  const(wo), const(g2), const(w1), const(w2), const(gf)],
        out_specs=row_prev(D_MODEL),
        out_shape=jax.ShapeDtypeStruct((T, D_MODEL), F32),
        scratch_shapes=[pltpu.VMEM((2 * (W // LANES), tm, LANES), F32),
                        pltpu.VMEM((2, tm, LANES), F32),
                        pltpu.VMEM((tm, W), BF16)],
        compiler_params=pltpu.CompilerParams(dimension_semantics=("arbitrary",),
                                             vmem_limit_bytes=VMEM_LIMIT),
        name="out_mlp",
    )(x2, oa, *os_, *ls, wo, g2, w1, w2, gf)


def kernel(x, attn_norm_g, w_in, gla_gate_w2, gla_gate_b, gla_norm_g, rel_bias, w_out, mlp_norm_g,
           w_ff1, w_ff2, final_norm_g):
    batch, seq, _ = x.shape
    assert seq % (DSA_PATTERN[-1][1] * DSA_BLOCK) == 0
    T = batch * seq
    x2 = x.reshape(T, D_MODEL)

    w = w_in[0]
    splits = np.cumsum([GLA_QK, GLA_QK, GLA_WIDTH, GLA_WIDTH, GLA_RANK, DSA_WIDTH, DSA_WIDTH, DSA_WIDTH])[:-1]
    wq, wk, wv, wr, wg, wdq, wdk, wdv = jnp.split(w, [int(s) for s in splits], axis=1)
    wq = (wq * GLA_DK ** -0.5).astype(BF16)
    wk = wk.astype(BF16)
    wv = wv.astype(BF16)
    wr = wr.astype(BF16)
    wg = jnp.pad(wg, ((0, 0), (0, RANK_PAD - GLA_RANK))).astype(BF16)
    w2 = jnp.pad(gla_gate_w2[0], ((0, RANK_PAD - GLA_RANK), (0, 0))).astype(BF16)
    gb = gla_gate_b[0].astype(F32).reshape(1, GLA_QK)
    wd = jnp.concatenate([wdq * (DSA_DH ** -0.5 * LOG2E), wdk, wdv], axis=1).astype(BF16)

    gq, gv, gr, d1, kt, glog, glogt, d4, d16, kt1, kt4 = _proj(
        x2, attn_norm_g[0].reshape(1, D_MODEL).astype(F32), wq, wv, wr, wk, wg, w2, gb, wd,
        batch=batch, seq=seq, tm=512)

    o_a = _gla(gq, kt, gv, gr, glog, glogt, gla_norm_g[0].reshape(1, GLA_WIDTH).astype(F32),
               batch=batch, seq=seq, tg=2048)

    bias = _bias_tables(rel_bias)
    os_, ls = [], []
    branches = ((d1.reshape(batch, 1, seq, 2 * DSA_WIDTH), kt1), (d4, kt4), (d16, None))
    for branch, (xd, ktd) in enumerate(branches):
        o, lse = _dsa_branch(xd, ktd, bias, branch, rows_per_step=4096)
        os_.append(o)
        ls.append(lse)
    os_[0] = os_[0].reshape(T, DSA_WIDTH)
    ls[0] = ls[0].reshape(T, LANES)

    y = _out(x2, o_a, os_, ls, w_out[0].astype(BF16), mlp_norm_g[0].reshape(1, D_MODEL).astype(F32),
             w_ff1[0].astype(BF16), w_ff2[0].astype(BF16), final_norm_g.reshape(1, D_MODEL).astype(F32),
             seq=seq, tm=512, ff_chunk=1024)
    return y.reshape(batch, seq, D_MODEL)
```

```python
import functools
import math

import numpy as np
import jax
import jax.numpy as jnp
from jax import lax
from jax.experimental import pallas as pl
from jax.experimental.pallas import tpu as pltpu

D_MODEL = 1024
GLA_WIDTH = 512
GLA_HEADS = 4
GLA_DK = 64
GLA_DV = 128
GLA_QK = GLA_HEADS * GLA_DK
GLA_RANK = 16
GLA_TAU = 16.0
GLA_CHUNK = 64
DSA_WIDTH = 512
DSA_HEADS = 8
DSA_DH = 64
DSA_PATTERN = ((128, 1), (512, 4), (2048, 16))
DSA_BLOCK = 128
REL_BUCKETS = 32
REL_MAX_DIST = 2048
D_FF = 4096
EPS = 1e-6
NEG = -1e30
LOG2E = math.log2(math.e)

LANES = 128
RANK_PAD = LANES
VMEM_LIMIT = 56 * 1024 * 1024

F32 = jnp.float32
BF16 = jnp.bfloat16

_NT = (((1,), (1,)), ((), ()))


def _dot(a, b):
    return jnp.dot(a, b, preferred_element_type=F32)


def _dot_nt(a, b):
    return lax.dot_general(a, b, _NT, preferred_element_type=F32)


def _split_hi_lo(x):
    hi = x.astype(BF16)
    lo = (x - hi.astype(F32)).astype(BF16)
    return hi, lo


def _log_sigmoid(x):
    return jnp.minimum(x, 0.0) - jnp.log1p(jnp.exp(-jnp.abs(x)))


def _bucket_tables():
    max_exact = REL_BUCKETS // 2
    L = DSA_BLOCK
    steps = L + np.arange(L)[:, None] - np.arange(2 * L)[None, :]
    tables = []
    for window, dilation in DSA_PATTERN:
        span = window // dilation
        in_band = (steps >= 0) & (steps <= span)
        n = np.maximum(steps * dilation, 0)
        large = max_exact + (np.log(np.maximum(n, 1) / max_exact)
                             / math.log(REL_MAX_DIST / max_exact)
                             * (REL_BUCKETS - max_exact)).astype(np.int32)
        large = np.minimum(large, REL_BUCKETS - 1)
        bucket = np.where(n < max_exact, n, large).astype(np.int32)
        tables.append(np.where(in_band, bucket, -1).astype(np.int32))
    return np.stack(tables)


def _bias_kernel(rb_ref, bt_ref, out_ref):
    bt = bt_ref[0]

    def one_head(h, carry):
        acc = jnp.full(bt.shape, NEG, F32)
        for b in range(REL_BUCKETS):
            acc = jnp.where(bt == b, rb_ref[b, h] * LOG2E, acc)
        out_ref[0, h] = acc
        return carry

    lax.fori_loop(0, DSA_HEADS, one_head, 0)


def _bias_tables(rel_bias):
    nb = len(DSA_PATTERN)
    L = DSA_BLOCK
    return pl.pallas_call(
        _bias_kernel,
        grid=(nb,),
        in_specs=[pl.BlockSpec(memory_space=pltpu.SMEM),
                  pl.BlockSpec((1, L, 2 * L), lambda d: (d, 0, 0))],
        out_specs=pl.BlockSpec((1, DSA_HEADS, L, 2 * L), lambda d: (d, 0, 0, 0)),
        out_shape=jax.ShapeDtypeStruct((nb, DSA_HEADS, L, 2 * L), F32),
        name="dsa_bias",
    )(rel_bias.astype(F32), jnp.asarray(_bucket_tables()))


def _proj_kernel(x_ref, g_ref, wq_ref, wv_ref, wr_ref, wk_ref, wg_ref, w2_ref, b_ref, wd_ref,
                 gq_ref, gv_ref, gr_ref, d1_ref, kt_ref, glog_ref, glogt_ref, d4_ref, d16_ref,
                 kt1_ref, kt4_ref, dsc_ref, t4_ref):
    tm = x_ref.shape[0]
    ncol = 3 * DSA_WIDTH // LANES
    kcols = range(DSA_WIDTH // LANES, 2 * DSA_WIDTH // LANES)
    n4 = tm // 4
    n16 = tm // 16

    x = x_ref[...]
    ms = jnp.mean(x * x, axis=-1, keepdims=True)
    nx = (x * lax.rsqrt(ms + EPS) * g_ref[...]).astype(BF16)

    kt_ref[0] = _dot(nx, wk_ref[...]).T.astype(BF16)
    glow = _dot(nx, wg_ref[...]).astype(BF16)
    glog = _log_sigmoid(_dot(glow, w2_ref[...]) + b_ref[...]) * (LOG2E / GLA_TAU)
    glog_ref[...] = glog
    glogt_ref[0] = glog.T

    for j in range(ncol // 2):
        res = _dot(nx, wd_ref[:, 2 * j * LANES:(2 * j + 2) * LANES])
        halves = {2 * j: res[:, :LANES], 2 * j + 1: res[:, LANES:]}
        for c in halves:
            dsc_ref[c % 2] = halves[c]
        for c in halves:
            sl = c % 2
            cols = slice(c * LANES, (c + 1) * LANES)
            is_key = c in kcols
            krows = slice((c - kcols[0]) * LANES, (c - kcols[0] + 1) * LANES)
            qv = c if c < kcols[0] else c - len(kcols)
            qv_cols = slice(qv * LANES, (qv + 1) * LANES)
            if is_key:
                for jb in range(tm // DSA_BLOCK):
                    blk_rows = slice(jb * DSA_BLOCK, (jb + 1) * DSA_BLOCK)
                    kt1_ref[jb, krows, :] = dsc_ref[sl, blk_rows, :].T.astype(BF16)
            else:
                d1_ref[:, qv_cols] = halves[c].astype(BF16)
            for r4 in range(4):
                sub = dsc_ref[sl, pl.ds(r4, n4, stride=4), :]
                t4_ref[sl, r4 * n4:(r4 + 1) * n4, :] = sub
                if is_key:
                    for jb in range(n4 // DSA_BLOCK):
                        blk_rows = slice(jb * DSA_BLOCK, (jb + 1) * DSA_BLOCK)
                        kt4_ref[jb, r4, krows, :] = sub[blk_rows, :].T.astype(BF16)
                else:
                    d4_ref[r4, :, qv_cols] = sub.astype(BF16)
            for r4 in range(4):
                for r2 in range(4):
                    d16_ref[r4 + 4 * r2, :, cols] = (
                        t4_ref[sl, pl.ds(r4 * n4 + r2, n16, stride=4), :].astype(BF16))

    gq_ref[...] = _dot(nx, wq_ref[...]).astype(BF16)
    gv_ref[...] = _dot(nx, wv_ref[...]).astype(BF16)
    gr_ref[...] = _dot(nx, wr_ref[...]).astype(BF16)


def _proj(x2, g, wq, wv, wr, wk, wg, w2, b, wd, *, batch, seq, tm):
    T = batch * seq
    spb = seq // tm
    nt = T // tm
    full = lambda a: pl.BlockSpec(a.shape, lambda s: (0,) * a.ndim, pipeline_mode=pl.Buffered(1))
    row_cur = lambda n: pl.BlockSpec((tm, n), lambda s: (s, 0))
    row_prev = row_cur
    colt = pl.BlockSpec((1, GLA_QK, tm), lambda s: (s // spb, 0, s % spb))
    W, L = DSA_WIDTH, DSA_BLOCK
    W2, W3 = 2 * W, 3 * W
    strided = lambda d, n: pl.BlockSpec((None, d, tm // d, n), lambda s: (s // spb, 0, s % spb, 0))
    assert tm % (4 * L) == 0
    kt1_spec = pl.BlockSpec((None, tm // L, None, W, L), lambda s: (s // spb, s % spb, 0, 0, 0))
    kt4_spec = pl.BlockSpec((None, tm // (4 * L), 4, W, L), lambda s: (s // spb, s % spb, 0, 0, 0))
    return pl.pallas_call(
        _proj_kernel,
        grid=(nt,),
        in_specs=[row_cur(D_MODEL)] + [full(a) for a in (g, wq, wv, wr, wk, wg, w2, b, wd)],
        out_specs=[row_cur(GLA_QK), row_cur(GLA_WIDTH), row_cur(GLA_WIDTH), row_cur(W2),
                   colt, row_prev(GLA_QK), colt, strided(4, W2), strided(16, W3), kt1_spec, kt4_spec],
        out_shape=[jax.ShapeDtypeStruct((T, GLA_QK), BF16),
                   jax.ShapeDtypeStruct((T, GLA_WIDTH), BF16),
                   jax.ShapeDtypeStruct((T, GLA_WIDTH), BF16),
                   jax.ShapeDtypeStruct((T, W2), BF16),
                   jax.ShapeDtypeStruct((batch, GLA_QK, seq), BF16),
                   jax.ShapeDtypeStruct((T, GLA_QK), F32),
                   jax.ShapeDtypeStruct((batch, GLA_QK, seq), F32),
                   jax.ShapeDtypeStruct((batch, 4, seq // 4, W2), BF16),
                   jax.ShapeDtypeStruct((batch, 16, seq // 16, W3), BF16),
                   jax.ShapeDtypeStruct((batch, seq // L, 1, W, L), BF16),
                   jax.ShapeDtypeStruct((batch, seq // (4 * L), 4, W, L), BF16)],
        scratch_shapes=[pltpu.VMEM((2, tm, LANES), F32)] * 2,
        compiler_params=pltpu.CompilerParams(dimension_semantics=("arbitrary",),
                                             vmem_limit_bytes=VMEM_LIMIT),
        name="proj",
    )(x2, g, wq, wv, wr, wk, wg, w2, b, wd)


def _gla_kernel(gq_ref, kt_ref, gv_ref, gr_ref, glog_ref, glogt_ref, gn_ref, o_ref, s_ref, *, pairs):
    C = GLA_CHUNK
    P = 2 * C

    @pl.when(pl.program_id(1) == 0)
    def _():
        s_ref[...] = jnp.zeros_like(s_ref)

    ri = lax.broadcasted_iota(jnp.int32, (P, P), 0)
    ci = lax.broadcasted_iota(jnp.int32, (P, P), 1)
    same_chunk = (ri < C) == (ci < C)
    causal = same_chunk & (ci <= ri)
    low = jnp.where(causal, 1.0, 0.0).astype(BF16)
    upp = jnp.where(same_chunk & (ri <= ci), 1.0, 0.0).astype(BF16)
    lane = lax.broadcasted_iota(jnp.int32, (P, LANES), 1)
    first_half = lane < C
    lane_t = lax.broadcasted_iota(jnp.int32, (GLA_QK, P), 1)
    first_t = lane_t < C

    zero16 = jnp.zeros((), BF16)
    state = [s_ref[h * GLA_DK:(h + 1) * GLA_DK, :] for h in range(GLA_HEADS)]

    heads = range(GLA_HEADS)
    hk = [slice(h * GLA_DK, (h + 1) * GLA_DK) for h in heads]
    hv = [slice(h * GLA_DV, (h + 1) * GLA_DV) for h in heads]
    grp = [slice((h // 2) * LANES, (h // 2 + 1) * LANES) for h in heads]

    def decay_stage(p):
        rows = slice(p * P, (p + 1) * P)
        g_hi, g_lo = _split_hi_lo(glog_ref[rows, :])
        b = _dot(low, g_hi) + _dot(low, g_lo)
        gt = glogt_ref[0, :, rows]
        gt_hi, gt_lo = _split_hi_lo(gt)
        bt = _dot(gt_hi, upp) + _dot(gt_lo, upp)
        tot_a = jnp.sum(jnp.where(first_t, gt, 0.0), axis=-1, keepdims=True)
        tot_b = jnp.sum(jnp.where(first_t, 0.0, gt), axis=-1, keepdims=True)
        return dict(rows=rows, b=b, bt=bt, dec_a=jnp.exp2(tot_a), dec_b=jnp.exp2(tot_b))

    def score_stage(c):
        rows = c["rows"]
        qd = (gq_ref[rows, :].astype(F32) * jnp.exp2(c["b"])).astype(BF16)
        kinv_t = (kt_ref[0, :, rows].astype(F32) * jnp.exp2(-c["bt"])).astype(BF16)
        kinv_a = jnp.where(first_t, kinv_t, zero16)
        kinv_b = jnp.where(first_t, zero16, kinv_t)
        c["v"] = [gv_ref[rows, hv[h]] for h in heads]
        c["qm"] = [jnp.where(first_half if h % 2 == 0 else ~first_half, qd[:, grp[h]], zero16) for h in heads]
        c["att"] = [_dot(c["qm"][h], kinv_t[grp[h], :]) for h in heads]
        c["upd_a"] = [_dot(kinv_a[hk[h], :], c["v"][h]) for h in heads]
        c["upd_b"] = [_dot(kinv_b[hk[h], :], c["v"][h]) for h in heads]

    def output_stage(c):
        o = []
        s_a = list(state)
        s_b = [c["dec_a"][hk[h], :] * (s_a[h] + c["upd_a"][h]) for h in heads]
        for h in heads:
            state[h] = c["dec_b"][hk[h], :] * (s_b[h] + c["upd_b"][h])
        for h in heads:
            att = jnp.where(causal, c["att"][h], 0.0).astype(BF16)
            pair_a = [s_a[h], s_a[h ^ 1]] if h % 2 == 0 else [s_a[h ^ 1], s_a[h]]
            pair_b = [s_b[h], s_a[h ^ 1]] if h % 2 == 0 else [s_a[h ^ 1], s_b[h]]
            o_a = _dot(c["qm"][h][:C, :], jnp.concatenate(pair_a, axis=0).astype(BF16))
            o_b = _dot(c["qm"][h][C:, :], jnp.concatenate(pair_b, axis=0).astype(BF16))
            o.append(_dot(att, c["v"][h]) + jnp.concatenate([o_a, o_b], axis=0))
        c["o"] = o

    def norm_stage(c):
        rows = c["rows"]
        for h in heads:
            o = c["o"][h]
            o = o * lax.rsqrt(jnp.mean(o * o, axis=-1, keepdims=True) + EPS) * gn_ref[:, hv[h]]
            r = gr_ref[rows, hv[h]].astype(F32)
            o_ref[rows, hv[h]] = (o * (r * jax.nn.sigmoid(r))).astype(BF16)

    ctx = {}
    for t in range(pairs + 3):
        if t < pairs:
            ctx[t] = decay_stage(t)
        if 0 <= t - 1 < pairs:
            score_stage(ctx[t - 1])
        if 0 <= t - 2 < pairs:
            output_stage(ctx[t - 2])
        if 0 <= t - 3 < pairs:
            norm_stage(ctx.pop(t - 3))

    for h in heads:
        s_ref[hk[h], :] = state[h]


def _gla(gq, kt, gv, gr, glog, glogt, gn, *, batch, seq, tg):
    T = batch * seq
    spb = seq // tg
    row = lambda n: pl.BlockSpec((tg, n), lambda b, i: (b * spb + i, 0))
    colt = pl.BlockSpec((1, GLA_QK, tg), lambda b, i: (b, 0, i))
    return pl.pallas_call(
        functools.partial(_gla_kernel, pairs=tg // (2 * GLA_CHUNK)),
        grid=(batch, spb),
        in_specs=[row(GLA_QK), colt, row(GLA_WIDTH), row(GLA_WIDTH), row(GLA_QK), colt,
                  pl.BlockSpec((1, GLA_WIDTH), lambda b, i: (0, 0))],
        out_specs=row(GLA_WIDTH),
        out_shape=jax.ShapeDtypeStruct((T, GLA_WIDTH), BF16),
        scratch_shapes=[pltpu.VMEM((GLA_QK, GLA_DV), F32)],
        compiler_params=pltpu.CompilerParams(dimension_semantics=("arbitrary", "arbitrary"),
                                             vmem_limit_bytes=VMEM_LIMIT),
        name="gla",
    )(gq, kt, gv, gr, glog, glogt, gn)


def _dsa_kernel(q_ref, k_ref, v_ref, kh_ref, vh_ref, bias_ref, o_ref, st_ref, *, res, blocks, keys_transposed):
    L = DSA_BLOCK
    first_tile = pl.program_id(2) == 0
    lane = lax.broadcasted_iota(jnp.int32, (L, LANES), 1)
    first_half = lane < DSA_DH
    prev_cols = lax.broadcasted_iota(jnp.int32, (L, 2 * L), 1) < L

    st_tiles = {}
    pair_out = {}

    def score_stage(r, blk, h):
        rows = slice(blk * L, (blk + 1) * L)
        both = slice((blk - 1) * L, (blk + 1) * L)
        grp = slice((h // 2) * LANES, (h // 2 + 1) * LANES)
        if keys_transposed:
            k_prev = kh_ref[r, grp, :] if blk == 0 else k_ref[blk - 1, r, grp, :]
            kcat = jnp.concatenate([k_prev, k_ref[blk, r, grp, :]], axis=1)
        elif blk == 0:
            kcat = jnp.concatenate([kh_ref[r, :, grp], k_ref[r, rows, grp]], axis=0)
        else:
            kcat = k_ref[r, both, grp]
        own = first_half if h % 2 == 0 else ~first_half
        qm = jnp.where(own, q_ref[r, rows, grp], jnp.zeros((), BF16))
        return _dot(qm, kcat) if keys_transposed else _dot_nt(qm, kcat)

    def value_stage(r, blk, h, s):
        rows = slice(blk * L, (blk + 1) * L)
        grp = slice((h // 2) * LANES, (h // 2 + 1) * LANES)
        bias = bias_ref[h]
        if blk == 0:
            bias = jnp.where(prev_cols & first_tile, NEG, bias)
        s = s + bias
        m = jnp.max(s, axis=-1, keepdims=True)
        p = jnp.exp2(s - m)
        den = jnp.sum(p, axis=-1, keepdims=True)
        st_old = st_tiles.get((r, blk), jnp.zeros((L, LANES), F32))
        st_tiles[r, blk] = jnp.where(lane == h, m, jnp.where(lane == DSA_HEADS + h, den, st_old))
        if blk == 0:
            vcat = jnp.concatenate([vh_ref[r, :, grp], v_ref[r, rows, grp]], axis=0)
        else:
            vcat = v_ref[r, both_rows(blk), grp]
        out = _dot(p.astype(BF16), vcat)
        if h % 2 == 0:
            pair_out[r, blk] = out
        else:
            o_ref[r, rows, grp] = jnp.where(first_half, pair_out.pop((r, blk)), out).astype(BF16)
        if h == DSA_HEADS - 1:
            st_ref[r, rows, :] = st_tiles.pop((r, blk))

    def both_rows(blk):
        return slice((blk - 1) * L, (blk + 1) * L)

    items = [(r, blk, h) for r in range(res) for blk in range(blocks) for h in range(DSA_HEADS)]
    lag = 1
    pending = {}
    for t in range(len(items) + lag):
        if t < len(items):
            pending[t] = score_stage(*items[t])
        if t - lag >= 0:
            value_stage(*items[t - lag], pending.pop(t - lag))


def _dsa_branch(x, kt, bias, branch, *, rows_per_step):
    batch, d, n, _ = x.shape
    tr = min(rows_per_step, n)
    res = min(rows_per_step // tr, d)
    hb = tr // DSA_BLOCK
    W = DSA_WIDTH
    prev_blk = lambda i: jnp.maximum(i * hb - 1, 0)
    blk = lambda part: pl.BlockSpec((None, res, tr, W), lambda b, r, i: (b, r, i, part))
    halo = lambda part: pl.BlockSpec((None, res, DSA_BLOCK, W), lambda b, r, i: (b, r, prev_blk(i), part))
    if kt is None:
        v_part = 2
        k_arg, k_spec, kh_spec = x, blk(1), halo(1)
    else:
        v_part = 1
        k_arg = kt
        k_spec = pl.BlockSpec((None, hb, res, W, DSA_BLOCK), lambda b, r, i: (b, i, r, 0, 0))
        kh_spec = pl.BlockSpec((None, None, res, W, DSA_BLOCK), lambda b, r, i: (b, prev_blk(i), r, 0, 0))
    return pl.pallas_call(
        functools.partial(_dsa_kernel, res=res, blocks=hb, keys_transposed=kt is not None),
        grid=(batch, d // res, n // tr),
        in_specs=[blk(0), k_spec, blk(v_part), kh_spec, halo(v_part),
                  pl.BlockSpec((None, DSA_HEADS, DSA_BLOCK, 2 * DSA_BLOCK), lambda b, r, i: (branch, 0, 0, 0))],
        out_specs=[pl.BlockSpec((None, res, tr, W), lambda b, r, i: (b, r, i, 0)),
                   pl.BlockSpec((None, res, tr, LANES), lambda b, r, i: (b, r, i, 0))],
        out_shape=[jax.ShapeDtypeStruct((batch, d, n, W), BF16),
                   jax.ShapeDtypeStruct((batch, d, n, LANES), F32)],
        compiler_params=pltpu.CompilerParams(dimension_semantics=("arbitrary",) * 3,
                                             vmem_limit_bytes=VMEM_LIMIT),
        name=f"dsa_d{d}",
    )(x, k_arg, x, k_arg, x, bias)


def _out_kernel(x_ref, oa_ref, o1_ref, o4_ref, o16_ref, l1_ref, l4_ref, l16_ref,
                wo_ref, g2_ref, w1_ref, w2_ref, gf_ref, y_ref, osc_ref, lsc_ref, ob_ref, *, ff_chunk):
    tm = x_ref.shape[0]
    npair = DSA_HEADS // 2
    lane = lax.broadcasted_iota(jnp.int32, (tm, LANES), 1)
    first_half = lane < DSA_DH

    @pl.when(pl.program_id(0) == 0)
    def _():
        ob_ref[...] = jnp.zeros_like(ob_ref)

    mixed = _dot(oa_ref[...], wo_ref[:GLA_WIDTH, :]) + _dot(ob_ref[...], wo_ref[GLA_WIDTH:, :])
    h = x_ref[...] + mixed
    nm = (h * lax.rsqrt(jnp.mean(h * h, axis=-1, keepdims=True) + EPS) * g2_ref[...]).astype(BF16)
    ff = None
    for c in range(D_FF // ff_chunk):
        cols = slice(c * ff_chunk, (c + 1) * ff_chunk)
        a = jnp.maximum(_dot(nm, w1_ref[:, cols]), 0.0)
        d = _dot((a * a).astype(BF16), w2_ref[cols, :])
        ff = d if ff is None else ff + d
    h = h + ff
    y_ref[...] = h * lax.rsqrt(jnp.mean(h * h, axis=-1, keepdims=True) + EPS) * gf_ref[...]

    for j, (d, o_ref, l_ref) in enumerate(((4, o4_ref, l4_ref), (16, o16_ref, l16_ref))):
        for r in range(d):
            lsc_ref[j, pl.ds(r, tm // d, stride=d), :] = l_ref[r]
            for hp in range(npair):
                osc_ref[j * npair + hp, pl.ds(r, tm // d, stride=d), :] = (
                    o_ref[r, :, hp * LANES:(hp + 1) * LANES].astype(F32))

    sts = (l1_ref[...], lsc_ref[0], lsc_ref[1])
    m = jnp.maximum(jnp.maximum(sts[0], sts[1]), sts[2])
    es = [jnp.exp2(st - m) for st in sts]
    total = sum(e * pltpu.roll(st, LANES - DSA_HEADS, axis=1) for e, st in zip(es, sts))
    inv = 1.0 / jnp.where(lane < DSA_HEADS, total, 1.0)
    ws = [e * inv for e in es]

    for hp in range(npair):
        grp = slice(hp * LANES, (hp + 1) * LANES)
        branch_o = (o1_ref[:, grp].astype(F32), osc_ref[hp], osc_ref[npair + hp])
        acc = jnp.zeros((tm, LANES), F32)
        for w, o in zip(ws, branch_o):
            wa = jnp.sum(jnp.where(lane == 2 * hp, w, 0.0), axis=-1, keepdims=True)
            wb = jnp.sum(jnp.where(lane == 2 * hp + 1, w, 0.0), axis=-1, keepdims=True)
            acc = acc + jnp.where(first_half, wa, wb) * o
        ob_ref[:, grp] = acc.astype(BF16)


def _out(x2, oa, os_, ls, wo, g2, w1, w2, gf, *, seq, tm, ff_chunk):
    T = x2.shape[0]
    spb = seq // tm
    nt = T // tm
    prev = lambda s: jnp.maximum(s - 1, 0)
    cur = lambda s: jnp.minimum(s, nt - 1)
    row_prev = lambda n: pl.BlockSpec((tm, n), lambda s: (prev(s), 0))
    row_cur = lambda n: pl.BlockSpec((tm, n), lambda s: (cur(s), 0))
    strided = lambda d, n: pl.BlockSpec((None, d, tm // d, n), lambda s: (cur(s) // spb, 0, cur(s) % spb, 0))
    const = lambda a: pl.BlockSpec(a.shape, lambda s: (0,) * a.ndim, pipeline_mode=pl.Buffered(1))
    W = DSA_WIDTH
    return pl.pallas_call(
        functools.partial(_out_kernel, ff_chunk=ff_chunk),
        grid=(nt + 1,),
        in_specs=[row_prev(D_MODEL), row_prev(GLA_WIDTH), row_cur(W), strided(4, W), strided(16, W),
                  row_cur(LANES), strided(4, LANES), strided(16, LANES),
                  const(wo), const(g2), const(w1), const(w2), const(gf)],
        out_specs=row_prev(D_MODEL),
        out_shape=jax.ShapeDtypeStruct((T, D_MODEL), F32),
        scratch_shapes=[pltpu.VMEM((2 * (W // LANES), tm, LANES), F32),
                        pltpu.VMEM((2, tm, LANES), F32),
                        pltpu.VMEM((tm, W), BF16)],
        compiler_params=pltpu.CompilerParams(dimension_semantics=("arbitrary",),
                                             vmem_limit_bytes=VMEM_LIMIT),
        name="out_mlp",
    )(x2, oa, *os_, *ls, wo, g2, w1, w2, gf)


def kernel(x, attn_norm_g, w_in, gla_gate_w2, gla_gate_b, gla_norm_g, rel_bias, w_out, mlp_norm_g,
           w_ff1, w_ff2, final_norm_g):
    batch, seq, _ = x.shape
    assert seq % (DSA_PATTERN[-1][1] * DSA_BLOCK) == 0
    T = batch * seq
    x2 = x.reshape(T, D_MODEL)

    w = w_in[0]
    splits = np.cumsum([GLA_QK, GLA_QK, GLA_WIDTH, GLA_WIDTH, GLA_RANK, DSA_WIDTH, DSA_WIDTH, DSA_WIDTH])[:-1]
    wq, wk, wv, wr, wg, wdq, wdk, wdv = jnp.split(w, [int(s) for s in splits], axis=1)
    wq = (wq * GLA_DK ** -0.5).astype(BF16)
    wk = wk.astype(BF16)
    wv = wv.astype(BF16)
    wr = wr.astype(BF16)
    wg = jnp.pad(wg, ((0, 0), (0, RANK_PAD - GLA_RANK))).astype(BF16)
    w2 = jnp.pad(gla_gate_w2[0], ((0, RANK_PAD - GLA_RANK), (0, 0))).astype(BF16)
    gb = gla_gate_b[0].astype(F32).reshape(1, GLA_QK)
    wd = jnp.concatenate([wdq * (DSA_DH ** -0.5 * LOG2E), wdk, wdv], axis=1).astype(BF16)

    gq, gv, gr, d1, kt, glog, glogt, d4, d16, kt1, kt4 = _proj(
        x2, attn_norm_g[0].reshape(1, D_MODEL).astype(F32), wq, wv, wr, wk, wg, w2, gb, wd,
        batch=batch, seq=seq, tm=1024)

    o_a = _gla(gq, kt, gv, gr, glog, glogt, gla_norm_g[0].reshape(1, GLA_WIDTH).astype(F32),
               batch=batch, seq=seq, tg=2048)

    bias = _bias_tables(rel_bias)
    os_, ls = [], []
    branches = ((d1.reshape(batch, 1, seq, 2 * DSA_WIDTH), kt1), (d4, kt4), (d16, None))
    for branch, (xd, ktd) in enumerate(branches):
        o, lse = _dsa_branch(xd, ktd, bias, branch, rows_per_step=2048)
        os_.append(o)
        ls.append(lse)
    os_[0] = os_[0].reshape(T, DSA_WIDTH)
    ls[0] = ls[0].reshape(T, LANES)

    y = _out(x2, o_a, os_, ls, w_out[0].astype(BF16), mlp_norm_g[0].reshape(1, D_MODEL).astype(F32),
             w_ff1[0].astype(BF16), w_ff2[0].astype(BF16), final_norm_g.reshape(1, D_MODEL).astype(F32),
             seq=seq, tm=512, ff_chunk=1024)
    return y.reshape(batch, seq, D_MODEL)
```

```python
import functools
import math

import numpy as np
import jax
import jax.numpy as jnp
from jax import lax
from jax.experimental import pallas as pl
from jax.experimental.pallas import tpu as pltpu

D_MODEL = 1024
GLA_WIDTH = 512
GLA_HEADS = 4
GLA_DK = 64
GLA_DV = 128
GLA_QK = GLA_HEADS * GLA_DK
GLA_RANK = 16
GLA_TAU = 16.0
GLA_CHUNK = 64
DSA_WIDTH = 512
DSA_HEADS = 8
DSA_DH = 64
DSA_PATTERN = ((128, 1), (512, 4), (2048, 16))
DSA_BLOCK = 128
REL_BUCKETS = 32
REL_MAX_DIST = 2048
D_FF = 4096
EPS = 1e-6
NEG = -1e30
LOG2E = math.log2(math.e)

LANES = 128
RANK_PAD = LANES
VMEM_LIMIT = 56 * 1024 * 1024

F32 = jnp.float32
BF16 = jnp.bfloat16

_NT = (((1,), (1,)), ((), ()))


def _dot(a, b):
    return jnp.dot(a, b, preferred_element_type=F32)


def _dot_nt(a, b):
    return lax.dot_general(a, b, _NT, preferred_element_type=F32)


def _split_hi_lo(x):
    hi = x.astype(BF16)
    lo = (x - hi.astype(F32)).astype(BF16)
    return hi, lo


def _log_sigmoid(x):
    return jnp.minimum(x, 0.0) - jnp.log1p(jnp.exp(-jnp.abs(x)))


def _bucket_tables():
    max_exact = REL_BUCKETS // 2
    L = DSA_BLOCK
    steps = L + np.arange(L)[:, None] - np.arange(2 * L)[None, :]
    tables = []
    for window, dilation in DSA_PATTERN:
        span = window // dilation
        in_band = (steps >= 0) & (steps <= span)
        n = np.maximum(steps * dilation, 0)
        large = max_exact + (np.log(np.maximum(n, 1) / max_exact)
                             / math.log(REL_MAX_DIST / max_exact)
                             * (REL_BUCKETS - max_exact)).astype(np.int32)
        large = np.minimum(large, REL_BUCKETS - 1)
        bucket = np.where(n < max_exact, n, large).astype(np.int32)
        tables.append(np.where(in_band, bucket, -1).astype(np.int32))
    return np.stack(tables)


def _bias_kernel(rb_ref, bt_ref, out_ref):
    bt = bt_ref[0]

    def one_head(h, carry):
        acc = jnp.full(bt.shape, NEG, F32)
        for b in range(REL_BUCKETS):
            acc = jnp.where(bt == b, rb_ref[b, h] * LOG2E, acc)
        out_ref[0, h] = acc
        return carry

    lax.fori_loop(0, DSA_HEADS, one_head, 0)


def _bias_tables(rel_bias):
    nb = len(DSA_PATTERN)
    L = DSA_BLOCK
    return pl.pallas_call(
        _bias_kernel,
        grid=(nb,),
        in_specs=[pl.BlockSpec(memory_space=pltpu.SMEM),
                  pl.BlockSpec((1, L, 2 * L), lambda d: (d, 0, 0))],
        out_specs=pl.BlockSpec((1, DSA_HEADS, L, 2 * L), lambda d: (d, 0, 0, 0)),
        out_shape=jax.ShapeDtypeStruct((nb, DSA_HEADS, L, 2 * L), F32),
        name="dsa_bias",
    )(rel_bias.astype(F32), jnp.asarray(_bucket_tables()))


def _proj_kernel(x_ref, g_ref, wq_ref, wv_ref, wr_ref, wk_ref, wg_ref, w2_ref, b_ref, wd_ref,
                 gq_ref, gv_ref, gr_ref, d1_ref, kt_ref, glog_ref, glogt_ref, d4_ref, d16_ref,
                 kt1_ref, kt4_ref, dsc_ref, t4_ref):
    tm = x_ref.shape[0]
    ncol = 3 * DSA_WIDTH // LANES
    kcols = range(DSA_WIDTH // LANES, 2 * DSA_WIDTH // LANES)
    n4 = tm // 4
    n16 = tm // 16

    x = x_ref[...]
    ms = jnp.mean(x * x, axis=-1, keepdims=True)
    nx = (x * lax.rsqrt(ms + EPS) * g_ref[...]).astype(BF16)

    kt_ref[0] = _dot(nx, wk_ref[...]).T.astype(BF16)
    glow = _dot(nx, wg_ref[...]).astype(BF16)
    glog = _log_sigmoid(_dot(glow, w2_ref[...]) + b_ref[...]) * (LOG2E / GLA_TAU)
    glog_ref[...] = glog
    glogt_ref[0] = glog.T

    for j in range(ncol // 2):
        res = _dot(nx, wd_ref[:, 2 * j * LANES:(2 * j + 2) * LANES])
        halves = {2 * j: res[:, :LANES], 2 * j + 1: res[:, LANES:]}
        for c in halves:
            dsc_ref[c % 2] = halves[c]
        for c in halves:
            sl = c % 2
            cols = slice(c * LANES, (c + 1) * LANES)
            is_key = c in kcols
            krows = slice((c - kcols[0]) * LANES, (c - kcols[0] + 1) * LANES)
            qv = c if c < kcols[0] else c - len(kcols)
            qv_cols = slice(qv * LANES, (qv + 1) * LANES)
            if is_key:
                for jb in range(tm // DSA_BLOCK):
                    blk_rows = slice(jb * DSA_BLOCK, (jb + 1) * DSA_BLOCK)
                    kt1_ref[jb, krows, :] = dsc_ref[sl, blk_rows, :].T.astype(BF16)
            else:
                d1_ref[:, qv_cols] = halves[c].astype(BF16)
            for r4 in range(4):
                sub = dsc_ref[sl, pl.ds(r4, n4, stride=4), :]
                t4_ref[sl, r4 * n4:(r4 + 1) * n4, :] = sub
                if is_key:
                    for jb in range(n4 // DSA_BLOCK):
                        blk_rows = slice(jb * DSA_BLOCK, (jb + 1) * DSA_BLOCK)
                        kt4_ref[jb, r4, krows, :] = sub[blk_rows, :].T.astype(BF16)
                else:
                    d4_ref[r4, :, qv_cols] = sub.astype(BF16)
            for r4 in range(4):
                for r2 in range(4):
                    d16_ref[r4 + 4 * r2, :, cols] = (
                        t4_ref[sl, pl.ds(r4 * n4 + r2, n16, stride=4), :].astype(BF16))

    gq_ref[...] = _dot(nx, wq_ref[...]).astype(BF16)
    gv_ref[...] = _dot(nx, wv_ref[...]).astype(BF16)
    gr_ref[...] = _dot(nx, wr_ref[...]).astype(BF16)


def _proj(x2, g, wq, wv, wr, wk, wg, w2, b, wd, *, batch, seq, tm):
    T = batch * seq
    spb = seq // tm
    nt = T // tm
    full = lambda a: pl.BlockSpec(a.shape, lambda s: (0,) * a.ndim, pipeline_mode=pl.Buffered(1))
    row_cur = lambda n: pl.BlockSpec((tm, n), lambda s: (s, 0))
    row_prev = row_cur
    colt = pl.BlockSpec((1, GLA_QK, tm), lambda s: (s // spb, 0, s % spb))
    W, L = DSA_WIDTH, DSA_BLOCK
    W2, W3 = 2 * W, 3 * W
    strided = lambda d, n: pl.BlockSpec((None, d, tm // d, n), lambda s: (s // spb, 0, s % spb, 0))
    assert tm % (4 * L) == 0
    kt1_spec = pl.BlockSpec((None, tm // L, None, W, L), lambda s: (s // spb, s % spb, 0, 0, 0))
    kt4_spec = pl.BlockSpec((None, tm // (4 * L), 4, W, L), lambda s: (s // spb, s % spb, 0, 0, 0))
    return pl.pallas_call(
        _proj_kernel,
        grid=(nt,),
        in_specs=[row_cur(D_MODEL)] + [full(a) for a in (g, wq, wv, wr, wk, wg, w2, b, wd)],
        out_specs=[row_cur(GLA_QK), row_cur(GLA_WIDTH), row_cur(GLA_WIDTH), row_cur(W2),
                   colt, row_prev(GLA_QK), colt, strided(4, W2), strided(16, W3), kt1_spec, kt4_spec],
        out_shape=[jax.ShapeDtypeStruct((T, GLA_QK), BF16),
                   jax.ShapeDtypeStruct((T, GLA_WIDTH), BF16),
                   jax.ShapeDtypeStruct((T, GLA_WIDTH), BF16),
                   jax.ShapeDtypeStruct((T, W2), BF16),
                   jax.ShapeDtypeStruct((batch, GLA_QK, seq), BF16),
                   jax.ShapeDtypeStruct((T, GLA_QK), F32),
                   jax.ShapeDtypeStruct((batch, GLA_QK, seq), F32),
                   jax.ShapeDtypeStruct((batch, 4, seq // 4, W2), BF16),
                   jax.ShapeDtypeStruct((batch, 16, seq // 16, W3), BF16),
                   jax.ShapeDtypeStruct((batch, seq // L, 1, W, L), BF16),
                   jax.ShapeDtypeStruct((batch, seq // (4 * L), 4, W, L), BF16)],
        scratch_shapes=[pltpu.VMEM((2, tm, LANES), F32)] * 2,
        compiler_params=pltpu.CompilerParams(dimension_semantics=("arbitrary",),
                                             vmem_limit_bytes=VMEM_LIMIT),
        name="proj",
    )(x2, g, wq, wv, wr, wk, wg, w2, b, wd)


def _gla_kernel(gq_ref, kt_ref, gv_ref, gr_ref, glog_ref, glogt_ref, gn_ref, o_ref, s_ref, *, pairs):
    C = GLA_CHUNK
    P = 2 * C

    @pl.when(pl.program_id(1) == 0)
    def _():
        s_ref[...] = jnp.zeros_like(s_ref)

    ri = lax.broadcasted_iota(jnp.int32, (P, P), 0)
    ci = lax.broadcasted_iota(jnp.int32, (P, P), 1)
    same_chunk = (ri < C) == (ci < C)
    causal = same_chunk & (ci <= ri)
    low = jnp.where(causal, 1.0, 0.0).astype(BF16)
    upp = jnp.where(same_chunk & (ri <= ci), 1.0, 0.0).astype(BF16)
    lane = lax.broadcasted_iota(jnp.int32, (P, LANES), 1)
    first_half = lane < C
    lane_t = lax.broadcasted_iota(jnp.int32, (GLA_QK, P), 1)
    first_t = lane_t < C

    zero16 = jnp.zeros((), BF16)
    state = [s_ref[h * GLA_DK:(h + 1) * GLA_DK, :] for h in range(GLA_HEADS)]

    heads = range(GLA_HEADS)
    hk = [slice(h * GLA_DK, (h + 1) * GLA_DK) for h in heads]
    hv = [slice(h * GLA_DV, (h + 1) * GLA_DV) for h in heads]
    grp = [slice((h // 2) * LANES, (h // 2 + 1) * LANES) for h in heads]

    def decay_stage(p):
        rows = slice(p * P, (p + 1) * P)
        g_hi, g_lo = _split_hi_lo(glog_ref[rows, :])
        b = _dot(low, g_hi) + _dot(low, g_lo)
        gt = glogt_ref[0, :, rows]
        gt_hi, gt_lo = _split_hi_lo(gt)
        bt = _dot(gt_hi, upp) + _dot(gt_lo, upp)
        tot_a = jnp.sum(jnp.where(first_t, gt, 0.0), axis=-1, keepdims=True)
        tot_b = jnp.sum(jnp.where(first_t, 0.0, gt), axis=-1, keepdims=True)
        return dict(rows=rows, b=b, bt=bt, dec_a=jnp.exp2(tot_a), dec_b=jnp.exp2(tot_b))

    def score_stage(c):
        rows = c["rows"]
        qd = (gq_ref[rows, :].astype(F32) * jnp.exp2(c["b"])).astype(BF16)
        kinv_t = (kt_ref[0, :, rows].astype(F32) * jnp.exp2(-c["bt"])).astype(BF16)
        kinv_a = jnp.where(first_t, kinv_t, zero16)
        kinv_b = jnp.where(first_t, zero16, kinv_t)
        c["v"] = [gv_ref[rows, hv[h]] for h in heads]
        c["qm"] = [jnp.where(first_half if h % 2 == 0 else ~first_half, qd[:, grp[h]], zero16) for h in heads]
        c["att"] = [_dot(c["qm"][h], kinv_t[grp[h], :]) for h in heads]
        c["upd_a"] = [_dot(kinv_a[hk[h], :], c["v"][h]) for h in heads]
        c["upd_b"] = [_dot(kinv_b[hk[h], :], c["v"][h]) for h in heads]

    def output_stage(c):
        o = []
        s_a = list(state)
        s_b = [c["dec_a"][hk[h], :] * (s_a[h] + c["upd_a"][h]) for h in heads]
        for h in heads:
            state[h] = c["dec_b"][hk[h], :] * (s_b[h] + c["upd_b"][h])
        for h in heads:
            att = jnp.where(causal, c["att"][h], 0.0).astype(BF16)
            pair_a = [s_a[h], s_a[h ^ 1]] if h % 2 == 0 else [s_a[h ^ 1], s_a[h]]
            pair_b = [s_b[h], s_a[h ^ 1]] if h % 2 == 0 else [s_a[h ^ 1], s_b[h]]
            o_a = _dot(c["qm"][h][:C, :], jnp.concatenate(pair_a, axis=0).astype(BF16))
            o_b = _dot(c["qm"][h][C:, :], jnp.concatenate(pair_b, axis=0).astype(BF16))
            o.append(_dot(att, c["v"][h]) + jnp.concatenate([o_a, o_b], axis=0))
        c["o"] = o

    def norm_stage(c):
        rows = c["rows"]
        for h in heads:
            o = c["o"][h]
            o = o * lax.rsqrt(jnp.mean(o * o, axis=-1, keepdims=True) + EPS) * gn_ref[:, hv[h]]
            r = gr_ref[rows, hv[h]].astype(F32)
            o_ref[rows, hv[h]] = (o * (r * jax.nn.sigmoid(r))).astype(BF16)

    ctx = {}
    for t in range(pairs + 3):
        if t < pairs:
            ctx[t] = decay_stage(t)
        if 0 <= t - 1 < pairs:
            score_stage(ctx[t - 1])
        if 0 <= t - 2 < pairs:
            output_stage(ctx[t - 2])
        if 0 <= t - 3 < pairs:
            norm_stage(ctx.pop(t - 3))

    for h in heads:
        s_ref[hk[h], :] = state[h]


def _gla(gq, kt, gv, gr, glog, glogt, gn, *, batch, seq, tg):
    T = batch * seq
    spb = seq // tg
    row = lambda n: pl.BlockSpec((tg, n), lambda b, i: (b * spb + i, 0))
    colt = pl.BlockSpec((1, GLA_QK, tg), lambda b, i: (b, 0, i))
    return pl.pallas_call(
        functools.partial(_gla_kernel, pairs=tg // (2 * GLA_CHUNK)),
        grid=(batch, spb),
        in_specs=[row(GLA_QK), colt, row(GLA_WIDTH), row(GLA_WIDTH), row(GLA_QK), colt,
                  pl.BlockSpec((1, GLA_WIDTH), lambda b, i: (0, 0))],
        out_specs=row(GLA_WIDTH),
        out_shape=jax.ShapeDtypeStruct((T, GLA_WIDTH), BF16),
        scratch_shapes=[pltpu.VMEM((GLA_QK, GLA_DV), F32)],
        compiler_params=pltpu.CompilerParams(dimension_semantics=("arbitrary", "arbitrary"),
                                             vmem_limit_bytes=VMEM_LIMIT),
        name="gla",
    )(gq, kt, gv, gr, glog, glogt, gn)


def _dsa_kernel(q_ref, k_ref, v_ref, kh_ref, vh_ref, bias_ref, o_ref, st_ref, *, res, blocks, keys_transposed):
    L = DSA_BLOCK
    first_tile = pl.program_id(2) == 0
    lane = lax.broadcasted_iota(jnp.int32, (L, LANES), 1)
    first_half = lane < DSA_DH
    prev_cols = lax.broadcasted_iota(jnp.int32, (L, 2 * L), 1) < L

    st_tiles = {}
    pair_out = {}

    def score_stage(r, blk, h):
        rows = slice(blk * L, (blk + 1) * L)
        both = slice((blk - 1) * L, (blk + 1) * L)
        grp = slice((h // 2) * LANES, (h // 2 + 1) * LANES)
        if keys_transposed:
            k_prev = kh_ref[r, grp, :] if blk == 0 else k_ref[blk - 1, r, grp, :]
            kcat = jnp.concatenate([k_prev, k_ref[blk, r, grp, :]], axis=1)
        elif blk == 0:
            kcat = jnp.concatenate([kh_ref[r, :, grp], k_ref[r, rows, grp]], axis=0)
        else:
            kcat = k_ref[r, both, grp]
        own = first_half if h % 2 == 0 else ~first_half
        qm = jnp.where(own, q_ref[r, rows, grp], jnp.zeros((), BF16))
        return _dot(qm, kcat) if keys_transposed else _dot_nt(qm, kcat)

    def value_stage(r, blk, h, s):
        rows = slice(blk * L, (blk + 1) * L)
        grp = slice((h // 2) * LANES, (h // 2 + 1) * LANES)
        bias = bias_ref[h]
        if blk == 0:
            bias = jnp.where(prev_cols & first_tile, NEG, bias)
        s = s + bias
        m = jnp.max(s, axis=-1, keepdims=True)
        p = jnp.exp2(s - m)
        den = jnp.sum(p, axis=-1, keepdims=True)
        st_old = st_tiles.get((r, blk), jnp.zeros((L, LANES), F32))
        st_tiles[r, blk] = jnp.where(lane == h, m, jnp.where(lane == DSA_HEADS + h, den, st_old))
        if blk == 0:
            vcat = jnp.concatenate([vh_ref[r, :, grp], v_ref[r, rows, grp]], axis=0)
        else:
            vcat = v_ref[r, both_rows(blk), grp]
        out = _dot(p.astype(BF16), vcat)
        if h % 2 == 0:
            pair_out[r, blk] = out
        else:
            o_ref[r, rows, grp] = jnp.where(first_half, pair_out.pop((r, blk)), out).astype(BF16)
        if h == DSA_HEADS - 1:
            st_ref[r, rows, :] = st_tiles.pop((r, blk))

    def both_rows(blk):
        return slice((blk - 1) * L, (blk + 1) * L)

    items = [(r, blk, h) for r in range(res) for blk in range(blocks) for h in range(DSA_HEADS)]
    lag = 1
    pending = {}
    for t in range(len(items) + lag):
        if t < len(items):
            pending[t] = score_stage(*items[t])
        if t - lag >= 0:
            value_stage(*items[t - lag], pending.pop(t - lag))


def _dsa_branch(x, kt, bias, branch, *, rows_per_step):
    batch, d, n, _ = x.shape
    tr = min(rows_per_step, n)
    res = min(rows_per_step // tr, d)
    hb = tr // DSA_BLOCK
    W = DSA_WIDTH
    prev_blk = lambda i: jnp.maximum(i * hb - 1, 0)
    blk = lambda part: pl.BlockSpec((None, res, tr, W), lambda b, r, i: (b, r, i, part))
    halo = lambda part: pl.BlockSpec((None, res, DSA_BLOCK, W), lambda b, r, i: (b, r, prev_blk(i), part))
    if kt is None:
        v_part = 2
        k_arg, k_spec, kh_spec = x, blk(1), halo(1)
    else:
        v_part = 1
        k_arg = kt
        k_spec = pl.BlockSpec((None, hb, res, W, DSA_BLOCK), lambda b, r, i: (b, i, r, 0, 0))
        kh_spec = pl.BlockSpec((None, None, res, W, DSA_BLOCK), lambda b, r, i: (b, prev_blk(i), r, 0, 0))
    return pl.pallas_call(
        functools.partial(_dsa_kernel, res=res, blocks=hb, keys_transposed=kt is not None),
        grid=(batch, d // res, n // tr),
        in_specs=[blk(0), k_spec, blk(v_part), kh_spec, halo(v_part),
                  pl.BlockSpec((None, DSA_HEADS, DSA_BLOCK, 2 * DSA_BLOCK), lambda b, r, i: (branch, 0, 0, 0))],
        out_specs=[pl.BlockSpec((None, res, tr, W), lambda b, r, i: (b, r, i, 0)),
                   pl.BlockSpec((None, res, tr, LANES), lambda b, r, i: (b, r, i, 0))],
        out_shape=[jax.ShapeDtypeStruct((batch, d, n, W), BF16),
                   jax.ShapeDtypeStruct((batch, d, n, LANES), F32)],
        compiler_params=pltpu.CompilerParams(dimension_semantics=("arbitrary",) * 3,
                                             vmem_limit_bytes=VMEM_LIMIT),
        name=f"dsa_d{d}",
    )(x, k_arg, x, k_arg, x, bias)


def _out_kernel(x_ref, oa_ref, o1_ref, o4_ref, o16_ref, l1_ref, l4_ref, l16_ref,
                wo_ref, g2_ref, w1_ref, w2_ref, gf_ref, y_ref, osc_ref, lsc_ref, ob_ref, *, ff_chunk):
    tm = x_ref.shape[0]
    npair = DSA_HEADS // 2
    lane = lax.broadcasted_iota(jnp.int32, (tm, LANES), 1)
    first_half = lane < DSA_DH

    @pl.when(pl.program_id(0) == 0)
    def _():
        ob_ref[...] = jnp.zeros_like(ob_ref)

    mixed = _dot(oa_ref[...], wo_ref[:GLA_WIDTH, :]) + _dot(ob_ref[...], wo_ref[GLA_WIDTH:, :])
    h = x_ref[...] + mixed
    nm = (h * lax.rsqrt(jnp.mean(h * h, axis=-1, keepdims=True) + EPS) * g2_ref[...]).astype(BF16)
    ff = None
    for c in range(D_FF // ff_chunk):
        cols = slice(c * ff_chunk, (c + 1) * ff_chunk)
        a = jnp.maximum(_dot(nm, w1_ref[:, cols]), 0.0)
        d = _dot((a * a).astype(BF16), w2_ref[cols, :])
        ff = d if ff is None else ff + d
    h = h + ff
    y_ref[...] = h * lax.rsqrt(jnp.mean(h * h, axis=-1, keepdims=True) + EPS) * gf_ref[...]

    for j, (d, o_ref, l_ref) in enumerate(((4, o4_ref, l4_ref), (16, o16_ref, l16_ref))):
        for r in range(d):
            lsc_ref[j, pl.ds(r, tm // d, stride=d), :] = l_ref[r]
            for hp in range(npair):
                osc_ref[j * npair + hp, pl.ds(r, tm // d, stride=d), :] = (
                    o_ref[r, :, hp * LANES:(hp + 1) * LANES].astype(F32))

    sts = (l1_ref[...], lsc_ref[0], lsc_ref[1])
    m = jnp.maximum(jnp.maximum(sts[0], sts[1]), sts[2])
    es = [jnp.exp2(st - m) for st in sts]
    total = sum(e * pltpu.roll(st, LANES - DSA_HEADS, axis=1) for e, st in zip(es, sts))
    inv = 1.0 / jnp.where(lane < DSA_HEADS, total, 1.0)
    ws = [e * inv for e in es]

    for hp in range(npair):
        grp = slice(hp * LANES, (hp + 1) * LANES)
        branch_o = (o1_ref[:, grp].astype(F32), osc_ref[hp], osc_ref[npair + hp])
        acc = jnp.zeros((tm, LANES), F32)
        for w, o in zip(ws, branch_o):
            wa = jnp.sum(jnp.where(lane == 2 * hp, w, 0.0), axis=-1, keepdims=True)
            wb = jnp.sum(jnp.where(lane == 2 * hp + 1, w, 0.0), axis=-1, keepdims=True)
            acc = acc + jnp.where(first_half, wa, wb) * o
        ob_ref[:, grp] = acc.astype(BF16)


def _out(x2, oa, os_, ls, wo, g2, w1, w2, gf, *, seq, tm, ff_chunk):
    T = x2.shape[0]
    spb = seq // tm
    nt = T // tm
    prev = lambda s: jnp.maximum(s - 1, 0)
    cur = lambda s: jnp.minimum(s, nt - 1)
    row_prev = lambda n: pl.BlockSpec((tm, n), lambda s: (prev(s), 0))
    row_cur = lambda n: pl.BlockSpec((tm, n), lambda s: (cur(s), 0))
    strided = lambda d, n: pl.BlockSpec((None, d, tm // d, n), lambda s: (cur(s) // spb, 0, cur(s) % spb, 0))
    const = lambda a: pl.BlockSpec(a.shape, lambda s: (0,) * a.ndim, pipeline_mode=pl.Buffered(1))
    W = DSA_WIDTH
    return pl.pallas_call(
        functools.partial(_out_kernel, ff_chunk=ff_chunk),
        grid=(nt + 1,),
        in_specs=[row_prev(D_MODEL), row_prev(GLA_WIDTH), row_cur(W), strided(4, W), strided(16, W),
                  row_cur(LANES), strided(4, LANES), strided(16, LANES),
                  const(wo), const(g2), const(w1), const(w2), const(gf)],
        out_specs=row_prev(D_MODEL),
        out_shape=jax.ShapeDtypeStruct((T, D_MODEL), F32),
        scratch_shapes=[pltpu.VMEM((2 * (W // LANES), tm, LANES), F32),
                        pltpu.VMEM((2, tm, LANES), F32),
                        pltpu.VMEM((tm, W), BF16)],
        compiler_params=pltpu.CompilerParams(dimension_semantics=("arbitrary",),
                                             vmem_limit_bytes=VMEM_LIMIT),
        name="out_mlp",
    )(x2, oa, *os_, *ls, wo, g2, w1, w2, gf)


_IN_SPLITS = (GLA_QK, GLA_QK, GLA_WIDTH, GLA_WIDTH, GLA_RANK, DSA_WIDTH, DSA_WIDTH, DSA_WIDTH)
_IN_OFFSETS = tuple(int(v) for v in np.cumsum((0,) + _IN_SPLITS))


def _split_w_in_kernel(w_ref, wq_ref, wk_ref, wv_ref, wr_ref, wg_ref, wd_ref, *, row_chunk):
    o = _IN_OFFSETS
    lane = lax.broadcasted_iota(jnp.int32, (row_chunk, RANK_PAD), 1)
    for i in range(w_ref.shape[0] // row_chunk):
        rows = slice(i * row_chunk, (i + 1) * row_chunk)
        wq_ref[rows, :] = (w_ref[rows, o[0]:o[1]] * GLA_DK ** -0.5).astype(BF16)
        wk_ref[rows, :] = w_ref[rows, o[1]:o[2]].astype(BF16)
        wv_ref[rows, :] = w_ref[rows, o[2]:o[3]].astype(BF16)
        wr_ref[rows, :] = w_ref[rows, o[3]:o[4]].astype(BF16)
        wg_ref[rows, :] = jnp.where(lane < GLA_RANK, w_ref[rows, o[4]:o[4] + RANK_PAD], 0.0).astype(BF16)
        wd_ref[rows, :DSA_WIDTH] = (w_ref[rows, o[5]:o[6]] * (DSA_DH ** -0.5 * LOG2E)).astype(BF16)
        wd_ref[rows, DSA_WIDTH:] = w_ref[rows, o[6]:o[8]].astype(BF16)


def _split_w_in(w):
    K = w.shape[0]
    shapes = [(K, GLA_QK), (K, GLA_QK), (K, GLA_WIDTH), (K, GLA_WIDTH), (K, RANK_PAD), (K, 3 * DSA_WIDTH)]
    return pl.pallas_call(
        functools.partial(_split_w_in_kernel, row_chunk=128),
        out_shape=[jax.ShapeDtypeStruct(s, BF16) for s in shapes],
        compiler_params=pltpu.CompilerParams(vmem_limit_bytes=VMEM_LIMIT),
        name="split_w_in",
    )(w)


def kernel(x, attn_norm_g, w_in, gla_gate_w2, gla_gate_b, gla_norm_g, rel_bias, w_out, mlp_norm_g,
           w_ff1, w_ff2, final_norm_g):
    batch, seq, _ = x.shape
    assert seq % (DSA_PATTERN[-1][1] * DSA_BLOCK) == 0
    T = batch * seq
    x2 = x.reshape(T, D_MODEL)

    wq, wk, wv, wr, wg, wd = _split_w_in(w_in[0])
    w2 = jnp.pad(gla_gate_w2[0], ((0, RANK_PAD - GLA_RANK), (0, 0))).astype(BF16)
    gb = gla_gate_b[0].astype(F32).reshape(1, GLA_QK)

    gq, gv, gr, d1, kt, glog, glogt, d4, d16, kt1, kt4 = _proj(
        x2, attn_norm_g[0].reshape(1, D_MODEL).astype(F32), wq, wv, wr, wk, wg, w2, gb, wd,
        batch=batch, seq=seq, tm=1024)

    o_a = _gla(gq, kt, gv, gr, glog, glogt, gla_norm_g[0].reshape(1, GLA_WIDTH).astype(F32),
               batch=batch, seq=seq, tg=2048)

    bias = _bias_tables(rel_bias)
    os_, ls = [], []
    branches = ((d1.reshape(batch, 1, seq, 2 * DSA_WIDTH), kt1), (d4, kt4), (d16, None))
    for branch, (xd, ktd) in enumerate(branches):
        o, lse = _dsa_branch(xd, ktd, bias, branch, rows_per_step=2048)
        os_.append(o)
        ls.append(lse)
    os_[0] = os_[0].reshape(T, DSA_WIDTH)
    ls[0] = ls[0].reshape(T, LANES)

    y = _out(x2, o_a, os_, ls, w_out[0].astype(BF16), mlp_norm_g[0].reshape(1, D_MODEL).astype(F32),
             w_ff1[0].astype(BF16), w_ff2[0].astype(BF16), final_norm_g.reshape(1, D_MODEL).astype(F32),
             seq=seq, tm=512, ff_chunk=1024)
    return y.reshape(batch, seq, D_MODEL)
```

```python
import functools
import math

import numpy as np
import jax
import jax.numpy as jnp
from jax import lax
from jax.experimental import pallas as pl
from jax.experimental.pallas import tpu as pltpu

D_MODEL = 1024
GLA_WIDTH = 512
GLA_HEADS = 4
GLA_DK = 64
GLA_DV = 128
GLA_QK = GLA_HEADS * GLA_DK
GLA_RANK = 16
GLA_TAU = 16.0
GLA_CHUNK = 64
DSA_WIDTH = 512
DSA_HEADS = 8
DSA_DH = 64
DSA_PATTERN = ((128, 1), (512, 4), (2048, 16))
DSA_BLOCK = 128
REL_BUCKETS = 32
REL_MAX_DIST = 2048
D_FF = 4096
EPS = 1e-6
NEG = -1e30
LOG2E = math.log2(math.e)

LANES = 128
RANK_PAD = LANES
VMEM_LIMIT = 56 * 1024 * 1024

F32 = jnp.float32
BF16 = jnp.bfloat16

_NT = (((1,), (1,)), ((), ()))


def _dot(a, b):
    return jnp.dot(a, b, preferred_element_type=F32)


def _dot_nt(a, b):
    return lax.dot_general(a, b, _NT, preferred_element_type=F32)


def _split_hi_lo(x):
    hi = x.astype(BF16)
    lo = (x - hi.astype(F32)).astype(BF16)
    return hi, lo


def _log_sigmoid(x):
    return jnp.minimum(x, 0.0) - jnp.log1p(jnp.exp(-jnp.abs(x)))


def _bucket_tables():
    max_exact = REL_BUCKETS // 2
    L = DSA_BLOCK
    steps = L + np.arange(L)[:, None] - np.arange(2 * L)[None, :]
    tables = []
    for window, dilation in DSA_PATTERN:
        span = window // dilation
        in_band = (steps >= 0) & (steps <= span)
        n = np.maximum(steps * dilation, 0)
        large = max_exact + (np.log(np.maximum(n, 1) / max_exact)
                             / math.log(REL_MAX_DIST / max_exact)
                             * (REL_BUCKETS - max_exact)).astype(np.int32)
        large = np.minimum(large, REL_BUCKETS - 1)
        bucket = np.where(n < max_exact, n, large).astype(np.int32)
        tables.append(np.where(in_band, bucket, -1).astype(np.int32))
    return np.stack(tables)


def _bias_kernel(rb_ref, bt_ref, out_ref):
    bt = bt_ref[0]

    def one_head(h, carry):
        acc = jnp.full(bt.shape, NEG, F32)
        for b in range(REL_BUCKETS):
            acc = jnp.where(bt == b, rb_ref[b, h] * LOG2E, acc)
        out_ref[0, h] = acc
        return carry

    lax.fori_loop(0, DSA_HEADS, one_head, 0)


def _bias_tables(rel_bias):
    nb = len(DSA_PATTERN)
    L = DSA_BLOCK
    return pl.pallas_call(
        _bias_kernel,
        grid=(nb,),
        in_specs=[pl.BlockSpec(memory_space=pltpu.SMEM),
                  pl.BlockSpec((1, L, 2 * L), lambda d: (d, 0, 0))],
        out_specs=pl.BlockSpec((1, DSA_HEADS, L, 2 * L), lambda d: (d, 0, 0, 0)),
        out_shape=jax.ShapeDtypeStruct((nb, DSA_HEADS, L, 2 * L), F32),
        name="dsa_bias",
    )(rel_bias.astype(F32), jnp.asarray(_bucket_tables()))


def _proj_kernel(x_ref, g_ref, wq_ref, wv_ref, wr_ref, wk_ref, wg_ref, w2_ref, b_ref, wd_ref,
                 gq_ref, gv_ref, gr_ref, d1_ref, kt_ref, glog_ref, glogt_ref, d4_ref, d16_ref,
                 kt1_ref, kt4_ref, dsc_ref, t4_ref):
    tm = x_ref.shape[0]
    ncol = 3 * DSA_WIDTH // LANES
    kcols = range(DSA_WIDTH // LANES, 2 * DSA_WIDTH // LANES)
    n4 = tm // 4
    n16 = tm // 16

    x = x_ref[...]
    ms = jnp.mean(x * x, axis=-1, keepdims=True)
    nx = (x * lax.rsqrt(ms + EPS) * g_ref[...]).astype(BF16)

    kt_ref[0] = _dot(nx, wk_ref[...]).T.astype(BF16)
    glow = _dot(nx, wg_ref[...]).astype(BF16)
    glog = _log_sigmoid(_dot(glow, w2_ref[...]) + b_ref[...]) * (LOG2E / GLA_TAU)
    glog_ref[...] = glog
    glogt_ref[0] = glog.T

    for j in range(ncol // 2):
        res = _dot(nx, wd_ref[:, 2 * j * LANES:(2 * j + 2) * LANES])
        halves = {2 * j: res[:, :LANES], 2 * j + 1: res[:, LANES:]}
        for c in halves:
            dsc_ref[c % 2] = halves[c]
        for c in halves:
            sl = c % 2
            cols = slice(c * LANES, (c + 1) * LANES)
            is_key = c in kcols
            krows = slice((c - kcols[0]) * LANES, (c - kcols[0] + 1) * LANES)
            qv = c if c < kcols[0] else c - len(kcols)
            qv_cols = slice(qv * LANES, (qv + 1) * LANES)
            if is_key:
                for jb in range(tm // DSA_BLOCK):
                    blk_rows = slice(jb * DSA_BLOCK, (jb + 1) * DSA_BLOCK)
                    kt1_ref[jb, krows, :] = dsc_ref[sl, blk_rows, :].T.astype(BF16)
            else:
                d1_ref[:, qv_cols] = halves[c].astype(BF16)
            for r4 in range(4):
                sub = dsc_ref[sl, pl.ds(r4, n4, stride=4), :]
                t4_ref[sl, r4 * n4:(r4 + 1) * n4, :] = sub
                if is_key:
                    for jb in range(n4 // DSA_BLOCK):
                        blk_rows = slice(jb * DSA_BLOCK, (jb + 1) * DSA_BLOCK)
                        kt4_ref[jb, r4, krows, :] = sub[blk_rows, :].T.astype(BF16)
                else:
                    d4_ref[r4, :, qv_cols] = sub.astype(BF16)
            for r4 in range(4):
                for r2 in range(4):
                    d16_ref[r4 + 4 * r2, :, cols] = (
                        t4_ref[sl, pl.ds(r4 * n4 + r2, n16, stride=4), :].astype(BF16))

    gq_ref[...] = _dot(nx, wq_ref[...]).astype(BF16)
    gv_ref[...] = _dot(nx, wv_ref[...]).astype(BF16)
    gr_ref[...] = _dot(nx, wr_ref[...]).astype(BF16)


def _proj(x2, g, wq, wv, wr, wk, wg, w2, b, wd, *, batch, seq, tm):
    T = batch * seq
    spb = seq // tm
    nt = T // tm
    full = lambda a: pl.BlockSpec(a.shape, lambda s: (0,) * a.ndim, pipeline_mode=pl.Buffered(1))
    row_cur = lambda n: pl.BlockSpec((tm, n), lambda s: (s, 0))
    row_prev = row_cur
    colt = pl.BlockSpec((1, GLA_QK, tm), lambda s: (s // spb, 0, s % spb))
    W, L = DSA_WIDTH, DSA_BLOCK
    W2, W3 = 2 * W, 3 * W
    strided = lambda d, n: pl.BlockSpec((None, d, tm // d, n), lambda s: (s // spb, 0, s % spb, 0))
    assert tm % (4 * L) == 0
    kt1_spec = pl.BlockSpec((None, tm // L, None, W, L), lambda s: (s // spb, s % spb, 0, 0, 0))
    kt4_spec = pl.BlockSpec((None, tm // (4 * L), 4, W, L), lambda s: (s // spb, s % spb, 0, 0, 0))
    return pl.pallas_call(
        _proj_kernel,
        grid=(nt,),
        in_specs=[row_cur(D_MODEL)] + [full(a) for a in (g, wq, wv, wr, wk, wg, w2, b, wd)],
        out_specs=[row_cur(GLA_QK), row_cur(GLA_WIDTH), row_cur(GLA_WIDTH), row_cur(W2),
                   colt, row_prev(GLA_QK), colt, strided(4, W2), strided(16, W3), kt1_spec, kt4_spec],
        out_shape=[jax.ShapeDtypeStruct((T, GLA_QK), BF16),
                   jax.ShapeDtypeStruct((T, GLA_WIDTH), BF16),
                   jax.ShapeDtypeStruct((T, GLA_WIDTH), BF16),
                   jax.ShapeDtypeStruct((T, W2), BF16),
                   jax.ShapeDtypeStruct((batch, GLA_QK, seq), BF16),
                   jax.ShapeDtypeStruct((T, GLA_QK), F32),
                   jax.ShapeDtypeStruct((batch, GLA_QK, seq), F32),
                   jax.ShapeDtypeStruct((batch, 4, seq // 4, W2), BF16),
                   jax.ShapeDtypeStruct((batch, 16, seq // 16, W3), BF16),
                   jax.ShapeDtypeStruct((batch, seq // L, 1, W, L), BF16),
                   jax.ShapeDtypeStruct((batch, seq // (4 * L), 4, W, L), BF16)],
        scratch_shapes=[pltpu.VMEM((2, tm, LANES), F32)] * 2,
        compiler_params=pltpu.CompilerParams(dimension_semantics=("arbitrary",),
                                             vmem_limit_bytes=VMEM_LIMIT),
        name="proj",
    )(x2, g, wq, wv, wr, wk, wg, w2, b, wd)


def _gla_kernel(gq_ref, kt_ref, gv_ref, gr_ref, glog_ref, glogt_ref, gn_ref, o_ref, s_ref, *, pairs):
    C = GLA_CHUNK
    P = 2 * C

    @pl.when(pl.program_id(1) == 0)
    def _():
        s_ref[...] = jnp.zeros_like(s_ref)

    ri = lax.broadcasted_iota(jnp.int32, (P, P), 0)
    ci = lax.broadcasted_iota(jnp.int32, (P, P), 1)
    same_chunk = (ri < C) == (ci < C)
    causal = same_chunk & (ci <= ri)
    low = jnp.where(causal, 1.0, 0.0).astype(BF16)
    upp = jnp.where(same_chunk & (ri <= ci), 1.0, 0.0).astype(BF16)
    lane = lax.broadcasted_iota(jnp.int32, (P, LANES), 1)
    first_half = lane < C
    lane_t = lax.broadcasted_iota(jnp.int32, (GLA_QK, P), 1)
    first_t = lane_t < C

    zero16 = jnp.zeros((), BF16)
    state = [s_ref[h * GLA_DK:(h + 1) * GLA_DK, :] for h in range(GLA_HEADS)]

    heads = range(GLA_HEADS)
    hk = [slice(h * GLA_DK, (h + 1) * GLA_DK) for h in heads]
    hv = [slice(h * GLA_DV, (h + 1) * GLA_DV) for h in heads]
    grp = [slice((h // 2) * LANES, (h // 2 + 1) * LANES) for h in heads]

    def decay_stage(p):
        rows = slice(p * P, (p + 1) * P)
        g_hi, g_lo = _split_hi_lo(glog_ref[rows, :])
        b = _dot(low, g_hi) + _dot(low, g_lo)
        gt = glogt_ref[0, :, rows]
        gt_hi, gt_lo = _split_hi_lo(gt)
        bt = _dot(gt_hi, upp) + _dot(gt_lo, upp)
        tot_a = jnp.sum(jnp.where(first_t, gt, 0.0), axis=-1, keepdims=True)
        tot_b = jnp.sum(jnp.where(first_t, 0.0, gt), axis=-1, keepdims=True)
        return dict(rows=rows, b=b, bt=bt, dec_a=jnp.exp2(tot_a), dec_b=jnp.exp2(tot_b))

    def score_stage(c):
        rows = c["rows"]
        qd = (gq_ref[rows, :].astype(F32) * jnp.exp2(c["b"])).astype(BF16)
        kinv_t = (kt_ref[0, :, rows].astype(F32) * jnp.exp2(-c["bt"])).astype(BF16)
        kinv_a = jnp.where(first_t, kinv_t, zero16)
        kinv_b = jnp.where(first_t, zero16, kinv_t)
        c["v"] = [gv_ref[rows, hv[h]] for h in heads]
        c["qm"] = [jnp.where(first_half if h % 2 == 0 else ~first_half, qd[:, grp[h]], zero16) for h in heads]
        c["att"] = [_dot(c["qm"][h], kinv_t[grp[h], :]) for h in heads]
        c["upd_a"] = [_dot(kinv_a[hk[h], :], c["v"][h]) for h in heads]
        c["upd_b"] = [_dot(kinv_b[hk[h], :], c["v"][h]) for h in heads]

    def output_stage(c):
        o = []
        s_a = list(state)
        s_b = [c["dec_a"][hk[h], :] * (s_a[h] + c["upd_a"][h]) for h in heads]
        for h in heads:
            state[h] = c["dec_b"][hk[h], :] * (s_b[h] + c["upd_b"][h])
        for h in heads:
            att = jnp.where(causal, c["att"][h], 0.0).astype(BF16)
            pair_a = [s_a[h], s_a[h ^ 1]] if h % 2 == 0 else [s_a[h ^ 1], s_a[h]]
            pair_b = [s_b[h], s_a[h ^ 1]] if h % 2 == 0 else [s_a[h ^ 1], s_b[h]]
            o_a = _dot(c["qm"][h][:C, :], jnp.concatenate(pair_a, axis=0).astype(BF16))
            o_b = _dot(c["qm"][h][C:, :], jnp.concatenate(pair_b, axis=0).astype(BF16))
            o.append(_dot(att, c["v"][h]) + jnp.concatenate([o_a, o_b], axis=0))
        c["o"] = o

    def norm_stage(c):
        rows = c["rows"]
        for h in heads:
            o = c["o"][h]
            o = o * lax.rsqrt(jnp.mean(o * o, axis=-1, keepdims=True) + EPS) * gn_ref[:, hv[h]]
            r = gr_ref[rows, hv[h]].astype(F32)
            o_ref[rows, hv[h]] = (o * (r * jax.nn.sigmoid(r))).astype(BF16)

    ctx = {}
    for t in range(pairs + 3):
        if t < pairs:
            ctx[t] = decay_stage(t)
        if 0 <= t - 1 < pairs:
            score_stage(ctx[t - 1])
        if 0 <= t - 2 < pairs:
            output_stage(ctx[t - 2])
        if 0 <= t - 3 < pairs:
            norm_stage(ctx.pop(t - 3))

    for h in heads:
        s_ref[hk[h], :] = state[h]


def _gla(gq, kt, gv, gr, glog, glogt, gn, *, batch, seq, tg):
    T = batch * seq
    spb = seq // tg
    row = lambda n: pl.BlockSpec((tg, n), lambda b, i: (b * spb + i, 0))
    colt = pl.BlockSpec((1, GLA_QK, tg), lambda b, i: (b, 0, i))
    return pl.pallas_call(
        functools.partial(_gla_kernel, pairs=tg // (2 * GLA_CHUNK)),
        grid=(batch, spb),
        in_specs=[row(GLA_QK), colt, row(GLA_WIDTH), row(GLA_WIDTH), row(GLA_QK), colt,
                  pl.BlockSpec((1, GLA_WIDTH), lambda b, i: (0, 0))],
        out_specs=row(GLA_WIDTH),
        out_shape=jax.ShapeDtypeStruct((T, GLA_WIDTH), BF16),
        scratch_shapes=[pltpu.VMEM((GLA_QK, GLA_DV), F32)],
        compiler_params=pltpu.CompilerParams(dimension_semantics=("arbitrary", "arbitrary"),
                                             vmem_limit_bytes=VMEM_LIMIT),
        name="gla",
    )(gq, kt, gv, gr, glog, glogt, gn)


def _dsa_kernel(q_ref, k_ref, v_ref, kh_ref, vh_ref, bias_ref, o_ref, st_ref, *, res, blocks, keys_transposed):
    L = DSA_BLOCK
    first_tile = pl.program_id(2) == 0
    lane = lax.broadcasted_iota(jnp.int32, (L, LANES), 1)
    first_half = lane < DSA_DH
    prev_cols = lax.broadcasted_iota(jnp.int32, (L, 2 * L), 1) < L

    st_tiles = {}
    pair_out = {}

    def score_stage(r, blk, h):
        rows = slice(blk * L, (blk + 1) * L)
        both = slice((blk - 1) * L, (blk + 1) * L)
        grp = slice((h // 2) * LANES, (h // 2 + 1) * LANES)
        if keys_transposed:
            k_prev = kh_ref[r, grp, :] if blk == 0 else k_ref[blk - 1, r, grp, :]
            kcat = jnp.concatenate([k_prev, k_ref[blk, r, grp, :]], axis=1)
        elif blk == 0:
            kcat = jnp.concatenate([kh_ref[r, :, grp], k_ref[r, rows, grp]], axis=0)
        else:
            kcat = k_ref[r, both, grp]
        own = first_half if h % 2 == 0 else ~first_half
        qm = jnp.where(own, q_ref[r, rows, grp], jnp.zeros((), BF16))
        return _dot(qm, kcat) if keys_transposed else _dot_nt(qm, kcat)

    def value_stage(r, blk, h, s):
        rows = slice(blk * L, (blk + 1) * L)
        grp = slice((h // 2) * LANES, (h // 2 + 1) * LANES)
        bias = bias_ref[h]
        if blk == 0:
            bias = jnp.where(prev_cols & first_tile, NEG, bias)
        s = s + bias
        m = jnp.max(s, axis=-1, keepdims=True)
        p = jnp.exp2(s - m)
        den = jnp.sum(p, axis=-1, keepdims=True)
        st_old = st_tiles.get((r, blk), jnp.zeros((L, LANES), F32))
        st_tiles[r, blk] = jnp.where(lane == h, m, jnp.where(lane == DSA_HEADS + h, den, st_old))
        if blk == 0:
            vcat = jnp.concatenate([vh_ref[r, :, grp], v_ref[r, rows, grp]], axis=0)
        else:
            vcat = v_ref[r, both_rows(blk), grp]
        out = _dot(p.astype(BF16), vcat)
        if h % 2 == 0:
            pair_out[r, blk] = out
        else:
            o_ref[r, rows, grp] = jnp.where(first_half, pair_out.pop((r, blk)), out).astype(BF16)
        if h == DSA_HEADS - 1:
            st_ref[r, rows, :] = st_tiles.pop((r, blk))

    def both_rows(blk):
        return slice((blk - 1) * L, (blk + 1) * L)

    items = [(r, blk, h) for r in range(res) for blk in range(blocks) for h in range(DSA_HEADS)]
    lag = 1
    pending = {}
    for t in range(len(items) + lag):
        if t < len(items):
            pending[t] = score_stage(*items[t])
        if t - lag >= 0:
            value_stage(*items[t - lag], pending.pop(t - lag))


def _dsa_branch(x, kt, bias, branch, *, rows_per_step):
    batch, d, n, _ = x.shape
    tr = min(rows_per_step, n)
    res = min(rows_per_step // tr, d)
    hb = tr // DSA_BLOCK
    W = DSA_WIDTH
    prev_blk = lambda i: jnp.maximum(i * hb - 1, 0)
    blk = lambda part: pl.BlockSpec((None, res, tr, W), lambda b, r, i: (b, r, i, part))
    halo = lambda part: pl.BlockSpec((None, res, DSA_BLOCK, W), lambda b, r, i: (b, r, prev_blk(i), part))
    if kt is None:
        v_part = 2
        k_arg, k_spec, kh_spec = x, blk(1), halo(1)
    else:
        v_part = 1
        k_arg = kt
        k_spec = pl.BlockSpec((None, hb, res, W, DSA_BLOCK), lambda b, r, i: (b, i, r, 0, 0))
        kh_spec = pl.BlockSpec((None, None, res, W, DSA_BLOCK), lambda b, r, i: (b, prev_blk(i), r, 0, 0))
    return pl.pallas_call(
        functools.partial(_dsa_kernel, res=res, blocks=hb, keys_transposed=kt is not None),
        grid=(batch, d // res, n // tr),
        in_specs=[blk(0), k_spec, blk(v_part), kh_spec, halo(v_part),
                  pl.BlockSpec((None, DSA_HEADS, DSA_BLOCK, 2 * DSA_BLOCK), lambda b, r, i: (branch, 0, 0, 0))],
        out_specs=[pl.BlockSpec((None, res, tr, W), lambda b, r, i: (b, r, i, 0)),
                   pl.BlockSpec((None, res, tr, LANES), lambda b, r, i: (b, r, i, 0))],
        out_shape=[jax.ShapeDtypeStruct((batch, d, n, W), BF16),
                   jax.ShapeDtypeStruct((batch, d, n, LANES), F32)],
        compiler_params=pltpu.CompilerParams(dimension_semantics=("arbitrary",) * 3,
                                             vmem_limit_bytes=VMEM_LIMIT),
        name=f"dsa_d{d}",
    )(x, k_arg, x, k_arg, x, bias)


def _out_kernel(x_ref, oa_ref, o1_ref, o4_ref, o16_ref, l1_ref, l4_ref, l16_ref,
                wo_ref, g2_ref, w1_ref, w2_ref, gf_ref, y_ref, osc_ref, lsc_ref, ob_ref, *, ff_chunk):
    tm = x_ref.shape[0]
    npair = DSA_HEADS // 2
    lane = lax.broadcasted_iota(jnp.int32, (tm, LANES), 1)
    first_half = lane < DSA_DH

    @pl.when(pl.program_id(0) == 0)
    def _():
        ob_ref[...] = jnp.zeros_like(ob_ref)

    mixed = _dot(oa_ref[...], wo_ref[:GLA_WIDTH, :]) + _dot(ob_ref[...], wo_ref[GLA_WIDTH:, :])
    h = x_ref[...] + mixed
    nm = (h * lax.rsqrt(jnp.mean(h * h, axis=-1, keepdims=True) + EPS) * g2_ref[...]).astype(BF16)
    ff = None
    for c in range(D_FF // ff_chunk):
        cols = slice(c * ff_chunk, (c + 1) * ff_chunk)
        a = jnp.maximum(_dot(nm, w1_ref[:, cols]), 0.0)
        d = _dot((a * a).astype(BF16), w2_ref[cols, :])
        ff = d if ff is None else ff + d
    h = h + ff
    y_ref[...] = h * lax.rsqrt(jnp.mean(h * h, axis=-1, keepdims=True) + EPS) * gf_ref[...]

    for j, (d, o_ref, l_ref) in enumerate(((4, o4_ref, l4_ref), (16, o16_ref, l16_ref))):
        for r in range(d):
            lsc_ref[j, pl.ds(r, tm // d, stride=d), :] = l_ref[r]
            for hp in range(npair):
                osc_ref[j * npair + hp, pl.ds(r, tm // d, stride=d), :] = (
                    o_ref[r, :, hp * LANES:(hp + 1) * LANES].astype(F32))

    sts = (l1_ref[...], lsc_ref[0], lsc_ref[1])
    m = jnp.maximum(jnp.maximum(sts[0], sts[1]), sts[2])
    es = [jnp.exp2(st - m) for st in sts]
    total = sum(e * pltpu.roll(st, LANES - DSA_HEADS, axis=1) for e, st in zip(es, sts))
    inv = 1.0 / jnp.where(lane < DSA_HEADS, total, 1.0)
    ws = [e * inv for e in es]

    for hp in range(npair):
        grp = slice(hp * LANES, (hp + 1) * LANES)
        branch_o = (o1_ref[:, grp].astype(F32), osc_ref[hp], osc_ref[npair + hp])
        acc = jnp.zeros((tm, LANES), F32)
        for w, o in zip(ws, branch_o):
            wa = jnp.sum(jnp.where(lane == 2 * hp, w, 0.0), axis=-1, keepdims=True)
            wb = jnp.sum(jnp.where(lane == 2 * hp + 1, w, 0.0), axis=-1, keepdims=True)
            acc = acc + jnp.where(first_half, wa, wb) * o
        ob_ref[:, grp] = acc.astype(BF16)


def _out(x2, oa, os_, ls, wo, g2, w1, w2, gf, *, seq, tm, ff_chunk):
    T = x2.shape[0]
    spb = seq // tm
    nt = T // tm
    prev = lambda s: jnp.maximum(s - 1, 0)
    cur = lambda s: jnp.minimum(s, nt - 1)
    row_prev = lambda n: pl.BlockSpec((tm, n), lambda s: (prev(s), 0))
    row_cur = lambda n: pl.BlockSpec((tm, n), lambda s: (cur(s), 0))
    strided = lambda d, n: pl.BlockSpec((None, d, tm // d, n), lambda s: (cur(s) // spb, 0, cur(s) % spb, 0))
    const = lambda a: pl.BlockSpec(a.shape, lambda s: (0,) * a.ndim, pipeline_mode=pl.Buffered(1))
    W = DSA_WIDTH
    return pl.pallas_call(
        functools.partial(_out_kernel, ff_chunk=ff_chunk),
        grid=(nt + 1,),
        in_specs=[row_prev(D_MODEL), row_prev(GLA_WIDTH), row_cur(W), strided(4, W), strided(16, W),
                  row_cur(LANES), strided(4, LANES), strided(16, LANES),
                  const(wo), const(g2), const(w1), const(w2), const(gf)],
        out_specs=row_prev(D_MODEL),
        out_shape=jax.ShapeDtypeStruct((T, D_MODEL), F32),
        scratch_shapes=[pltpu.VMEM((2 * (W // LANES), tm, LANES), F32),
                        pltpu.VMEM((2, tm, LANES), F32),
                        pltpu.VMEM((tm, W), BF16)],
        compiler_params=pltpu.CompilerParams(dimension_semantics=("arbitrary",),
                                             vmem_limit_bytes=VMEM_LIMIT),
        name="out_mlp",
    )(x2, oa, *os_, *ls, wo, g2, w1, w2, gf)


_IN_SPLITS = (GLA_QK, GLA_QK, GLA_WIDTH, GLA_WIDTH, GLA_RANK, DSA_WIDTH, DSA_WIDTH, DSA_WIDTH)
_IN_OFFSETS = tuple(int(v) for v in np.cumsum((0,) + _IN_SPLITS))


def _split_w_in_kernel(wt_ref, wq_ref, wk_ref, wv_ref, wr_ref, wg_ref, wd_ref):
    o = _IN_OFFSETS
    lane = lax.broadcasted_iota(jnp.int32, (wt_ref.shape[1], RANK_PAD), 1)

    def piece(start):
        return wt_ref[start:start + LANES, :].T

    for dst, start, width, scale in ((wq_ref, o[0], GLA_QK, GLA_DK ** -0.5), (wk_ref, o[1], GLA_QK, None),
                                     (wv_ref, o[2], GLA_WIDTH, None), (wr_ref, o[3], GLA_WIDTH, None),
                                     (wd_ref, o[5], 3 * DSA_WIDTH, None)):
        for j in range(width // LANES):
            blk = piece(start + j * LANES)
            if scale is not None:
                blk = blk * scale
            if dst is wd_ref and j < DSA_WIDTH // LANES:
                blk = blk * (DSA_DH ** -0.5 * LOG2E)
            dst[:, j * LANES:(j + 1) * LANES] = blk.astype(BF16)
    wg_ref[...] = jnp.where(lane < GLA_RANK, piece(o[4]), 0.0).astype(BF16)


def _split_w_in(wt):
    _, N, K = wt.shape
    shapes = [(K, GLA_QK), (K, GLA_QK), (K, GLA_WIDTH), (K, GLA_WIDTH), (K, RANK_PAD), (K, 3 * DSA_WIDTH)]
    return pl.pallas_call(
        _split_w_in_kernel,
        grid=(1,),
        in_specs=[pl.BlockSpec((None, N, K), lambda i: (0, 0, 0))],
        out_specs=[pl.BlockSpec(s, lambda i: (0, 0)) for s in shapes],
        out_shape=[jax.ShapeDtypeStruct(s, BF16) for s in shapes],
        compiler_params=pltpu.CompilerParams(vmem_limit_bytes=VMEM_LIMIT),
        name="split_w_in",
    )(wt)


def kernel(x, attn_norm_g, w_in, gla_gate_w2, gla_gate_b, gla_norm_g, rel_bias, w_out, mlp_norm_g,
           w_ff1, w_ff2, final_norm_g):
    batch, seq, _ = x.shape
    assert seq % (DSA_PATTERN[-1][1] * DSA_BLOCK) == 0
    T = batch * seq
    x2 = x.reshape(T, D_MODEL)

    wq, wk, wv, wr, wg, wd = _split_w_in(jnp.swapaxes(w_in, 1, 2))
    w2 = jnp.pad(gla_gate_w2[0], ((0, RANK_PAD - GLA_RANK), (0, 0))).astype(BF16)
    gb = gla_gate_b[0].astype(F32).reshape(1, GLA_QK)

    gq, gv, gr, d1, kt, glog, glogt, d4, d16, kt1, kt4 = _proj(
        x2, attn_norm_g[0].reshape(1, D_MODEL).astype(F32), wq, wv, wr, wk, wg, w2, gb, wd,
        batch=batch, seq=seq, tm=1024)

    o_a = _gla(gq, kt, gv, gr, glog, glogt, gla_norm_g[0].reshape(1, GLA_WIDTH).astype(F32),
               batch=batch, seq=seq, tg=2048)

    bias = _bias_tables(rel_bias)
    os_, ls = [], []
    branches = ((d1.reshape(batch, 1, seq, 2 * DSA_WIDTH), kt1), (d4, kt4), (d16, None))
    for branch, (xd, ktd) in enumerate(branches):
        o, lse = _dsa_branch(xd, ktd, bias, branch, rows_per_step=2048)
        os_.append(o)
        ls.append(lse)
    os_[0] = os_[0].reshape(T, DSA_WIDTH)
    ls[0] = ls[0].reshape(T, LANES)

    y = _out(x2, o_a, os_, ls, w_out[0].astype(BF16), mlp_norm_g[0].reshape(1, D_MODEL).astype(F32),
             w_ff1[0].astype(BF16), w_ff2[0].astype(BF16), final_norm_g.reshape(1, D_MODEL).astype(F32),
             seq=seq, tm=512, ff_chunk=1024)
    return y.reshape(batch, seq, D_MODEL)
```

```python
import functools
import math

import numpy as np
import jax
import jax.numpy as jnp
from jax import lax
from jax.experimental import pallas as pl
from jax.experimental.pallas import tpu as pltpu

D_MODEL = 1024
GLA_WIDTH = 512
GLA_HEADS = 4
GLA_DK = 64
GLA_DV = 128
GLA_QK = GLA_HEADS * GLA_DK
GLA_RANK = 16
GLA_TAU = 16.0
GLA_CHUNK = 64
DSA_WIDTH = 512
DSA_HEADS = 8
DSA_DH = 64
DSA_PATTERN = ((128, 1), (512, 4), (2048, 16))
DSA_BLOCK = 128
REL_BUCKETS = 32
REL_MAX_DIST = 2048
D_FF = 4096
EPS = 1e-6
NEG = -1e30
LOG2E = math.log2(math.e)

LANES = 128
RANK_PAD = LANES
VMEM_LIMIT = 56 * 1024 * 1024

F32 = jnp.float32
BF16 = jnp.bfloat16

_NT = (((1,), (1,)), ((), ()))


def _dot(a, b):
    return jnp.dot(a, b, preferred_element_type=F32)


def _dot_nt(a, b):
    return lax.dot_general(a, b, _NT, preferred_element_type=F32)


def _split_hi_lo(x):
    hi = x.astype(BF16)
    lo = (x - hi.astype(F32)).astype(BF16)
    return hi, lo


def _log_sigmoid(x):
    return jnp.minimum(x, 0.0) - jnp.log1p(jnp.exp(-jnp.abs(x)))


def _bucket_tables():
    max_exact = REL_BUCKETS // 2
    L = DSA_BLOCK
    steps = L + np.arange(L)[:, None] - np.arange(2 * L)[None, :]
    tables = []
    for window, dilation in DSA_PATTERN:
        span = window // dilation
        in_band = (steps >= 0) & (steps <= span)
        n = np.maximum(steps * dilation, 0)
        large = max_exact + (np.log(np.maximum(n, 1) / max_exact)
                             / math.log(REL_MAX_DIST / max_exact)
                             * (REL_BUCKETS - max_exact)).astype(np.int32)
        large = np.minimum(large, REL_BUCKETS - 1)
        bucket = np.where(n < max_exact, n, large).astype(np.int32)
        tables.append(np.where(in_band, bucket, -1).astype(np.int32))
    return np.stack(tables)


def _bias_kernel(rb_ref, bt_ref, out_ref):
    for d in range(len(DSA_PATTERN)):
        bt = bt_ref[d]

        def one_head(h, carry, bt=bt, d=d):
            acc = jnp.full(bt.shape, NEG, F32)
            for b in range(REL_BUCKETS):
                acc = jnp.where(bt == b, rb_ref[b, h] * LOG2E, acc)
            out_ref[d, h] = acc
            return carry

        lax.fori_loop(0, DSA_HEADS, one_head, 0)


def _bias_tables(rel_bias):
    nb = len(DSA_PATTERN)
    L = DSA_BLOCK
    return pl.pallas_call(
        _bias_kernel,
        grid=(1,),
        in_specs=[pl.BlockSpec(memory_space=pltpu.SMEM),
                  pl.BlockSpec((nb, L, 2 * L), lambda i: (0, 0, 0))],
        out_specs=pl.BlockSpec((nb, DSA_HEADS, L, 2 * L), lambda i: (0, 0, 0, 0)),
        out_shape=jax.ShapeDtypeStruct((nb, DSA_HEADS, L, 2 * L), F32),
        name="dsa_bias",
    )(rel_bias.astype(F32), jnp.asarray(_bucket_tables()))


def _proj_kernel(x_ref, g_ref, wq_ref, wv_ref, wr_ref, wk_ref, wg_ref, w2_ref, b_ref, wd_ref,
                 gq_ref, gv_ref, gr_ref, d1_ref, kt_ref, glog_ref, glogt_ref, d4_ref, d16_ref,
                 kt1_ref, kt4_ref, dsc_ref, t4_ref):
    tm = x_ref.shape[0]
    ncol = 3 * DSA_WIDTH // LANES
    kcols = range(DSA_WIDTH // LANES, 2 * DSA_WIDTH // LANES)
    n4 = tm // 4
    n16 = tm // 16

    x = x_ref[...]
    ms = jnp.mean(x * x, axis=-1, keepdims=True)
    nx = (x * lax.rsqrt(ms + EPS) * g_ref[...]).astype(BF16)

    kt_ref[0] = _dot(nx, wk_ref[...]).T.astype(BF16)
    glow = _dot(nx, wg_ref[...]).astype(BF16)
    glog = _log_sigmoid(_dot(glow, w2_ref[...]) + b_ref[...]) * (LOG2E / GLA_TAU)
    glog_ref[...] = glog
    glogt_ref[0] = glog.T

    for j in range(ncol // 2):
        res = _dot(nx, wd_ref[:, 2 * j * LANES:(2 * j + 2) * LANES])
        halves = {2 * j: res[:, :LANES], 2 * j + 1: res[:, LANES:]}
        for c in halves:
            dsc_ref[c % 2] = halves[c]
        for c in halves:
            sl = c % 2
            cols = slice(c * LANES, (c + 1) * LANES)
            is_key = c in kcols
            krows = slice((c - kcols[0]) * LANES, (c - kcols[0] + 1) * LANES)
            qv = c if c < kcols[0] else c - len(kcols)
            qv_cols = slice(qv * LANES, (qv + 1) * LANES)
            if is_key:
                for jb in range(tm // DSA_BLOCK):
                    blk_rows = slice(jb * DSA_BLOCK, (jb + 1) * DSA_BLOCK)
                    kt1_ref[jb, krows, :] = dsc_ref[sl, blk_rows, :].T.astype(BF16)
            else:
                d1_ref[:, qv_cols] = halves[c].astype(BF16)
            for r4 in range(4):
                sub = dsc_ref[sl, pl.ds(r4, n4, stride=4), :]
                t4_ref[sl, r4 * n4:(r4 + 1) * n4, :] = sub
                if is_key:
                    for jb in range(n4 // DSA_BLOCK):
                        blk_rows = slice(jb * DSA_BLOCK, (jb + 1) * DSA_BLOCK)
                        kt4_ref[jb, r4, krows, :] = sub[blk_rows, :].T.astype(BF16)
                else:
                    d4_ref[r4, :, qv_cols] = sub.astype(BF16)
            for r4 in range(4):
                for r2 in range(4):
                    d16_ref[r4 + 4 * r2, :, cols] = (
                        t4_ref[sl, pl.ds(r4 * n4 + r2, n16, stride=4), :].astype(BF16))

    gq_ref[...] = _dot(nx, wq_ref[...]).astype(BF16)
    gv_ref[...] = _dot(nx, wv_ref[...]).astype(BF16)
    gr_ref[...] = _dot(nx, wr_ref[...]).astype(BF16)


def _proj(x2, g, wq, wv, wr, wk, wg, w2, b, wd, *, batch, seq, tm):
    T = batch * seq
    spb = seq // tm
    nt = T // tm
    full = lambda a: pl.BlockSpec(a.shape, lambda s: (0,) * a.ndim, pipeline_mode=pl.Buffered(1))
    row_cur = lambda n: pl.BlockSpec((tm, n), lambda s: (s, 0))
    row_prev = row_cur
    colt = pl.BlockSpec((1, GLA_QK, tm), lambda s: (s // spb, 0, s % spb))
    W, L = DSA_WIDTH, DSA_BLOCK
    W2, W3 = 2 * W, 3 * W
    strided = lambda d, n: pl.BlockSpec((None, d, tm // d, n), lambda s: (s // spb, 0, s % spb, 0))
    assert tm % (4 * L) == 0
    kt1_spec = pl.BlockSpec((None, tm // L, None, W, L), lambda s: (s // spb, s % spb, 0, 0, 0))
    kt4_spec = pl.BlockSpec((None, tm // (4 * L), 4, W, L), lambda s: (s // spb, s % spb, 0, 0, 0))
    return pl.pallas_call(
        _proj_kernel,
        grid=(nt,),
        in_specs=[row_cur(D_MODEL)] + [full(a) for a in (g, wq, wv, wr, wk, wg, w2, b, wd)],
        out_specs=[row_cur(GLA_QK), row_cur(GLA_WIDTH), row_cur(GLA_WIDTH), row_cur(W2),
                   colt, row_prev(GLA_QK), colt, strided(4, W2), strided(16, W3), kt1_spec, kt4_spec],
        out_shape=[jax.ShapeDtypeStruct((T, GLA_QK), BF16),
                   jax.ShapeDtypeStruct((T, GLA_WIDTH), BF16),
                   jax.ShapeDtypeStruct((T, GLA_WIDTH), BF16),
                   jax.ShapeDtypeStruct((T, W2), BF16),
                   jax.ShapeDtypeStruct((batch, GLA_QK, seq), BF16),
                   jax.ShapeDtypeStruct((T, GLA_QK), F32),
                   jax.ShapeDtypeStruct((batch, GLA_QK, seq), F32),
                   jax.ShapeDtypeStruct((batch, 4, seq // 4, W2), BF16),
                   jax.ShapeDtypeStruct((batch, 16, seq // 16, W3), BF16),
                   jax.ShapeDtypeStruct((batch, seq // L, 1, W, L), BF16),
                   jax.ShapeDtypeStruct((batch, seq // (4 * L), 4, W, L), BF16)],
        scratch_shapes=[pltpu.VMEM((2, tm, LANES), F32)] * 2,
        compiler_params=pltpu.CompilerParams(dimension_semantics=("arbitrary",),
                                             vmem_limit_bytes=VMEM_LIMIT),
        name="proj",
    )(x2, g, wq, wv, wr, wk, wg, w2, b, wd)


def _gla_kernel(gq_ref, kt_ref, gv_ref, gr_ref, glog_ref, glogt_ref, gn_ref, o_ref, s_ref, *, pairs):
    C = GLA_CHUNK
    P = 2 * C

    @pl.when(pl.program_id(1) == 0)
    def _():
        s_ref[...] = jnp.zeros_like(s_ref)

    ri = lax.broadcasted_iota(jnp.int32, (P, P), 0)
    ci = lax.broadcasted_iota(jnp.int32, (P, P), 1)
    same_chunk = (ri < C) == (ci < C)
    causal = same_chunk & (ci <= ri)
    low = jnp.where(causal, 1.0, 0.0).astype(BF16)
    upp = jnp.where(same_chunk & (ri <= ci), 1.0, 0.0).astype(BF16)
    lane = lax.broadcasted_iota(jnp.int32, (P, LANES), 1)
    first_half = lane < C
    lane_t = lax.broadcasted_iota(jnp.int32, (GLA_QK, P), 1)
    first_t = lane_t < C

    zero16 = jnp.zeros((), BF16)
    state = [s_ref[h * GLA_DK:(h + 1) * GLA_DK, :] for h in range(GLA_HEADS)]

    heads = range(GLA_HEADS)
    hk = [slice(h * GLA_DK, (h + 1) * GLA_DK) for h in heads]
    hv = [slice(h * GLA_DV, (h + 1) * GLA_DV) for h in heads]
    grp = [slice((h // 2) * LANES, (h // 2 + 1) * LANES) for h in heads]

    def decay_stage(p):
        rows = slice(p * P, (p + 1) * P)
        g_hi, g_lo = _split_hi_lo(glog_ref[rows, :])
        b = _dot(low, g_hi) + _dot(low, g_lo)
        gt = glogt_ref[0, :, rows]
        gt_hi, gt_lo = _split_hi_lo(gt)
        bt = _dot(gt_hi, upp) + _dot(gt_lo, upp)
        tot_a = jnp.sum(jnp.where(first_t, gt, 0.0), axis=-1, keepdims=True)
        tot_b = jnp.sum(jnp.where(first_t, 0.0, gt), axis=-1, keepdims=True)
        return dict(rows=rows, b=b, bt=bt, dec_a=jnp.exp2(tot_a), dec_b=jnp.exp2(tot_b))

    def score_stage(c):
        rows = c["rows"]
        qd = (gq_ref[rows, :].astype(F32) * jnp.exp2(c["b"])).astype(BF16)
        kinv_t = (kt_ref[0, :, rows].astype(F32) * jnp.exp2(-c["bt"])).astype(BF16)
        kinv_a = jnp.where(first_t, kinv_t, zero16)
        kinv_b = jnp.where(first_t, zero16, kinv_t)
        c["v"] = [gv_ref[rows, hv[h]] for h in heads]
        c["qm"] = [jnp.where(first_half if h % 2 == 0 else ~first_half, qd[:, grp[h]], zero16) for h in heads]
        c["att"] = [_dot(c["qm"][h], kinv_t[grp[h], :]) for h in heads]
        c["upd_a"] = [_dot(kinv_a[hk[h], :], c["v"][h]) for h in heads]
        c["upd_b"] = [_dot(kinv_b[hk[h], :], c["v"][h]) for h in heads]

    def output_stage(c):
        o = []
        s_a = list(state)
        s_b = [c["dec_a"][hk[h], :] * (s_a[h] + c["upd_a"][h]) for h in heads]
        for h in heads:
            state[h] = c["dec_b"][hk[h], :] * (s_b[h] + c["upd_b"][h])
        for h in heads:
            att = jnp.where(causal, c["att"][h], 0.0).astype(BF16)
            pair_a = [s_a[h], s_a[h ^ 1]] if h % 2 == 0 else [s_a[h ^ 1], s_a[h]]
            pair_b = [s_b[h], s_a[h ^ 1]] if h % 2 == 0 else [s_a[h ^ 1], s_b[h]]
            o_a = _dot(c["qm"][h][:C, :], jnp.concatenate(pair_a, axis=0).astype(BF16))
            o_b = _dot(c["qm"][h][C:, :], jnp.concatenate(pair_b, axis=0).astype(BF16))
            o.append(_dot(att, c["v"][h]) + jnp.concatenate([o_a, o_b], axis=0))
        c["o"] = o

    def norm_stage(c):
        rows = c["rows"]
        for h in heads:
            o = c["o"][h]
            o = o * lax.rsqrt(jnp.mean(o * o, axis=-1, keepdims=True) + EPS) * gn_ref[:, hv[h]]
            r = gr_ref[rows, hv[h]].astype(F32)
            o_ref[rows, hv[h]] = (o * (r * jax.nn.sigmoid(r))).astype(BF16)

    ctx = {}
    for t in range(pairs + 3):
        if t < pairs:
            ctx[t] = decay_stage(t)
        if 0 <= t - 1 < pairs:
            score_stage(ctx[t - 1])
        if 0 <= t - 2 < pairs:
            output_stage(ctx[t - 2])
        if 0 <= t - 3 < pairs:
            norm_stage(ctx.pop(t - 3))

    for h in heads:
        s_ref[hk[h], :] = state[h]


def _gla(gq, kt, gv, gr, glog, glogt, gn, *, batch, seq, tg):
    T = batch * seq
    spb = seq // tg
    row = lambda n: pl.BlockSpec((tg, n), lambda b, i: (b * spb + i, 0))
    colt = pl.BlockSpec((1, GLA_QK, tg), lambda b, i: (b, 0, i))
    return pl.pallas_call(
        functools.partial(_gla_kernel, pairs=tg // (2 * GLA_CHUNK)),
        grid=(batch, spb),
        in_specs=[row(GLA_QK), colt, row(GLA_WIDTH), row(GLA_WIDTH), row(GLA_QK), colt,
                  pl.BlockSpec((1, GLA_WIDTH), lambda b, i: (0, 0))],
        out_specs=row(GLA_WIDTH),
        out_shape=jax.ShapeDtypeStruct((T, GLA_WIDTH), BF16),
        scratch_shapes=[pltpu.VMEM((GLA_QK, GLA_DV), F32)],
        compiler_params=pltpu.CompilerParams(dimension_semantics=("arbitrary", "arbitrary"),
                                             vmem_limit_bytes=VMEM_LIMIT),
        name="gla",
    )(gq, kt, gv, gr, glog, glogt, gn)


def _dsa_kernel(q_ref, k_ref, v_ref, kh_ref, vh_ref, bias_ref, o_ref, st_ref, *, res, blocks, keys_transposed):
    L = DSA_BLOCK
    first_tile = pl.program_id(2) == 0
    lane = lax.broadcasted_iota(jnp.int32, (L, LANES), 1)
    first_half = lane < DSA_DH
    prev_cols = lax.broadcasted_iota(jnp.int32, (L, 2 * L), 1) < L

    st_tiles = {}
    pair_out = {}

    def score_stage(r, blk, h):
        rows = slice(blk * L, (blk + 1) * L)
        both = slice((blk - 1) * L, (blk + 1) * L)
        grp = slice((h // 2) * LANES, (h // 2 + 1) * LANES)
        if keys_transposed:
            k_prev = kh_ref[r, grp, :] if blk == 0 else k_ref[blk - 1, r, grp, :]
            kcat = jnp.concatenate([k_prev, k_ref[blk, r, grp, :]], axis=1)
        elif blk == 0:
            kcat = jnp.concatenate([kh_ref[r, :, grp], k_ref[r, rows, grp]], axis=0)
        else:
            kcat = k_ref[r, both, grp]
        own = first_half if h % 2 == 0 else ~first_half
        qm = jnp.where(own, q_ref[r, rows, grp], jnp.zeros((), BF16))
        return _dot(qm, kcat) if keys_transposed else _dot_nt(qm, kcat)

    def value_stage(r, blk, h, s):
        rows = slice(blk * L, (blk + 1) * L)
        grp = slice((h // 2) * LANES, (h // 2 + 1) * LANES)
        bias = bias_ref[h]
        if blk == 0:
            bias = jnp.where(prev_cols & first_tile, NEG, bias)
        s = s + bias
        m = jnp.max(s, axis=-1, keepdims=True)
        p = jnp.exp2(s - m)
        den = jnp.sum(p, axis=-1, keepdims=True)
        st_old = st_tiles.get((r, blk), jnp.zeros((L, LANES), F32))
        st_tiles[r, blk] = jnp.where(lane == h, m, jnp.where(lane == DSA_HEADS + h, den, st_old))
        if blk == 0:
            vcat = jnp.concatenate([vh_ref[r, :, grp], v_ref[r, rows, grp]], axis=0)
        else:
            vcat = v_ref[r, both_rows(blk), grp]
        out = _dot(p.astype(BF16), vcat)
        if h % 2 == 0:
            pair_out[r, blk] = out
        else:
            o_ref[r, rows, grp] = jnp.where(first_half, pair_out.pop((r, blk)), out).astype(BF16)
        if h == DSA_HEADS - 1:
            st_ref[r, rows, :] = st_tiles.pop((r, blk))

    def both_rows(blk):
        return slice((blk - 1) * L, (blk + 1) * L)

    items = [(r, blk, h) for r in range(res) for blk in range(blocks) for h in range(DSA_HEADS)]
    lag = 1
    pending = {}
    for t in range(len(items) + lag):
        if t < len(items):
            pending[t] = score_stage(*items[t])
        if t - lag >= 0:
            value_stage(*items[t - lag], pending.pop(t - lag))


def _dsa_branch(x, kt, bias, branch, *, rows_per_step):
    batch, d, n, _ = x.shape
    tr = min(rows_per_step, n)
    res = min(rows_per_step // tr, d)
    hb = tr // DSA_BLOCK
    W = DSA_WIDTH
    prev_blk = lambda i: jnp.maximum(i * hb - 1, 0)
    blk = lambda part: pl.BlockSpec((None, res, tr, W), lambda b, r, i: (b, r, i, part))
    halo = lambda part: pl.BlockSpec((None, res, DSA_BLOCK, W), lambda b, r, i: (b, r, prev_blk(i), part))
    if kt is None:
        v_part = 2
        k_arg, k_spec, kh_spec = x, blk(1), halo(1)
    else:
        v_part = 1
        k_arg = kt
        k_spec = pl.BlockSpec((None, hb, res, W, DSA_BLOCK), lambda b, r, i: (b, i, r, 0, 0))
        kh_spec = pl.BlockSpec((None, None, res, W, DSA_BLOCK), lambda b, r, i: (b, prev_blk(i), r, 0, 0))
    return pl.pallas_call(
        functools.partial(_dsa_kernel, res=res, blocks=hb, keys_transposed=kt is not None),
        grid=(batch, d // res, n // tr),
        in_specs=[blk(0), k_spec, blk(v_part), kh_spec, halo(v_part),
                  pl.BlockSpec((None, DSA_HEADS, DSA_BLOCK, 2 * DSA_BLOCK), lambda b, r, i: (branch, 0, 0, 0))],
        out_specs=[pl.BlockSpec((None, res, tr, W), lambda b, r, i: (b, r, i, 0)),
                   pl.BlockSpec((None, res, tr, LANES), lambda b, r, i: (b, r, i, 0))],
        out_shape=[jax.ShapeDtypeStruct((batch, d, n, W), BF16),
                   jax.ShapeDtypeStruct((batch, d, n, LANES), F32)],
        compiler_params=pltpu.CompilerParams(dimension_semantics=("arbitrary",) * 3,
                                             vmem_limit_bytes=VMEM_LIMIT),
        name=f"dsa_d{d}",
    )(x, k_arg, x, k_arg, x, bias)


def _out_kernel(x_ref, oa_ref, o1_ref, o4_ref, o16_ref, l1_ref, l4_ref, l16_ref,
                f1_ref, f4_ref, f16_ref, g1_ref, g4_ref, g16_ref,
                wo_ref, g2_ref, w1_ref, w2_ref, gf_ref, y_ref, osc_ref, lsc_ref, ob_ref, *, ff_chunk):
    tm = x_ref.shape[0]
    npair = DSA_HEADS // 2
    lane = lax.broadcasted_iota(jnp.int32, (tm, LANES), 1)
    first_half = lane < DSA_DH

    def combine(o1_ref, o4_ref, o16_ref, l1_ref, l4_ref, l16_ref):
        for j, (d, o_ref, l_ref) in enumerate(((4, o4_ref, l4_ref), (16, o16_ref, l16_ref))):
            for r in range(d):
                lsc_ref[j, pl.ds(r, tm // d, stride=d), :] = l_ref[r]
                for hp in range(npair):
                    osc_ref[j * npair + hp, pl.ds(r, tm // d, stride=d), :] = (
                        o_ref[r, :, hp * LANES:(hp + 1) * LANES].astype(F32))

        sts = (l1_ref[...], lsc_ref[0], lsc_ref[1])
        m = jnp.maximum(jnp.maximum(sts[0], sts[1]), sts[2])
        es = [jnp.exp2(st - m) for st in sts]
        total = sum(e * pltpu.roll(st, LANES - DSA_HEADS, axis=1) for e, st in zip(es, sts))
        inv = 1.0 / jnp.where(lane < DSA_HEADS, total, 1.0)
        ws = [e * inv for e in es]

        for hp in range(npair):
            grp = slice(hp * LANES, (hp + 1) * LANES)
            branch_o = (o1_ref[:, grp].astype(F32), osc_ref[hp], osc_ref[npair + hp])
            acc = jnp.zeros((tm, LANES), F32)
            for w, o in zip(ws, branch_o):
                wa = jnp.sum(jnp.where(lane == 2 * hp, w, 0.0), axis=-1, keepdims=True)
                wb = jnp.sum(jnp.where(lane == 2 * hp + 1, w, 0.0), axis=-1, keepdims=True)
                acc = acc + jnp.where(first_half, wa, wb) * o
            ob_ref[:, grp] = acc.astype(BF16)

    @pl.when(pl.program_id(0) == 0)
    def _():
        combine(f1_ref, f4_ref, f16_ref, g1_ref, g4_ref, g16_ref)

    mixed = _dot(oa_ref[...], wo_ref[:GLA_WIDTH, :]) + _dot(ob_ref[...], wo_ref[GLA_WIDTH:, :])
    h = x_ref[...] + mixed
    nm = (h * lax.rsqrt(jnp.mean(h * h, axis=-1, keepdims=True) + EPS) * g2_ref[...]).astype(BF16)
    ff = None
    for c in range(D_FF // ff_chunk):
        cols = slice(c * ff_chunk, (c + 1) * ff_chunk)
        a = jnp.maximum(_dot(nm, w1_ref[:, cols]), 0.0)
        d = _dot((a * a).astype(BF16), w2_ref[cols, :])
        ff = d if ff is None else ff + d
    h = h + ff
    y_ref[...] = h * lax.rsqrt(jnp.mean(h * h, axis=-1, keepdims=True) + EPS) * gf_ref[...]

    combine(o1_ref, o4_ref, o16_ref, l1_ref, l4_ref, l16_ref)


def _out(x2, oa, os_, ls, wo, g2, w1, w2, gf, *, seq, tm, ff_chunk):
    T = x2.shape[0]
    spb = seq // tm
    nt = T // tm
    nxt = lambda s: jnp.minimum(s + 1, nt - 1)
    one = pl.Buffered(1)
    row = lambda n: pl.BlockSpec((tm, n), lambda s: (s, 0))
    row_nxt = lambda n: pl.BlockSpec((tm, n), lambda s: (nxt(s), 0))
    row_first = lambda n: pl.BlockSpec((tm, n), lambda s: (0, 0), pipeline_mode=one)
    strided = lambda d, n: pl.BlockSpec((None, d, tm // d, n), lambda s: (nxt(s) // spb, 0, nxt(s) % spb, 0))
    strided_first = lambda d, n: pl.BlockSpec((None, d, tm // d, n), lambda s: (0, 0, 0, 0), pipeline_mode=one)
    const = lambda a: pl.BlockSpec(a.shape, lambda s: (0,) * a.ndim, pipeline_mode=one)
    W = DSA_WIDTH
    return pl.pallas_call(
        functools.partial(_out_kernel, ff_chunk=ff_chunk),
        grid=(nt,),
        in_specs=[row(D_MODEL), row(GLA_WIDTH), row_nxt(W), strided(4, W), strided(16, W),
                  row_nxt(LANES), strided(4, LANES), strided(16, LANES),
                  row_first(W), strided_first(4, W), strided_first(16, W),
                  row_first(LANES), strided_first(4, LANES), strided_first(16, LANES),
                  const(wo), const(g2), const(w1), const(w2), const(gf)],
        out_specs=row(D_MODEL),
        out_shape=jax.ShapeDtypeStruct((T, D_MODEL), F32),
        scratch_shapes=[pltpu.VMEM((2 * (W // LANES), tm, LANES), F32),
                        pltpu.VMEM((2, tm, LANES), F32),
                        pltpu.VMEM((tm, W), BF16)],
        compiler_params=pltpu.CompilerParams(dimension_semantics=("arbitrary",),
                                             vmem_limit_bytes=VMEM_LIMIT),
        name="out_mlp",
    )(x2, oa, *os_, *ls, *os_, *ls, wo, g2, w1, w2, gf)


_IN_SPLITS = (GLA_QK, GLA_QK, GLA_WIDTH, GLA_WIDTH, GLA_RANK, DSA_WIDTH, DSA_WIDTH, DSA_WIDTH)
_IN_OFFSETS = tuple(int(v) for v in np.cumsum((0,) + _IN_SPLITS))


def _split_w_in_kernel(wt_ref, wq_ref, wk_ref, wv_ref, wr_ref, wg_ref, wd_ref):
    o = _IN_OFFSETS
    lane = lax.broadcasted_iota(jnp.int32, (wt_ref.shape[1], RANK_PAD), 1)

    def piece(start):
        return wt_ref[start:start + LANES, :].T

    for dst, start, width, scale in ((wq_ref, o[0], GLA_QK, GLA_DK ** -0.5), (wk_ref, o[1], GLA_QK, None),
                                     (wv_ref, o[2], GLA_WIDTH, None), (wr_ref, o[3], GLA_WIDTH, None),
                                     (wd_ref, o[5], 3 * DSA_WIDTH, None)):
        for j in range(width // LANES):
            blk = piece(start + j * LANES)
            if scale is not None:
                blk = blk * scale
            if dst is wd_ref and j < DSA_WIDTH // LANES:
                blk = blk * (DSA_DH ** -0.5 * LOG2E)
            dst[:, j * LANES:(j + 1) * LANES] = blk.astype(BF16)
    wg_ref[...] = jnp.where(lane < GLA_RANK, piece(o[4]), 0.0).astype(BF16)


def _split_w_in(wt):
    _, N, K = wt.shape
    shapes = [(K, GLA_QK), (K, GLA_QK), (K, GLA_WIDTH), (K, GLA_WIDTH), (K, RANK_PAD), (K, 3 * DSA_WIDTH)]
    return pl.pallas_call(
        _split_w_in_kernel,
        grid=(1,),
        in_specs=[pl.BlockSpec((None, N, K), lambda i: (0, 0, 0))],
        out_specs=[pl.BlockSpec(s, lambda i: (0, 0)) for s in shapes],
        out_shape=[jax.ShapeDtypeStruct(s, BF16) for s in shapes],
        compiler_params=pltpu.CompilerParams(vmem_limit_bytes=VMEM_LIMIT),
        name="split_w_in",
    )(wt)


def kernel(x, attn_norm_g, w_in, gla_gate_w2, gla_gate_b, gla_norm_g, rel_bias, w_out, mlp_norm_g,
           w_ff1, w_ff2, final_norm_g):
    batch, seq, _ = x.shape
    assert seq % (DSA_PATTERN[-1][1] * DSA_BLOCK) == 0
    T = batch * seq
    x2 = x.reshape(T, D_MODEL)

    wq, wk, wv, wr, wg, wd = _split_w_in(jnp.swapaxes(w_in, 1, 2))
    w2 = jnp.pad(gla_gate_w2[0], ((0, RANK_PAD - GLA_RANK), (0, 0))).astype(BF16)
    gb = gla_gate_b[0].astype(F32).reshape(1, GLA_QK)

    gq, gv, gr, d1, kt, glog, glogt, d4, d16, kt1, kt4 = _proj(
        x2, attn_norm_g[0].reshape(1, D_MODEL).astype(F32), wq, wv, wr, wk, wg, w2, gb, wd,
        batch=batch, seq=seq, tm=1024)

    o_a = _gla(gq, kt, gv, gr, glog, glogt, gla_norm_g[0].reshape(1, GLA_WIDTH).astype(F32),
               batch=batch, seq=seq, tg=2048)

    bias = _bias_tables(rel_bias)
    os_, ls = [], []
    branches = ((d1.reshape(batch, 1, seq, 2 * DSA_WIDTH), kt1), (d4, kt4), (d16, None))
    for branch, (xd, ktd) in enumerate(branches):
        o, lse = _dsa_branch(xd, ktd, bias, branch, rows_per_step=2048)
        os_.append(o)
        ls.append(lse)
    os_[0] = os_[0].reshape(T, DSA_WIDTH)
    ls[0] = ls[0].reshape(T, LANES)

    y = _out(x2, o_a, os_, ls, w_out[0].astype(BF16), mlp_norm_g[0].reshape(1, D_MODEL).astype(F32),
             w_ff1[0].astype(BF16), w_ff2[0].astype(BF16), final_norm_g.reshape(1, D_MODEL).astype(F32),
             seq=seq, tm=512, ff_chunk=1024)
    return y.reshape(batch, seq, D_MODEL)
```

```python
import functools
import math

import numpy as np
import jax
import jax.numpy as jnp
from jax import lax
from jax.experimental import pallas as pl
from jax.experimental.pallas import tpu as pltpu

D_MODEL = 1024
GLA_WIDTH = 512
GLA_HEADS = 4
GLA_DK = 64
GLA_DV = 128
GLA_QK = GLA_HEADS * GLA_DK
GLA_RANK = 16
GLA_TAU = 16.0
GLA_CHUNK = 64
DSA_WIDTH = 512
DSA_HEADS = 8
DSA_DH = 64
DSA_PATTERN = ((128, 1), (512, 4), (2048, 16))
DSA_BLOCK = 128
REL_BUCKETS = 32
REL_MAX_DIST = 2048
D_FF = 4096
EPS = 1e-6
NEG = -1e30
LOG2E = math.log2(math.e)

LANES = 128
RANK_PAD = LANES
VMEM_LIMIT = 56 * 1024 * 1024

F32 = jnp.float32
BF16 = jnp.bfloat16

_NT = (((1,), (1,)), ((), ()))


def _dot(a, b):
    return jnp.dot(a, b, preferred_element_type=F32)


def _dot_nt(a, b):
    return lax.dot_general(a, b, _NT, preferred_element_type=F32)


def _split_hi_lo(x):
    hi = x.astype(BF16)
    lo = (x - hi.astype(F32)).astype(BF16)
    return hi, lo


def _log_sigmoid(x):
    return jnp.minimum(x, 0.0) - jnp.log1p(jnp.exp(-jnp.abs(x)))


def _bucket_tables():
    max_exact = REL_BUCKETS // 2
    L = DSA_BLOCK
    steps = L + np.arange(L)[:, None] - np.arange(2 * L)[None, :]
    tables = []
    for window, dilation in DSA_PATTERN:
        span = window // dilation
        in_band = (steps >= 0) & (steps <= span)
        n = np.maximum(steps * dilation, 0)
        large = max_exact + (np.log(np.maximum(n, 1) / max_exact)
                             / math.log(REL_MAX_DIST / max_exact)
                             * (REL_BUCKETS - max_exact)).astype(np.int32)
        large = np.minimum(large, REL_BUCKETS - 1)
        bucket = np.where(n < max_exact, n, large).astype(np.int32)
        tables.append(np.where(in_band, bucket, -1).astype(np.int32))
    return np.stack(tables)


def _bias_kernel(rb_ref, bt_ref, out_ref):
    for d in range(len(DSA_PATTERN)):
        bt = bt_ref[d]

        def one_head(h, carry, bt=bt, d=d):
            acc = jnp.full(bt.shape, NEG, F32)
            for b in range(REL_BUCKETS):
                acc = jnp.where(bt == b, rb_ref[b, h] * LOG2E, acc)
            out_ref[d, h] = acc
            return carry

        lax.fori_loop(0, DSA_HEADS, one_head, 0)


def _bias_tables(rel_bias):
    nb = len(DSA_PATTERN)
    L = DSA_BLOCK
    return pl.pallas_call(
        _bias_kernel,
        grid=(1,),
        in_specs=[pl.BlockSpec(memory_space=pltpu.SMEM),
                  pl.BlockSpec((nb, L, 2 * L), lambda i: (0, 0, 0))],
        out_specs=pl.BlockSpec((nb, DSA_HEADS, L, 2 * L), lambda i: (0, 0, 0, 0)),
        out_shape=jax.ShapeDtypeStruct((nb, DSA_HEADS, L, 2 * L), F32),
        name="dsa_bias",
    )(rel_bias.astype(F32), jnp.asarray(_bucket_tables()))


def _proj_kernel(x_ref, g_ref, wq_ref, wv_ref, wr_ref, wk_ref, wg_ref, w2_ref, b_ref, wd_ref,
                 gq_ref, gv_ref, gr_ref, d1_ref, kt_ref, glog_ref, glogt_ref, d4_ref, d16_ref,
                 kt1_ref, kt4_ref, dsc_ref, t4_ref):
    tm = x_ref.shape[0]
    ncol = 3 * DSA_WIDTH // LANES
    kcols = range(DSA_WIDTH // LANES, 2 * DSA_WIDTH // LANES)
    n4 = tm // 4
    n16 = tm // 16

    x = x_ref[...]
    ms = jnp.mean(x * x, axis=-1, keepdims=True)
    nx = (x * lax.rsqrt(ms + EPS) * g_ref[...]).astype(BF16)

    for j in range(ncol // 2):
        res = _dot(nx, wd_ref[:, 2 * j * LANES:(2 * j + 2) * LANES])
        halves = {2 * j: res[:, :LANES], 2 * j + 1: res[:, LANES:]}
        for c in halves:
            dsc_ref[c % 2] = halves[c]
        for c in halves:
            sl = c % 2
            cols = slice(c * LANES, (c + 1) * LANES)
            is_key = c in kcols
            krows = slice((c - kcols[0]) * LANES, (c - kcols[0] + 1) * LANES)
            qv = c if c < kcols[0] else c - len(kcols)
            qv_cols = slice(qv * LANES, (qv + 1) * LANES)
            if is_key:
                for jb in range(tm // DSA_BLOCK):
                    blk_rows = slice(jb * DSA_BLOCK, (jb + 1) * DSA_BLOCK)
                    kt1_ref[jb, krows, :] = dsc_ref[sl, blk_rows, :].T.astype(BF16)
            else:
                d1_ref[:, qv_cols] = halves[c].astype(BF16)
            for r4 in range(4):
                sub = dsc_ref[sl, pl.ds(r4, n4, stride=4), :]
                t4_ref[sl, r4 * n4:(r4 + 1) * n4, :] = sub
                if is_key:
                    for jb in range(n4 // DSA_BLOCK):
                        blk_rows = slice(jb * DSA_BLOCK, (jb + 1) * DSA_BLOCK)
                        kt4_ref[jb, r4, krows, :] = sub[blk_rows, :].T.astype(BF16)
                else:
                    d4_ref[r4, :, qv_cols] = sub.astype(BF16)
            for r4 in range(4):
                for r2 in range(4):
                    d16_ref[r4 + 4 * r2, :, cols] = (
                        t4_ref[sl, pl.ds(r4 * n4 + r2, n16, stride=4), :].astype(BF16))

    gq_ref[...] = _dot(nx, wq_ref[...]).astype(BF16)

    kt_ref[0] = _dot(nx, wk_ref[...]).T.astype(BF16)
    glow = _dot(nx, wg_ref[...]).astype(BF16)
    glog = _log_sigmoid(_dot(glow, w2_ref[...]) + b_ref[...]) * (LOG2E / GLA_TAU)
    glog_ref[...] = glog
    glogt_ref[0] = glog.T

    gv_ref[...] = _dot(nx, wv_ref[...]).astype(BF16)
    gr_ref[...] = _dot(nx, wr_ref[...]).astype(BF16)


def _proj(x2, g, wq, wv, wr, wk, wg, w2, b, wd, *, batch, seq, tm):
    T = batch * seq
    spb = seq // tm
    nt = T // tm
    full = lambda a: pl.BlockSpec(a.shape, lambda s: (0,) * a.ndim, pipeline_mode=pl.Buffered(1))
    row_cur = lambda n: pl.BlockSpec((tm, n), lambda s: (s, 0))
    row_prev = row_cur
    colt = pl.BlockSpec((1, GLA_QK, tm), lambda s: (s // spb, 0, s % spb))
    W, L = DSA_WIDTH, DSA_BLOCK
    W2, W3 = 2 * W, 3 * W
    strided = lambda d, n: pl.BlockSpec((None, d, tm // d, n), lambda s: (s // spb, 0, s % spb, 0))
    assert tm % (4 * L) == 0
    kt1_spec = pl.BlockSpec((None, tm // L, None, W, L), lambda s: (s // spb, s % spb, 0, 0, 0))
    kt4_spec = pl.BlockSpec((None, tm // (4 * L), 4, W, L), lambda s: (s // spb, s % spb, 0, 0, 0))
    return pl.pallas_call(
        _proj_kernel,
        grid=(nt,),
        in_specs=[row_cur(D_MODEL)] + [full(a) for a in (g, wq, wv, wr, wk, wg, w2, b, wd)],
        out_specs=[row_cur(GLA_QK), row_cur(GLA_WIDTH), row_cur(GLA_WIDTH), row_cur(W2),
                   colt, row_prev(GLA_QK), colt, strided(4, W2), strided(16, W3), kt1_spec, kt4_spec],
        out_shape=[jax.ShapeDtypeStruct((T, GLA_QK), BF16),
                   jax.ShapeDtypeStruct((T, GLA_WIDTH), BF16),
                   jax.ShapeDtypeStruct((T, GLA_WIDTH), BF16),
                   jax.ShapeDtypeStruct((T, W2), BF16),
                   jax.ShapeDtypeStruct((batch, GLA_QK, seq), BF16),
                   jax.ShapeDtypeStruct((T, GLA_QK), F32),
                   jax.ShapeDtypeStruct((batch, GLA_QK, seq), F32),
                   jax.ShapeDtypeStruct((batch, 4, seq // 4, W2), BF16),
                   jax.ShapeDtypeStruct((batch, 16, seq // 16, W3), BF16),
                   jax.ShapeDtypeStruct((batch, seq // L, 1, W, L), BF16),
                   jax.ShapeDtypeStruct((batch, seq // (4 * L), 4, W, L), BF16)],
        scratch_shapes=[pltpu.VMEM((2, tm, LANES), F32)] * 2,
        compiler_params=pltpu.CompilerParams(dimension_semantics=("arbitrary",),
                                             vmem_limit_bytes=VMEM_LIMIT),
        name="proj",
    )(x2, g, wq, wv, wr, wk, wg, w2, b, wd)


def _gla_kernel(gq_ref, kt_ref, gv_ref, gr_ref, glog_ref, glogt_ref, gn_ref, o_ref, s_ref, *, pairs):
    C = GLA_CHUNK
    P = 2 * C

    @pl.when(pl.program_id(1) == 0)
    def _():
        s_ref[...] = jnp.zeros_like(s_ref)

    ri = lax.broadcasted_iota(jnp.int32, (P, P), 0)
    ci = lax.broadcasted_iota(jnp.int32, (P, P), 1)
    same_chunk = (ri < C) == (ci < C)
    causal = same_chunk & (ci <= ri)
    low = jnp.where(causal, 1.0, 0.0).astype(BF16)
    upp = jnp.where(same_chunk & (ri <= ci), 1.0, 0.0).astype(BF16)
    lane = lax.broadcasted_iota(jnp.int32, (P, LANES), 1)
    first_half = lane < C
    lane_t = lax.broadcasted_iota(jnp.int32, (GLA_QK, P), 1)
    first_t = lane_t < C

    zero16 = jnp.zeros((), BF16)
    state = [s_ref[h * GLA_DK:(h + 1) * GLA_DK, :] for h in range(GLA_HEADS)]

    heads = range(GLA_HEADS)
    hk = [slice(h * GLA_DK, (h + 1) * GLA_DK) for h in heads]
    hv = [slice(h * GLA_DV, (h + 1) * GLA_DV) for h in heads]
    grp = [slice((h // 2) * LANES, (h // 2 + 1) * LANES) for h in heads]

    def decay_stage(p):
        rows = slice(p * P, (p + 1) * P)
        g_hi, g_lo = _split_hi_lo(glog_ref[rows, :])
        b = _dot(low, g_hi) + _dot(low, g_lo)
        gt = glogt_ref[0, :, rows]
        gt_hi, gt_lo = _split_hi_lo(gt)
        bt = _dot(gt_hi, upp) + _dot(gt_lo, upp)
        tot_a = jnp.sum(jnp.where(first_t, gt, 0.0), axis=-1, keepdims=True)
        tot_b = jnp.sum(jnp.where(first_t, 0.0, gt), axis=-1, keepdims=True)
        return dict(rows=rows, b=b, bt=bt, dec_a=jnp.exp2(tot_a), dec_b=jnp.exp2(tot_b))

    def score_stage(c):
        rows = c["rows"]
        qd = (gq_ref[rows, :].astype(F32) * jnp.exp2(c["b"])).astype(BF16)
        kinv_t = (kt_ref[0, :, rows].astype(F32) * jnp.exp2(-c["bt"])).astype(BF16)
        kinv_a = jnp.where(first_t, kinv_t, zero16)
        kinv_b = jnp.where(first_t, zero16, kinv_t)
        c["v"] = [gv_ref[rows, hv[h]] for h in heads]
        c["qm"] = [jnp.where(first_half if h % 2 == 0 else ~first_half, qd[:, grp[h]], zero16) for h in heads]
        c["att"] = [_dot(c["qm"][h], kinv_t[grp[h], :]) for h in heads]
        c["upd_a"] = [_dot(kinv_a[hk[h], :], c["v"][h]) for h in heads]
        c["upd_b"] = [_dot(kinv_b[hk[h], :], c["v"][h]) for h in heads]

    def output_stage(c):
        o = []
        s_a = list(state)
        s_b = [c["dec_a"][hk[h], :] * (s_a[h] + c["upd_a"][h]) for h in heads]
        for h in heads:
            state[h] = c["dec_b"][hk[h], :] * (s_b[h] + c["upd_b"][h])
        for h in heads:
            att = jnp.where(causal, c["att"][h], 0.0).astype(BF16)
            pair_a = [s_a[h], s_a[h ^ 1]] if h % 2 == 0 else [s_a[h ^ 1], s_a[h]]
            pair_b = [s_b[h], s_a[h ^ 1]] if h % 2 == 0 else [s_a[h ^ 1], s_b[h]]
            o_a = _dot(c["qm"][h][:C, :], jnp.concatenate(pair_a, axis=0).astype(BF16))
            o_b = _dot(c["qm"][h][C:, :], jnp.concatenate(pair_b, axis=0).astype(BF16))
            o.append(_dot(att, c["v"][h]) + jnp.concatenate([o_a, o_b], axis=0))
        c["o"] = o

    def norm_stage(c):
        rows = c["rows"]
        for h in heads:
            o = c["o"][h]
            o = o * lax.rsqrt(jnp.mean(o * o, axis=-1, keepdims=True) + EPS) * gn_ref[:, hv[h]]
            r = gr_ref[rows, hv[h]].astype(F32)
            o_ref[rows, hv[h]] = (o * (r * jax.nn.sigmoid(r))).astype(BF16)

    ctx = {}
    for t in range(pairs + 3):
        if t < pairs:
            ctx[t] = decay_stage(t)
        if 0 <= t - 1 < pairs:
            score_stage(ctx[t - 1])
        if 0 <= t - 2 < pairs:
            output_stage(ctx[t - 2])
        if 0 <= t - 3 < pairs:
            norm_stage(ctx.pop(t - 3))

    for h in heads:
        s_ref[hk[h], :] = state[h]


def _gla(gq, kt, gv, gr, glog, glogt, gn, *, batch, seq, tg):
    T = batch * seq
    spb = seq // tg
    row = lambda n: pl.BlockSpec((tg, n), lambda b, i: (b * spb + i, 0))
    colt = pl.BlockSpec((1, GLA_QK, tg), lambda b, i: (b, 0, i))
    return pl.pallas_call(
        functools.partial(_gla_kernel, pairs=tg // (2 * GLA_CHUNK)),
        grid=(batch, spb),
        in_specs=[row(GLA_QK), colt, row(GLA_WIDTH), row(GLA_WIDTH), row(GLA_QK), colt,
                  pl.BlockSpec((1, GLA_WIDTH), lambda b, i: (0, 0))],
        out_specs=row(GLA_WIDTH),
        out_shape=jax.ShapeDtypeStruct((T, GLA_WIDTH), BF16),
        scratch_shapes=[pltpu.VMEM((GLA_QK, GLA_DV), F32)],
        compiler_params=pltpu.CompilerParams(dimension_semantics=("arbitrary", "arbitrary"),
                                             vmem_limit_bytes=VMEM_LIMIT),
        name="gla",
    )(gq, kt, gv, gr, glog, glogt, gn)


def _dsa_kernel(q_ref, k_ref, v_ref, kh_ref, vh_ref, bias_ref, o_ref, st_ref, *, res, blocks, keys_transposed):
    L = DSA_BLOCK
    first_tile = pl.program_id(2) == 0
    lane = lax.broadcasted_iota(jnp.int32, (L, LANES), 1)
    first_half = lane < DSA_DH
    prev_cols = lax.broadcasted_iota(jnp.int32, (L, 2 * L), 1) < L

    st_tiles = {}
    pair_out = {}

    def score_stage(r, blk, h):
        rows = slice(blk * L, (blk + 1) * L)
        both = slice((blk - 1) * L, (blk + 1) * L)
        grp = slice((h // 2) * LANES, (h // 2 + 1) * LANES)
        if keys_transposed:
            k_prev = kh_ref[r, grp, :] if blk == 0 else k_ref[blk - 1, r, grp, :]
            kcat = jnp.concatenate([k_prev, k_ref[blk, r, grp, :]], axis=1)
        elif blk == 0:
            kcat = jnp.concatenate([kh_ref[r, :, grp], k_ref[r, rows, grp]], axis=0)
        else:
            kcat = k_ref[r, both, grp]
        own = first_half if h % 2 == 0 else ~first_half
        qm = jnp.where(own, q_ref[r, rows, grp], jnp.zeros((), BF16))
        return _dot(qm, kcat) if keys_transposed else _dot_nt(qm, kcat)

    def value_stage(r, blk, h, s):
        rows = slice(blk * L, (blk + 1) * L)
        grp = slice((h // 2) * LANES, (h // 2 + 1) * LANES)
        bias = bias_ref[h]
        if blk == 0:
            bias = jnp.where(prev_cols & first_tile, NEG, bias)
        s = s + bias
        m = jnp.max(s, axis=-1, keepdims=True)
        p = jnp.exp2(s - m)
        den = jnp.sum(p, axis=-1, keepdims=True)
        st_old = st_tiles.get((r, blk), jnp.zeros((L, LANES), F32))
        st_tiles[r, blk] = jnp.where(lane == h, m, jnp.where(lane == DSA_HEADS + h, den, st_old))
        if blk == 0:
            vcat = jnp.concatenate([vh_ref[r, :, grp], v_ref[r, rows, grp]], axis=0)
        else:
            vcat = v_ref[r, both_rows(blk), grp]
        out = _dot(p.astype(BF16), vcat)
        if h % 2 == 0:
            pair_out[r, blk] = out
        else:
            o_ref[r, rows, grp] = jnp.where(first_half, pair_out.pop((r, blk)), out).astype(BF16)
        if h == DSA_HEADS - 1:
            st_ref[r, rows, :] = st_tiles.pop((r, blk))

    def both_rows(blk):
        return slice((blk - 1) * L, (blk + 1) * L)

    items = [(r, blk, h) for r in range(res) for blk in range(blocks) for h in range(DSA_HEADS)]
    lag = 1
    pending = {}
    for t in range(len(items) + lag):
        if t < len(items):
            pending[t] = score_stage(*items[t])
        if t - lag >= 0:
            value_stage(*items[t - lag], pending.pop(t - lag))


def _dsa_branch(x, kt, bias, branch, *, rows_per_step):
    batch, d, n, _ = x.shape
    tr = min(rows_per_step, n)
    res = min(rows_per_step // tr, d)
    hb = tr // DSA_BLOCK
    W = DSA_WIDTH
    prev_blk = lambda i: jnp.maximum(i * hb - 1, 0)
    blk = lambda part: pl.BlockSpec((None, res, tr, W), lambda b, r, i: (b, r, i, part))
    halo = lambda part: pl.BlockSpec((None, res, DSA_BLOCK, W), lambda b, r, i: (b, r, prev_blk(i), part))
    if kt is None:
        v_part = 2
        k_arg, k_spec, kh_spec = x, blk(1), halo(1)
    else:
        v_part = 1
        k_arg = kt
        k_spec = pl.BlockSpec((None, hb, res, W, DSA_BLOCK), lambda b, r, i: (b, i, r, 0, 0))
        kh_spec = pl.BlockSpec((None, None, res, W, DSA_BLOCK), lambda b, r, i: (b, prev_blk(i), r, 0, 0))
    return pl.pallas_call(
        functools.partial(_dsa_kernel, res=res, blocks=hb, keys_transposed=kt is not None),
        grid=(batch, d // res, n // tr),
        in_specs=[blk(0), k_spec, blk(v_part), kh_spec, halo(v_part),
                  pl.BlockSpec((None, DSA_HEADS, DSA_BLOCK, 2 * DSA_BLOCK), lambda b, r, i: (branch, 0, 0, 0))],
        out_specs=[pl.BlockSpec((None, res, tr, W), lambda b, r, i: (b, r, i, 0)),
                   pl.BlockSpec((None, res, tr, LANES), lambda b, r, i: (b, r, i, 0))],
        out_shape=[jax.ShapeDtypeStruct((batch, d, n, W), BF16),
                   jax.ShapeDtypeStruct((batch, d, n, LANES), F32)],
        compiler_params=pltpu.CompilerParams(dimension_semantics=("arbitrary",) * 3,
                                             vmem_limit_bytes=VMEM_LIMIT),
        name=f"dsa_d{d}",
    )(x, k_arg, x, k_arg, x, bias)


def _out_kernel(x_ref, oa_ref, o1_ref, o4_ref, o16_ref, l1_ref, l4_ref, l16_ref,
                f1_ref, f4_ref, f16_ref, g1_ref, g4_ref, g16_ref,
                wo_ref, g2_ref, w1_ref, w2_ref, gf_ref, y_ref, osc_ref, lsc_ref, ob_ref, *, ff_chunk):
    tm = x_ref.shape[0]
    npair = DSA_HEADS // 2
    lane = lax.broadcasted_iota(jnp.int32, (tm, LANES), 1)
    first_half = lane < DSA_DH

    def combine(o1_ref, o4_ref, o16_ref, l1_ref, l4_ref, l16_ref):
        for j, (d, o_ref, l_ref) in enumerate(((4, o4_ref, l4_ref), (16, o16_ref, l16_ref))):
            for r in range(d):
                lsc_ref[j, pl.ds(r, tm // d, stride=d), :] = l_ref[r]
                for hp in range(npair):
                    osc_ref[j * npair + hp, pl.ds(r, tm // d, stride=d), :] = (
                        o_ref[r, :, hp * LANES:(hp + 1) * LANES].astype(F32))

        sts = (l1_ref[...], lsc_ref[0], lsc_ref[1])
        m = jnp.maximum(jnp.maximum(sts[0], sts[1]), sts[2])
        es = [jnp.exp2(st - m) for st in sts]
        total = sum(e * pltpu.roll(st, LANES - DSA_HEADS, axis=1) for e, st in zip(es, sts))
        inv = 1.0 / jnp.where(lane < DSA_HEADS, total, 1.0)
        ws = [e * inv for e in es]

        for hp in range(npair):
            grp = slice(hp * LANES, (hp + 1) * LANES)
            branch_o = (o1_ref[:, grp].astype(F32), osc_ref[hp], osc_ref[npair + hp])
            acc = jnp.zeros((tm, LANES), F32)
            for w, o in zip(ws, branch_o):
                wa = jnp.sum(jnp.where(lane == 2 * hp, w, 0.0), axis=-1, keepdims=True)
                wb = jnp.sum(jnp.where(lane == 2 * hp + 1, w, 0.0), axis=-1, keepdims=True)
                acc = acc + jnp.where(first_half, wa, wb) * o
            ob_ref[:, grp] = acc.astype(BF16)

    @pl.when(pl.program_id(0) == 0)
    def _():
        combine(f1_ref, f4_ref, f16_ref, g1_ref, g4_ref, g16_ref)

    mixed = _dot(oa_ref[...], wo_ref[:GLA_WIDTH, :]) + _dot(ob_ref[...], wo_ref[GLA_WIDTH:, :])
    h = x_ref[...] + mixed
    nm = (h * lax.rsqrt(jnp.mean(h * h, axis=-1, keepdims=True) + EPS) * g2_ref[...]).astype(BF16)
    ff = None
    for c in range(D_FF // ff_chunk):
        cols = slice(c * ff_chunk, (c + 1) * ff_chunk)
        a = jnp.maximum(_dot(nm, w1_ref[:, cols]), 0.0)
        d = _dot((a * a).astype(BF16), w2_ref[cols, :])
        ff = d if ff is None else ff + d
    h = h + ff
    y_ref[...] = h * lax.rsqrt(jnp.mean(h * h, axis=-1, keepdims=True) + EPS) * gf_ref[...]

    combine(o1_ref, o4_ref, o16_ref, l1_ref, l4_ref, l16_ref)


def _out(x2, oa, os_, ls, wo, g2, w1, w2, gf, *, seq, tm, ff_chunk):
    T = x2.shape[0]
    spb = seq // tm
    nt = T // tm
    nxt = lambda s: jnp.minimum(s + 1, nt - 1)
    one = pl.Buffered(1)
    row = lambda n: pl.BlockSpec((tm, n), lambda s: (s, 0))
    row_nxt = lambda n: pl.BlockSpec((tm, n), lambda s: (nxt(s), 0))
    row_first = lambda n: pl.BlockSpec((tm, n), lambda s: (0, 0), pipeline_mode=one)
    strided = lambda d, n: pl.BlockSpec((None, d, tm // d, n), lambda s: (nxt(s) // spb, 0, nxt(s) % spb, 0))
    strided_first = lambda d, n: pl.BlockSpec((None, d, tm // d, n), lambda s: (0, 0, 0, 0), pipeline_mode=one)
    const = lambda a: pl.BlockSpec(a.shape, lambda s: (0,) * a.ndim, pipeline_mode=one)
    W = DSA_WIDTH
    return pl.pallas_call(
        functools.partial(_out_kernel, ff_chunk=ff_chunk),
        grid=(nt,),
        in_specs=[row(D_MODEL), row(GLA_WIDTH), row_nxt(W), strided(4, W), strided(16, W),
                  row_nxt(LANES), strided(4, LANES), strided(16, LANES),
                  row_first(W), strided_first(4, W), strided_first(16, W),
                  row_first(LANES), strided_first(4, LANES), strided_first(16, LANES),
                  const(wo), const(g2), const(w1), const(w2), const(gf)],
        out_specs=row(D_MODEL),
        out_shape=jax.ShapeDtypeStruct((T, D_MODEL), F32),
        scratch_shapes=[pltpu.VMEM((2 * (W // LANES), tm, LANES), F32),
                        pltpu.VMEM((2, tm, LANES), F32),
                        pltpu.VMEM((tm, W), BF16)],
        compiler_params=pltpu.CompilerParams(dimension_semantics=("arbitrary",),
                                             vmem_limit_bytes=VMEM_LIMIT),
        name="out_mlp",
    )(x2, oa, *os_, *ls, *os_, *ls, wo, g2, w1, w2, gf)


_IN_SPLITS = (GLA_QK, GLA_QK, GLA_WIDTH, GLA_WIDTH, GLA_RANK, DSA_WIDTH, DSA_WIDTH, DSA_WIDTH)
_IN_OFFSETS = tuple(int(v) for v in np.cumsum((0,) + _IN_SPLITS))


def _split_w_in_kernel(wt_ref, wq_ref, wk_ref, wv_ref, wr_ref, wg_ref, wd_ref):
    o = _IN_OFFSETS
    lane = lax.broadcasted_iota(jnp.int32, (wt_ref.shape[1], RANK_PAD), 1)

    def piece(start):
        return wt_ref[start:start + LANES, :].T

    for dst, start, width, scale in ((wq_ref, o[0], GLA_QK, GLA_DK ** -0.5), (wk_ref, o[1], GLA_QK, None),
                                     (wv_ref, o[2], GLA_WIDTH, None), (wr_ref, o[3], GLA_WIDTH, None),
                                     (wd_ref, o[5], 3 * DSA_WIDTH, None)):
        for j in range(width // LANES):
            blk = piece(start + j * LANES)
            if scale is not None:
                blk = blk * scale
            if dst is wd_ref and j < DSA_WIDTH // LANES:
                blk = blk * (DSA_DH ** -0.5 * LOG2E)
            dst[:, j * LANES:(j + 1) * LANES] = blk.astype(BF16)
    wg_ref[...] = jnp.where(lane < GLA_RANK, piece(o[4]), 0.0).astype(BF16)


def _split_w_in(wt):
    _, N, K = wt.shape
    shapes = [(K, GLA_QK), (K, GLA_QK), (K, GLA_WIDTH), (K, GLA_WIDTH), (K, RANK_PAD), (K, 3 * DSA_WIDTH)]
    return pl.pallas_call(
        _split_w_in_kernel,
        grid=(1,),
        in_specs=[pl.BlockSpec((None, N, K), lambda i: (0, 0, 0))],
        out_specs=[pl.BlockSpec(s, lambda i: (0, 0)) for s in shapes],
        out_shape=[jax.ShapeDtypeStruct(s, BF16) for s in shapes],
        compiler_params=pltpu.CompilerParams(vmem_limit_bytes=VMEM_LIMIT),
        name="split_w_in",
    )(wt)


def kernel(x, attn_norm_g, w_in, gla_gate_w2, gla_gate_b, gla_norm_g, rel_bias, w_out, mlp_norm_g,
           w_ff1, w_ff2, final_norm_g):
    batch, seq, _ = x.shape
    assert seq % (DSA_PATTERN[-1][1] * DSA_BLOCK) == 0
    T = batch * seq
    x2 = x.reshape(T, D_MODEL)

    wq, wk, wv, wr, wg, wd = _split_w_in(jnp.swapaxes(w_in, 1, 2))
    w2 = jnp.pad(gla_gate_w2[0], ((0, RANK_PAD - GLA_RANK), (0, 0))).astype(BF16)
    gb = gla_gate_b[0].astype(F32).reshape(1, GLA_QK)

    gq, gv, gr, d1, kt, glog, glogt, d4, d16, kt1, kt4 = _proj(
        x2, attn_norm_g[0].reshape(1, D_MODEL).astype(F32), wq, wv, wr, wk, wg, w2, gb, wd,
        batch=batch, seq=seq, tm=1024)

    o_a = _gla(gq, kt, gv, gr, glog, glogt, gla_norm_g[0].reshape(1, GLA_WIDTH).astype(F32),
               batch=batch, seq=seq, tg=2048)

    bias = _bias_tables(rel_bias)
    os_, ls = [], []
    branches = ((d1.reshape(batch, 1, seq, 2 * DSA_WIDTH), kt1), (d4, kt4), (d16, None))
    for branch, (xd, ktd) in enumerate(branches):
        o, lse = _dsa_branch(xd, ktd, bias, branch, rows_per_step=2048)
        os_.append(o)
        ls.append(lse)
    os_[0] = os_[0].reshape(T, DSA_WIDTH)
    ls[0] = ls[0].reshape(T, LANES)

    y = _out(x2, o_a, os_, ls, w_out[0].astype(BF16), mlp_norm_g[0].reshape(1, D_MODEL).astype(F32),
             w_ff1[0].astype(BF16), w_ff2[0].astype(BF16), final_norm_g.reshape(1, D_MODEL).astype(F32),
             seq=seq, tm=512, ff_chunk=2048)
    return y.reshape(batch, seq, D_MODEL)
```

```python
import functools
import math

import numpy as np
import jax
import jax.numpy as jnp
from jax import lax
from jax.experimental import pallas as pl
from jax.experimental.pallas import tpu as pltpu

D_MODEL = 1024
GLA_WIDTH = 512
GLA_HEADS = 4
GLA_DK = 64
GLA_DV = 128
GLA_QK = GLA_HEADS * GLA_DK
GLA_RANK = 16
GLA_TAU = 16.0
GLA_CHUNK = 64
DSA_WIDTH = 512
DSA_HEADS = 8
DSA_DH = 64
DSA_PATTERN = ((128, 1), (512, 4), (2048, 16))
DSA_BLOCK = 128
REL_BUCKETS = 32
REL_MAX_DIST = 2048
D_FF = 4096
EPS = 1e-6
NEG = -1e30
LOG2E = math.log2(math.e)

LANES = 128
RANK_PAD = LANES
VMEM_LIMIT = 56 * 1024 * 1024

F32 = jnp.float32
BF16 = jnp.bfloat16

_NT = (((1,), (1,)), ((), ()))


def _dot(a, b):
    return jnp.dot(a, b, preferred_element_type=F32)


def _dot_nt(a, b):
    return lax.dot_general(a, b, _NT, preferred_element_type=F32)


def _split_hi_lo(x):
    hi = x.astype(BF16)
    lo = (x - hi.astype(F32)).astype(BF16)
    return hi, lo


def _log_sigmoid(x):
    return jnp.minimum(x, 0.0) - jnp.log1p(jnp.exp(-jnp.abs(x)))


def _bucket_tables():
    max_exact = REL_BUCKETS // 2
    L = DSA_BLOCK
    steps = L + np.arange(L)[:, None] - np.arange(2 * L)[None, :]
    tables = []
    for window, dilation in DSA_PATTERN:
        span = window // dilation
        in_band = (steps >= 0) & (steps <= span)
        n = np.maximum(steps * dilation, 0)
        large = max_exact + (np.log(np.maximum(n, 1) / max_exact)
                             / math.log(REL_MAX_DIST / max_exact)
                             * (REL_BUCKETS - max_exact)).astype(np.int32)
        large = np.minimum(large, REL_BUCKETS - 1)
        bucket = np.where(n < max_exact, n, large).astype(np.int32)
        tables.append(np.where(in_band, bucket, -1).astype(np.int32))
    return np.stack(tables)


def _bias_kernel(rb_ref, bt_ref, out_ref):
    for d in range(len(DSA_PATTERN)):
        bt = bt_ref[d]

        def one_head(h, carry, bt=bt, d=d):
            acc = jnp.full(bt.shape, NEG, F32)
            for b in range(REL_BUCKETS):
                acc = jnp.where(bt == b, rb_ref[b, h] * LOG2E, acc)
            out_ref[d, h] = acc
            return carry

        lax.fori_loop(0, DSA_HEADS, one_head, 0)


def _bias_tables(rel_bias):
    nb = len(DSA_PATTERN)
    L = DSA_BLOCK
    return pl.pallas_call(
        _bias_kernel,
        grid=(1,),
        in_specs=[pl.BlockSpec(memory_space=pltpu.SMEM),
                  pl.BlockSpec((nb, L, 2 * L), lambda i: (0, 0, 0))],
        out_specs=pl.BlockSpec((nb, DSA_HEADS, L, 2 * L), lambda i: (0, 0, 0, 0)),
        out_shape=jax.ShapeDtypeStruct((nb, DSA_HEADS, L, 2 * L), F32),
        name="dsa_bias",
    )(rel_bias.astype(F32), jnp.asarray(_bucket_tables()))


def _proj_kernel(x_ref, g_ref, wq_ref, wv_ref, wr_ref, wk_ref, wg_ref, w2_ref, b_ref, wd_ref,
                 gq_ref, gv_ref, gr_ref, d1_ref, kt_ref, glog_ref, glogt_ref, d4_ref, d16_ref,
                 kt1_ref, kt4_ref, dsc_ref, t4_ref):
    tm = x_ref.shape[0]
    ncol = 3 * DSA_WIDTH // LANES
    kcols = range(DSA_WIDTH // LANES, 2 * DSA_WIDTH // LANES)
    n4 = tm // 4
    n16 = tm // 16

    x = x_ref[...]
    ms = jnp.mean(x * x, axis=-1, keepdims=True)
    nx = (x * lax.rsqrt(ms + EPS) * g_ref[...]).astype(BF16)

    for j in range(ncol // 2):
        res = _dot(nx, wd_ref[:, 2 * j * LANES:(2 * j + 2) * LANES])
        halves = {2 * j: res[:, :LANES], 2 * j + 1: res[:, LANES:]}
        for c in halves:
            dsc_ref[c % 2] = halves[c]
        for c in halves:
            sl = c % 2
            cols = slice(c * LANES, (c + 1) * LANES)
            is_key = c in kcols
            krows = slice((c - kcols[0]) * LANES, (c - kcols[0] + 1) * LANES)
            qv = c if c < kcols[0] else c - len(kcols)
            qv_cols = slice(qv * LANES, (qv + 1) * LANES)
            if is_key:
                for jb in range(tm // DSA_BLOCK):
                    blk_rows = slice(jb * DSA_BLOCK, (jb + 1) * DSA_BLOCK)
                    kt1_ref[jb, krows, :] = dsc_ref[sl, blk_rows, :].T.astype(BF16)
            else:
                d1_ref[:, qv_cols] = halves[c].astype(BF16)
            for r4 in range(4):
                sub = dsc_ref[sl, pl.ds(r4, n4, stride=4), :]
                t4_ref[sl, r4 * n4:(r4 + 1) * n4, :] = sub
                if is_key:
                    for jb in range(n4 // DSA_BLOCK):
                        blk_rows = slice(jb * DSA_BLOCK, (jb + 1) * DSA_BLOCK)
                        kt4_ref[jb, r4, krows, :] = sub[blk_rows, :].T.astype(BF16)
                else:
                    d4_ref[r4, :, qv_cols] = sub.astype(BF16)
            for r4 in range(4):
                for r2 in range(4):
                    d16_ref[r4 + 4 * r2, :, cols] = (
                        t4_ref[sl, pl.ds(r4 * n4 + r2, n16, stride=4), :].astype(BF16))

    gq_ref[...] = _dot(nx, wq_ref[...]).astype(BF16)

    kt_ref[0] = _dot(nx, wk_ref[...]).T.astype(BF16)
    glow = _dot(nx, wg_ref[...]).astype(BF16)
    glog = _log_sigmoid(_dot(glow, w2_ref[...]) + b_ref[...]) * (LOG2E / GLA_TAU)
    glog_ref[...] = glog
    glogt_ref[0] = glog.T

    gv_ref[...] = _dot(nx, wv_ref[...]).astype(BF16)
    gr_ref[...] = _dot(nx, wr_ref[...]).astype(BF16)


def _proj(x2, g, wq, wv, wr, wk, wg, w2, b, wd, *, batch, seq, tm):
    T = batch * seq
    spb = seq // tm
    nt = T // tm
    full = lambda a: pl.BlockSpec(a.shape, lambda s: (0,) * a.ndim, pipeline_mode=pl.Buffered(1))
    row_cur = lambda n: pl.BlockSpec((tm, n), lambda s: (s, 0))
    row_prev = row_cur
    colt = pl.BlockSpec((1, GLA_QK, tm), lambda s: (s // spb, 0, s % spb))
    W, L = DSA_WIDTH, DSA_BLOCK
    W2, W3 = 2 * W, 3 * W
    strided = lambda d, n: pl.BlockSpec((None, d, tm // d, n), lambda s: (s // spb, 0, s % spb, 0))
    assert tm % (4 * L) == 0
    kt1_spec = pl.BlockSpec((None, tm // L, None, W, L), lambda s: (s // spb, s % spb, 0, 0, 0))
    kt4_spec = pl.BlockSpec((None, tm // (4 * L), 4, W, L), lambda s: (s // spb, s % spb, 0, 0, 0))
    return pl.pallas_call(
        _proj_kernel,
        grid=(nt,),
        in_specs=[row_cur(D_MODEL)] + [full(a) for a in (g, wq, wv, wr, wk, wg, w2, b, wd)],
        out_specs=[row_cur(GLA_QK), row_cur(GLA_WIDTH), row_cur(GLA_WIDTH), row_cur(W2),
                   colt, row_prev(GLA_QK), colt, strided(4, W2), strided(16, W3), kt1_spec, kt4_spec],
        out_shape=[jax.ShapeDtypeStruct((T, GLA_QK), BF16),
                   jax.ShapeDtypeStruct((T, GLA_WIDTH), BF16),
                   jax.ShapeDtypeStruct((T, GLA_WIDTH), BF16),
                   jax.ShapeDtypeStruct((T, W2), BF16),
                   jax.ShapeDtypeStruct((batch, GLA_QK, seq), BF16),
                   jax.ShapeDtypeStruct((T, GLA_QK), F32),
                   jax.ShapeDtypeStruct((batch, GLA_QK, seq), F32),
                   jax.ShapeDtypeStruct((batch, 4, seq // 4, W2), BF16),
                   jax.ShapeDtypeStruct((batch, 16, seq // 16, W3), BF16),
                   jax.ShapeDtypeStruct((batch, seq // L, 1, W, L), BF16),
                   jax.ShapeDtypeStruct((batch, seq // (4 * L), 4, W, L), BF16)],
        scratch_shapes=[pltpu.VMEM((2, tm, LANES), F32)] * 2,
        compiler_params=pltpu.CompilerParams(dimension_semantics=("arbitrary",),
                                             vmem_limit_bytes=VMEM_LIMIT),
        name="proj",
    )(x2, g, wq, wv, wr, wk, wg, w2, b, wd)


def _gla_kernel(gq_ref, kt_ref, gv_ref, gr_ref, glog_ref, glogt_ref, gn_ref, o_ref, s_ref, *, pairs):
    C = GLA_CHUNK
    P = 2 * C

    @pl.when(pl.program_id(1) == 0)
    def _():
        s_ref[...] = jnp.zeros_like(s_ref)

    ri = lax.broadcasted_iota(jnp.int32, (P, P), 0)
    ci = lax.broadcasted_iota(jnp.int32, (P, P), 1)
    same_chunk = (ri < C) == (ci < C)
    causal = same_chunk & (ci <= ri)
    low = jnp.where(causal, 1.0, 0.0).astype(BF16)
    upp = jnp.where(same_chunk & (ri <= ci), 1.0, 0.0).astype(BF16)
    lane = lax.broadcasted_iota(jnp.int32, (P, LANES), 1)
    first_half = lane < C
    lane_t = lax.broadcasted_iota(jnp.int32, (GLA_QK, P), 1)
    first_t = lane_t < C

    zero16 = jnp.zeros((), BF16)
    state = [s_ref[h * GLA_DK:(h + 1) * GLA_DK, :] for h in range(GLA_HEADS)]

    heads = range(GLA_HEADS)
    hk = [slice(h * GLA_DK, (h + 1) * GLA_DK) for h in heads]
    hv = [slice(h * GLA_DV, (h + 1) * GLA_DV) for h in heads]
    grp = [slice((h // 2) * LANES, (h // 2 + 1) * LANES) for h in heads]

    def decay_stage(p):
        rows = slice(p * P, (p + 1) * P)
        g_hi, g_lo = _split_hi_lo(glog_ref[rows, :])
        b = _dot(low, g_hi) + _dot(low, g_lo)
        gt = glogt_ref[0, :, rows]
        gt_hi, gt_lo = _split_hi_lo(gt)
        bt = _dot(gt_hi, upp) + _dot(gt_lo, upp)
        tot_a = jnp.sum(jnp.where(first_t, gt, 0.0), axis=-1, keepdims=True)
        tot_b = jnp.sum(jnp.where(first_t, 0.0, gt), axis=-1, keepdims=True)
        return dict(rows=rows, b=b, bt=bt, dec_a=jnp.exp2(tot_a), dec_b=jnp.exp2(tot_b))

    def score_stage(c):
        rows = c["rows"]
        qd = (gq_ref[rows, :].astype(F32) * jnp.exp2(c["b"])).astype(BF16)
        kinv_t = (kt_ref[0, :, rows].astype(F32) * jnp.exp2(-c["bt"])).astype(BF16)
        kinv_a = jnp.where(first_t, kinv_t, zero16)
        kinv_b = jnp.where(first_t, zero16, kinv_t)
        c["v"] = [gv_ref[rows, hv[h]] for h in heads]
        c["qm"] = [jnp.where(first_half if h % 2 == 0 else ~first_half, qd[:, grp[h]], zero16) for h in heads]
        c["att"] = [_dot(c["qm"][h], kinv_t[grp[h], :]) for h in heads]
        c["upd_a"] = [_dot(kinv_a[hk[h], :], c["v"][h]) for h in heads]
        c["upd_b"] = [_dot(kinv_b[hk[h], :], c["v"][h]) for h in heads]

    def output_stage(c):
        o = []
        s_a = list(state)
        s_b = [c["dec_a"][hk[h], :] * (s_a[h] + c["upd_a"][h]) for h in heads]
        for h in heads:
            state[h] = c["dec_b"][hk[h], :] * (s_b[h] + c["upd_b"][h])
        for h in heads:
            att = jnp.where(causal, c["att"][h], 0.0).astype(BF16)
            pair_a = [s_a[h], s_a[h ^ 1]] if h % 2 == 0 else [s_a[h ^ 1], s_a[h]]
            pair_b = [s_b[h], s_a[h ^ 1]] if h % 2 == 0 else [s_a[h ^ 1], s_b[h]]
            o_a = _dot(c["qm"][h][:C, :], jnp.concatenate(pair_a, axis=0).astype(BF16))
            o_b = _dot(c["qm"][h][C:, :], jnp.concatenate(pair_b, axis=0).astype(BF16))
            o.append(_dot(att, c["v"][h]) + jnp.concatenate([o_a, o_b], axis=0))
        c["o"] = o

    def norm_stage(c):
        rows = c["rows"]
        for h in heads:
            o = c["o"][h]
            o = o * lax.rsqrt(jnp.mean(o * o, axis=-1, keepdims=True) + EPS) * gn_ref[:, hv[h]]
            r = gr_ref[rows, hv[h]].astype(F32)
            o_ref[rows, hv[h]] = (o * (r * jax.nn.sigmoid(r))).astype(BF16)

    ctx = {}
    for t in range(pairs + 3):
        if t < pairs:
            ctx[t] = decay_stage(t)
        if 0 <= t - 1 < pairs:
            score_stage(ctx[t - 1])
        if 0 <= t - 2 < pairs:
            output_stage(ctx[t - 2])
        if 0 <= t - 3 < pairs:
            norm_stage(ctx.pop(t - 3))

    for h in heads:
        s_ref[hk[h], :] = state[h]


def _gla(gq, kt, gv, gr, glog, glogt, gn, *, batch, seq, tg):
    T = batch * seq
    spb = seq // tg
    row = lambda n: pl.BlockSpec((tg, n), lambda b, i: (b * spb + i, 0))
    colt = pl.BlockSpec((1, GLA_QK, tg), lambda b, i: (b, 0, i))
    return pl.pallas_call(
        functools.partial(_gla_kernel, pairs=tg // (2 * GLA_CHUNK)),
        grid=(batch, spb),
        in_specs=[row(GLA_QK), colt, row(GLA_WIDTH), row(GLA_WIDTH), row(GLA_QK), colt,
                  pl.BlockSpec((1, GLA_WIDTH), lambda b, i: (0, 0))],
        out_specs=row(GLA_WIDTH),
        out_shape=jax.ShapeDtypeStruct((T, GLA_WIDTH), BF16),
        scratch_shapes=[pltpu.VMEM((GLA_QK, GLA_DV), F32)],
        compiler_params=pltpu.CompilerParams(dimension_semantics=("arbitrary", "arbitrary"),
                                             vmem_limit_bytes=VMEM_LIMIT),
        name="gla",
    )(gq, kt, gv, gr, glog, glogt, gn)


def _dsa_kernel(q_ref, k_ref, v_ref, kh_ref, vh_ref, bias_ref, o_ref, st_ref, *, res, blocks, keys_transposed):
    L = DSA_BLOCK
    first_tile = pl.program_id(2) == 0
    lane = lax.broadcasted_iota(jnp.int32, (L, LANES), 1)
    first_half = lane < DSA_DH
    prev_cols = lax.broadcasted_iota(jnp.int32, (L, 2 * L), 1) < L

    st_tiles = {}
    pair_out = {}

    def score_stage(r, blk, h):
        rows = slice(blk * L, (blk + 1) * L)
        both = slice((blk - 1) * L, (blk + 1) * L)
        grp = slice((h // 2) * LANES, (h // 2 + 1) * LANES)
        if keys_transposed:
            k_prev = kh_ref[r, grp, :] if blk == 0 else k_ref[blk - 1, r, grp, :]
            kcat = jnp.concatenate([k_prev, k_ref[blk, r, grp, :]], axis=1)
        elif blk == 0:
            kcat = jnp.concatenate([kh_ref[r, :, grp], k_ref[r, rows, grp]], axis=0)
        else:
            kcat = k_ref[r, both, grp]
        own = first_half if h % 2 == 0 else ~first_half
        qm = jnp.where(own, q_ref[r, rows, grp], jnp.zeros((), BF16))
        return _dot(qm, kcat) if keys_transposed else _dot_nt(qm, kcat)

    def value_stage(r, blk, h, s):
        rows = slice(blk * L, (blk + 1) * L)
        grp = slice((h // 2) * LANES, (h // 2 + 1) * LANES)
        bias = bias_ref[h]
        if blk == 0:
            bias = jnp.where(prev_cols & first_tile, NEG, bias)
        s = s + bias
        m = jnp.max(s, axis=-1, keepdims=True)
        p = jnp.exp2(s - m)
        den = jnp.sum(p, axis=-1, keepdims=True)
        st_old = st_tiles.get((r, blk), jnp.zeros((L, LANES), F32))
        st_tiles[r, blk] = jnp.where(lane == h, m, jnp.where(lane == DSA_HEADS + h, den, st_old))
        if blk == 0:
            vcat = jnp.concatenate([vh_ref[r, :, grp], v_ref[r, rows, grp]], axis=0)
        else:
            vcat = v_ref[r, both_rows(blk), grp]
        out = _dot(p.astype(BF16), vcat)
        if h % 2 == 0:
            pair_out[r, blk] = out
        else:
            o_ref[r, rows, grp] = jnp.where(first_half, pair_out.pop((r, blk)), out).astype(BF16)
        if h == DSA_HEADS - 1:
            st_ref[r, rows, :] = st_tiles.pop((r, blk))

    def both_rows(blk):
        return slice((blk - 1) * L, (blk + 1) * L)

    items = [(r, blk, h) for r in range(res) for blk in range(blocks) for h in range(DSA_HEADS)]
    lag = 1
    pending = {}
    for t in range(len(items) + lag):
        if t < len(items):
            pending[t] = score_stage(*items[t])
        if t - lag >= 0:
            value_stage(*items[t - lag], pending.pop(t - lag))


def _dsa_branch(x, kt, bias, branch, *, rows_per_step):
    batch, d, n, _ = x.shape
    tr = min(rows_per_step, n)
    res = min(rows_per_step // tr, d)
    hb = tr // DSA_BLOCK
    W = DSA_WIDTH
    prev_blk = lambda i: jnp.maximum(i * hb - 1, 0)
    blk = lambda part: pl.BlockSpec((None, res, tr, W), lambda b, r, i: (b, r, i, part))
    halo = lambda part: pl.BlockSpec((None, res, DSA_BLOCK, W), lambda b, r, i: (b, r, prev_blk(i), part))
    if kt is None:
        v_part = 2
        k_arg, k_spec, kh_spec = x, blk(1), halo(1)
    else:
        v_part = 1
        k_arg = kt
        k_spec = pl.BlockSpec((None, hb, res, W, DSA_BLOCK), lambda b, r, i: (b, i, r, 0, 0))
        kh_spec = pl.BlockSpec((None, None, res, W, DSA_BLOCK), lambda b, r, i: (b, prev_blk(i), r, 0, 0))
    return pl.pallas_call(
        functools.partial(_dsa_kernel, res=res, blocks=hb, keys_transposed=kt is not None),
        grid=(batch, d // res, n // tr),
        in_specs=[blk(0), k_spec, blk(v_part), kh_spec, halo(v_part),
                  pl.BlockSpec((None, DSA_HEADS, DSA_BLOCK, 2 * DSA_BLOCK), lambda b, r, i: (branch, 0, 0, 0))],
        out_specs=[pl.BlockSpec((None, res, tr, W), lambda b, r, i: (b, r, i, 0)),
                   pl.BlockSpec((None, res, tr, LANES), lambda b, r, i: (b, r, i, 0))],
        out_shape=[jax.ShapeDtypeStruct((batch, d, n, W), BF16),
                   jax.ShapeDtypeStruct((batch, d, n, LANES), F32)],
        compiler_params=pltpu.CompilerParams(dimension_semantics=("arbitrary",) * 3,
                                             vmem_limit_bytes=VMEM_LIMIT),
        name=f"dsa_d{d}",
    )(x, k_arg, x, k_arg, x, bias)


def _out_kernel(x_ref, oa_ref, o1_ref, o4_ref, o16_ref, l1_ref, l4_ref, l16_ref,
                f1_ref, f4_ref, f16_ref, g1_ref, g4_ref, g16_ref,
                wo_ref, g2_ref, w1_ref, w2_ref, gf_ref, y_ref, osc_ref, lsc_ref, ob_ref, tsc_ref, *, ff_chunk):
    tm = x_ref.shape[0]
    npair = DSA_HEADS // 2
    lane = lax.broadcasted_iota(jnp.int32, (tm, LANES), 1)
    first_half = lane < DSA_DH

    def combine(o1_ref, o4_ref, o16_ref, l1_ref, l4_ref, l16_ref):
        n4, n16 = tm // 4, tm // 16
        for r4 in range(4):
            for r2 in range(4):
                dst = pl.ds(r4 * n4 + r2, n16, stride=4)
                tsc_ref[npair, dst, :] = l16_ref[r4 + 4 * r2]
                for hp in range(npair):
                    tsc_ref[hp, dst, :] = o16_ref[r4 + 4 * r2, :, hp * LANES:(hp + 1) * LANES].astype(F32)
        for r4 in range(4):
            dst = pl.ds(r4, n4, stride=4)
            src = slice(r4 * n4, (r4 + 1) * n4)
            lsc_ref[0, dst, :] = l4_ref[r4]
            lsc_ref[1, dst, :] = tsc_ref[npair, src, :]
            for hp in range(npair):
                osc_ref[hp, dst, :] = o4_ref[r4, :, hp * LANES:(hp + 1) * LANES].astype(F32)
                osc_ref[npair + hp, dst, :] = tsc_ref[hp, src, :]

        sts = (l1_ref[...], lsc_ref[0], lsc_ref[1])
        m = jnp.maximum(jnp.maximum(sts[0], sts[1]), sts[2])
        es = [jnp.exp2(st - m) for st in sts]
        total = sum(e * pltpu.roll(st, LANES - DSA_HEADS, axis=1) for e, st in zip(es, sts))
        inv = 1.0 / jnp.where(lane < DSA_HEADS, total, 1.0)
        ws = [e * inv for e in es]

        for hp in range(npair):
            grp = slice(hp * LANES, (hp + 1) * LANES)
            branch_o = (o1_ref[:, grp].astype(F32), osc_ref[hp], osc_ref[npair + hp])
            acc = jnp.zeros((tm, LANES), F32)
            for w, o in zip(ws, branch_o):
                wa = jnp.sum(jnp.where(lane == 2 * hp, w, 0.0), axis=-1, keepdims=True)
                wb = jnp.sum(jnp.where(lane == 2 * hp + 1, w, 0.0), axis=-1, keepdims=True)
                acc = acc + jnp.where(first_half, wa, wb) * o
            ob_ref[:, grp] = acc.astype(BF16)

    @pl.when(pl.program_id(0) == 0)
    def _():
        combine(f1_ref, f4_ref, f16_ref, g1_ref, g4_ref, g16_ref)

    mixed = _dot(oa_ref[...], wo_ref[:GLA_WIDTH, :]) + _dot(ob_ref[...], wo_ref[GLA_WIDTH:, :])
    h = x_ref[...] + mixed
    nm = (h * lax.rsqrt(jnp.mean(h * h, axis=-1, keepdims=True) + EPS) * g2_ref[...]).astype(BF16)
    ff = None
    for c in range(D_FF // ff_chunk):
        cols = slice(c * ff_chunk, (c + 1) * ff_chunk)
        a = jnp.maximum(_dot(nm, w1_ref[:, cols]), 0.0)
        d = _dot((a * a).astype(BF16), w2_ref[cols, :])
        ff = d if ff is None else ff + d
    h = h + ff
    y_ref[...] = h * lax.rsqrt(jnp.mean(h * h, axis=-1, keepdims=True) + EPS) * gf_ref[...]

    combine(o1_ref, o4_ref, o16_ref, l1_ref, l4_ref, l16_ref)


def _out(x2, oa, os_, ls, wo, g2, w1, w2, gf, *, seq, tm, ff_chunk):
    T = x2.shape[0]
    spb = seq // tm
    nt = T // tm
    nxt = lambda s: jnp.minimum(s + 1, nt - 1)
    one = pl.Buffered(1)
    row = lambda n: pl.BlockSpec((tm, n), lambda s: (s, 0))
    row_nxt = lambda n: pl.BlockSpec((tm, n), lambda s: (nxt(s), 0))
    row_first = lambda n: pl.BlockSpec((tm, n), lambda s: (0, 0), pipeline_mode=one)
    strided = lambda d, n: pl.BlockSpec((None, d, tm // d, n), lambda s: (nxt(s) // spb, 0, nxt(s) % spb, 0))
    strided_first = lambda d, n: pl.BlockSpec((None, d, tm // d, n), lambda s: (0, 0, 0, 0), pipeline_mode=one)
    const = lambda a: pl.BlockSpec(a.shape, lambda s: (0,) * a.ndim, pipeline_mode=one)
    W = DSA_WIDTH
    return pl.pallas_call(
        functools.partial(_out_kernel, ff_chunk=ff_chunk),
        grid=(nt,),
        in_specs=[row(D_MODEL), row(GLA_WIDTH), row_nxt(W), strided(4, W), strided(16, W),
                  row_nxt(LANES), strided(4, LANES), strided(16, LANES),
                  row_first(W), strided_first(4, W), strided_first(16, W),
                  row_first(LANES), strided_first(4, LANES), strided_first(16, LANES),
                  const(wo), const(g2), const(w1), const(w2), const(gf)],
        out_specs=row(D_MODEL),
        out_shape=jax.ShapeDtypeStruct((T, D_MODEL), F32),
        scratch_shapes=[pltpu.VMEM((2 * (W // LANES), tm, LANES), F32),
                        pltpu.VMEM((2, tm, LANES), F32),
                        pltpu.VMEM((tm, W), BF16),
                        pltpu.VMEM((W // LANES + 1, tm, LANES), F32)],
        compiler_params=pltpu.CompilerParams(dimension_semantics=("arbitrary",),
                                             vmem_limit_bytes=VMEM_LIMIT),
        name="out_mlp",
    )(x2, oa, *os_, *ls, *os_, *ls, wo, g2, w1, w2, gf)


_IN_SPLITS = (GLA_QK, GLA_QK, GLA_WIDTH, GLA_WIDTH, GLA_RANK, DSA_WIDTH, DSA_WIDTH, DSA_WIDTH)
_IN_OFFSETS = tuple(int(v) for v in np.cumsum((0,) + _IN_SPLITS))


def _split_w_in_kernel(wt_ref, wq_ref, wk_ref, wv_ref, wr_ref, wg_ref, wd_ref):
    o = _IN_OFFSETS
    lane = lax.broadcasted_iota(jnp.int32, (wt_ref.shape[1], RANK_PAD), 1)

    def piece(start):
        return wt_ref[start:start + LANES, :].T

    for dst, start, width, scale in ((wq_ref, o[0], GLA_QK, GLA_DK ** -0.5), (wk_ref, o[1], GLA_QK, None),
                                     (wv_ref, o[2], GLA_WIDTH, None), (wr_ref, o[3], GLA_WIDTH, None),
                                     (wd_ref, o[5], 3 * DSA_WIDTH, None)):
        for j in range(width // LANES):
            blk = piece(start + j * LANES)
            if scale is not None:
                blk = blk * scale
            if dst is wd_ref and j < DSA_WIDTH // LANES:
                blk = blk * (DSA_DH ** -0.5 * LOG2E)
            dst[:, j * LANES:(j + 1) * LANES] = blk.astype(BF16)
    wg_ref[...] = jnp.where(lane < GLA_RANK, piece(o[4]), 0.0).astype(BF16)


def _split_w_in(wt):
    _, N, K = wt.shape
    shapes = [(K, GLA_QK), (K, GLA_QK), (K, GLA_WIDTH), (K, GLA_WIDTH), (K, RANK_PAD), (K, 3 * DSA_WIDTH)]
    return pl.pallas_call(
        _split_w_in_kernel,
        grid=(1,),
        in_specs=[pl.BlockSpec((None, N, K), lambda i: (0, 0, 0))],
        out_specs=[pl.BlockSpec(s, lambda i: (0, 0)) for s in shapes],
        out_shape=[jax.ShapeDtypeStruct(s, BF16) for s in shapes],
        compiler_params=pltpu.CompilerParams(vmem_limit_bytes=VMEM_LIMIT),
        name="split_w_in",
    )(wt)


def kernel(x, attn_norm_g, w_in, gla_gate_w2, gla_gate_b, gla_norm_g, rel_bias, w_out, mlp_norm_g,
           w_ff1, w_ff2, final_norm_g):
    batch, seq, _ = x.shape
    assert seq % (DSA_PATTERN[-1][1] * DSA_BLOCK) == 0
    T = batch * seq
    x2 = x.reshape(T, D_MODEL)

    wq, wk, wv, wr, wg, wd = _split_w_in(jnp.swapaxes(w_in, 1, 2))
    w2 = jnp.pad(gla_gate_w2[0], ((0, RANK_PAD - GLA_RANK), (0, 0))).astype(BF16)
    gb = gla_gate_b[0].astype(F32).reshape(1, GLA_QK)

    gq, gv, gr, d1, kt, glog, glogt, d4, d16, kt1, kt4 = _proj(
        x2, attn_norm_g[0].reshape(1, D_MODEL).astype(F32), wq, wv, wr, wk, wg, w2, gb, wd,
        batch=batch, seq=seq, tm=1024)

    o_a = _gla(gq, kt, gv, gr, glog, glogt, gla_norm_g[0].reshape(1, GLA_WIDTH).astype(F32),
               batch=batch, seq=seq, tg=2048)

    bias = _bias_tables(rel_bias)
    os_, ls = [], []
    branches = ((d1.reshape(batch, 1, seq, 2 * DSA_WIDTH), kt1), (d4, kt4), (d16, None))
    for branch, (xd, ktd) in enumerate(branches):
        o, lse = _dsa_branch(xd, ktd, bias, branch, rows_per_step=2048)
        os_.append(o)
        ls.append(lse)
    os_[0] = os_[0].reshape(T, DSA_WIDTH)
    ls[0] = ls[0].reshape(T, LANES)

    y = _out(x2, o_a, os_, ls, w_out[0].astype(BF16), mlp_norm_g[0].reshape(1, D_MODEL).astype(F32),
             w_ff1[0].astype(BF16), w_ff2[0].astype(BF16), final_norm_g.reshape(1, D_MODEL).astype(F32),
             seq=seq, tm=512, ff_chunk=2048)
    return y.reshape(batch, seq, D_MODEL)
```

```python
import functools
import math

import numpy as np
import jax
import jax.numpy as jnp
from jax import lax
from jax.experimental import pallas as pl
from jax.experimental.pallas import tpu as pltpu

D_MODEL = 1024
GLA_WIDTH = 512
GLA_HEADS = 4
GLA_DK = 64
GLA_DV = 128
GLA_QK = GLA_HEADS * GLA_DK
GLA_RANK = 16
GLA_TAU = 16.0
GLA_CHUNK = 64
DSA_WIDTH = 512
DSA_HEADS = 8
DSA_DH = 64
DSA_PATTERN = ((128, 1), (512, 4), (2048, 16))
DSA_BLOCK = 128
REL_BUCKETS = 32
REL_MAX_DIST = 2048
D_FF = 4096
EPS = 1e-6
NEG = -1e30
LOG2E = math.log2(math.e)

LANES = 128
RANK_PAD = LANES
VMEM_LIMIT = 56 * 1024 * 1024

F32 = jnp.float32
BF16 = jnp.bfloat16

_NT = (((1,), (1,)), ((), ()))


def _dot(a, b):
    return jnp.dot(a, b, preferred_element_type=F32)


def _dot_nt(a, b):
    return lax.dot_general(a, b, _NT, preferred_element_type=F32)


def _split_hi_lo(x):
    hi = x.astype(BF16)
    lo = (x - hi.astype(F32)).astype(BF16)
    return hi, lo


def _log_sigmoid(x):
    return jnp.minimum(x, 0.0) - jnp.log1p(jnp.exp(-jnp.abs(x)))


def _bucket_tables():
    max_exact = REL_BUCKETS // 2
    L = DSA_BLOCK
    steps = L + np.arange(L)[:, None] - np.arange(2 * L)[None, :]
    tables = []
    for window, dilation in DSA_PATTERN:
        span = window // dilation
        in_band = (steps >= 0) & (steps <= span)
        n = np.maximum(steps * dilation, 0)
        large = max_exact + (np.log(np.maximum(n, 1) / max_exact)
                             / math.log(REL_MAX_DIST / max_exact)
                             * (REL_BUCKETS - max_exact)).astype(np.int32)
        large = np.minimum(large, REL_BUCKETS - 1)
        bucket = np.where(n < max_exact, n, large).astype(np.int32)
        tables.append(np.where(in_band, bucket, -1).astype(np.int32))
    return np.stack(tables)


def _bias_kernel(rb_ref, bt_ref, out_ref):
    for d in range(len(DSA_PATTERN)):
        bt = bt_ref[d]

        def one_head(h, carry, bt=bt, d=d):
            acc = jnp.full(bt.shape, NEG, F32)
            for b in range(REL_BUCKETS):
                acc = jnp.where(bt == b, rb_ref[b, h] * LOG2E, acc)
            out_ref[d, h] = acc
            return carry

        lax.fori_loop(0, DSA_HEADS, one_head, 0)


def _bias_tables(rel_bias):
    nb = len(DSA_PATTERN)
    L = DSA_BLOCK
    return pl.pallas_call(
        _bias_kernel,
        grid=(1,),
        in_specs=[pl.BlockSpec(memory_space=pltpu.SMEM),
                  pl.BlockSpec((nb, L, 2 * L), lambda i: (0, 0, 0))],
        out_specs=pl.BlockSpec((nb, DSA_HEADS, L, 2 * L), lambda i: (0, 0, 0, 0)),
        out_shape=jax.ShapeDtypeStruct((nb, DSA_HEADS, L, 2 * L), F32),
        name="dsa_bias",
    )(rel_bias.astype(F32), jnp.asarray(_bucket_tables()))


def _proj_kernel(x_ref, g_ref, wq_ref, wv_ref, wr_ref, wk_ref, wg_ref, w2_ref, b_ref, wd_ref,
                 gq_ref, gv_ref, gr_ref, d1_ref, kt_ref, glogt_ref, d4_ref, d16_ref,
                 kt1_ref, kt4_ref, dsc_ref, t4_ref):
    tm = x_ref.shape[0]
    ncol = 3 * DSA_WIDTH // LANES
    kcols = range(DSA_WIDTH // LANES, 2 * DSA_WIDTH // LANES)
    n4 = tm // 4
    n16 = tm // 16

    x = x_ref[...]
    ms = jnp.mean(x * x, axis=-1, keepdims=True)
    nx = (x * lax.rsqrt(ms + EPS) * g_ref[...]).astype(BF16)

    for j in range(ncol // 2):
        res = _dot(nx, wd_ref[:, 2 * j * LANES:(2 * j + 2) * LANES])
        halves = {2 * j: res[:, :LANES], 2 * j + 1: res[:, LANES:]}
        for c in halves:
            dsc_ref[c % 2] = halves[c]
        for c in halves:
            sl = c % 2
            cols = slice(c * LANES, (c + 1) * LANES)
            is_key = c in kcols
            krows = slice((c - kcols[0]) * LANES, (c - kcols[0] + 1) * LANES)
            qv = c if c < kcols[0] else c - len(kcols)
            qv_cols = slice(qv * LANES, (qv + 1) * LANES)
            if is_key:
                for jb in range(tm // DSA_BLOCK):
                    blk_rows = slice(jb * DSA_BLOCK, (jb + 1) * DSA_BLOCK)
                    kt1_ref[jb, krows, :] = dsc_ref[sl, blk_rows, :].T.astype(BF16)
            else:
                d1_ref[:, qv_cols] = halves[c].astype(BF16)
            for r4 in range(4):
                sub = dsc_ref[sl, pl.ds(r4, n4, stride=4), :]
                t4_ref[sl, r4 * n4:(r4 + 1) * n4, :] = sub
                if is_key:
                    for jb in range(n4 // DSA_BLOCK):
                        blk_rows = slice(jb * DSA_BLOCK, (jb + 1) * DSA_BLOCK)
                        kt4_ref[jb, r4, krows, :] = sub[blk_rows, :].T.astype(BF16)
                else:
                    d4_ref[r4, :, qv_cols] = sub.astype(BF16)
            for r4 in range(4):
                for r2 in range(4):
                    d16_ref[r4 + 4 * r2, :, cols] = (
                        t4_ref[sl, pl.ds(r4 * n4 + r2, n16, stride=4), :].astype(BF16))

    gq_ref[...] = _dot(nx, wq_ref[...]).astype(BF16)

    kt_ref[0] = _dot(nx, wk_ref[...]).T.astype(BF16)
    glow = _dot(nx, wg_ref[...]).astype(BF16)
    glog = _log_sigmoid(_dot(glow, w2_ref[...]) + b_ref[...]) * (LOG2E / GLA_TAU)
    glogt_ref[0] = glog.T

    gv_ref[...] = _dot(nx, wv_ref[...]).astype(BF16)
    gr_ref[...] = _dot(nx, wr_ref[...]).astype(BF16)


def _proj(x2, g, wq, wv, wr, wk, wg, w2, b, wd, *, batch, seq, tm):
    T = batch * seq
    spb = seq // tm
    nt = T // tm
    full = lambda a: pl.BlockSpec(a.shape, lambda s: (0,) * a.ndim, pipeline_mode=pl.Buffered(1))
    row_cur = lambda n: pl.BlockSpec((tm, n), lambda s: (s, 0))
    colt = pl.BlockSpec((1, GLA_QK, tm), lambda s: (s // spb, 0, s % spb))
    W, L = DSA_WIDTH, DSA_BLOCK
    W2, W3 = 2 * W, 3 * W
    strided = lambda d, n: pl.BlockSpec((None, d, tm // d, n), lambda s: (s // spb, 0, s % spb, 0))
    assert tm % (4 * L) == 0
    kt1_spec = pl.BlockSpec((None, tm // L, None, W, L), lambda s: (s // spb, s % spb, 0, 0, 0))
    kt4_spec = pl.BlockSpec((None, tm // (4 * L), 4, W, L), lambda s: (s // spb, s % spb, 0, 0, 0))
    return pl.pallas_call(
        _proj_kernel,
        grid=(nt,),
        in_specs=[row_cur(D_MODEL)] + [full(a) for a in (g, wq, wv, wr, wk, wg, w2, b, wd)],
        out_specs=[row_cur(GLA_QK), row_cur(GLA_WIDTH), row_cur(GLA_WIDTH), row_cur(W2),
                   colt, colt, strided(4, W2), strided(16, W3), kt1_spec, kt4_spec],
        out_shape=[jax.ShapeDtypeStruct((T, GLA_QK), BF16),
                   jax.ShapeDtypeStruct((T, GLA_WIDTH), BF16),
                   jax.ShapeDtypeStruct((T, GLA_WIDTH), BF16),
                   jax.ShapeDtypeStruct((T, W2), BF16),
                   jax.ShapeDtypeStruct((batch, GLA_QK, seq), BF16),
                   jax.ShapeDtypeStruct((batch, GLA_QK, seq), F32),
                   jax.ShapeDtypeStruct((batch, 4, seq // 4, W2), BF16),
                   jax.ShapeDtypeStruct((batch, 16, seq // 16, W3), BF16),
                   jax.ShapeDtypeStruct((batch, seq // L, 1, W, L), BF16),
                   jax.ShapeDtypeStruct((batch, seq // (4 * L), 4, W, L), BF16)],
        scratch_shapes=[pltpu.VMEM((2, tm, LANES), F32)] * 2,
        compiler_params=pltpu.CompilerParams(dimension_semantics=("arbitrary",),
                                             vmem_limit_bytes=VMEM_LIMIT),
        name="proj",
    )(x2, g, wq, wv, wr, wk, wg, w2, b, wd)


def _gla_kernel(gq_ref, kt_ref, gv_ref, gr_ref, glogt_ref, gn_ref, o_ref, s_ref, *, pairs):
    C = GLA_CHUNK
    P = 2 * C

    @pl.when(pl.program_id(1) == 0)
    def _():
        s_ref[...] = jnp.zeros_like(s_ref)

    ri = lax.broadcasted_iota(jnp.int32, (P, P), 0)
    ci = lax.broadcasted_iota(jnp.int32, (P, P), 1)
    same_chunk = (ri < C) == (ci < C)
    causal = same_chunk & (ci <= ri)
    upp = jnp.where(same_chunk & (ri <= ci), 1.0, 0.0).astype(BF16)
    lane = lax.broadcasted_iota(jnp.int32, (P, LANES), 1)
    first_half = lane < C
    lane_t = lax.broadcasted_iota(jnp.int32, (GLA_QK, P), 1)
    first_t = lane_t < C

    zero16 = jnp.zeros((), BF16)
    state = [s_ref[h * GLA_DK:(h + 1) * GLA_DK, :] for h in range(GLA_HEADS)]

    heads = range(GLA_HEADS)
    hk = [slice(h * GLA_DK, (h + 1) * GLA_DK) for h in heads]
    hv = [slice(h * GLA_DV, (h + 1) * GLA_DV) for h in heads]
    grp = [slice((h // 2) * LANES, (h // 2 + 1) * LANES) for h in heads]

    def decay_stage(p):
        rows = slice(p * P, (p + 1) * P)
        gt = glogt_ref[0, :, rows]
        gt_hi, gt_lo = _split_hi_lo(gt)
        bt = _dot(gt_hi, upp) + _dot(gt_lo, upp)
        b = bt.T
        tot_a = bt[:, C - 1:C]
        tot_b = bt[:, P - 1:P]
        return dict(rows=rows, b=b, bt=bt, dec_a=jnp.exp2(tot_a), dec_b=jnp.exp2(tot_b))

    def score_stage(c):
        rows = c["rows"]
        qd = (gq_ref[rows, :].astype(F32) * jnp.exp2(c["b"])).astype(BF16)
        kinv_t = (kt_ref[0, :, rows].astype(F32) * jnp.exp2(-c["bt"])).astype(BF16)
        kinv_a = jnp.where(first_t, kinv_t, zero16)
        kinv_b = jnp.where(first_t, zero16, kinv_t)
        c["v"] = [gv_ref[rows, hv[h]] for h in heads]
        c["qm"] = [jnp.where(first_half if h % 2 == 0 else ~first_half, qd[:, grp[h]], zero16) for h in heads]
        c["att"] = [_dot(c["qm"][h], kinv_t[grp[h], :]) for h in heads]
        c["upd_a"] = [_dot(kinv_a[hk[h], :], c["v"][h]) for h in heads]
        c["upd_b"] = [_dot(kinv_b[hk[h], :], c["v"][h]) for h in heads]

    def output_stage(c):
        o = []
        s_a = list(state)
        s_b = [c["dec_a"][hk[h], :] * (s_a[h] + c["upd_a"][h]) for h in heads]
        for h in heads:
            state[h] = c["dec_b"][hk[h], :] * (s_b[h] + c["upd_b"][h])
        for h in heads:
            att = jnp.where(causal, c["att"][h], 0.0).astype(BF16)
            pair_a = [s_a[h], s_a[h ^ 1]] if h % 2 == 0 else [s_a[h ^ 1], s_a[h]]
            pair_b = [s_b[h], s_a[h ^ 1]] if h % 2 == 0 else [s_a[h ^ 1], s_b[h]]
            o_a = _dot(c["qm"][h][:C, :], jnp.concatenate(pair_a, axis=0).astype(BF16))
            o_b = _dot(c["qm"][h][C:, :], jnp.concatenate(pair_b, axis=0).astype(BF16))
            o.append(_dot(att, c["v"][h]) + jnp.concatenate([o_a, o_b], axis=0))
        c["o"] = o

    def norm_stage(c):
        rows = c["rows"]
        for h in heads:
            o = c["o"][h]
            o = o * lax.rsqrt(jnp.mean(o * o, axis=-1, keepdims=True) + EPS) * gn_ref[:, hv[h]]
            r = gr_ref[rows, hv[h]].astype(F32)
            o_ref[rows, hv[h]] = (o * (r * jax.nn.sigmoid(r))).astype(BF16)

    ctx = {}
    for t in range(pairs + 3):
        if t < pairs:
            ctx[t] = decay_stage(t)
        if 0 <= t - 1 < pairs:
            score_stage(ctx[t - 1])
        if 0 <= t - 2 < pairs:
            output_stage(ctx[t - 2])
        if 0 <= t - 3 < pairs:
            norm_stage(ctx.pop(t - 3))

    for h in heads:
        s_ref[hk[h], :] = state[h]


def _gla(gq, kt, gv, gr, glogt, gn, *, batch, seq, tg):
    T = batch * seq
    spb = seq // tg
    row = lambda n: pl.BlockSpec((tg, n), lambda b, i: (b * spb + i, 0))
    colt = pl.BlockSpec((1, GLA_QK, tg), lambda b, i: (b, 0, i))
    return pl.pallas_call(
        functools.partial(_gla_kernel, pairs=tg // (2 * GLA_CHUNK)),
        grid=(batch, spb),
        in_specs=[row(GLA_QK), colt, row(GLA_WIDTH), row(GLA_WIDTH), colt,
                  pl.BlockSpec((1, GLA_WIDTH), lambda b, i: (0, 0))],
        out_specs=row(GLA_WIDTH),
        out_shape=jax.ShapeDtypeStruct((T, GLA_WIDTH), BF16),
        scratch_shapes=[pltpu.VMEM((GLA_QK, GLA_DV), F32)],
        compiler_params=pltpu.CompilerParams(dimension_semantics=("arbitrary", "arbitrary"),
                                             vmem_limit_bytes=VMEM_LIMIT),
        name="gla",
    )(gq, kt, gv, gr, glogt, gn)


def _dsa_kernel(q_ref, k_ref, v_ref, kh_ref, vh_ref, bias_ref, o_ref, st_ref, *, res, blocks, keys_transposed):
    L = DSA_BLOCK
    first_tile = pl.program_id(2) == 0
    lane = lax.broadcasted_iota(jnp.int32, (L, LANES), 1)
    first_half = lane < DSA_DH
    prev_cols = lax.broadcasted_iota(jnp.int32, (L, 2 * L), 1) < L

    st_tiles = {}
    pair_out = {}

    def score_stage(r, blk, h):
        rows = slice(blk * L, (blk + 1) * L)
        both = slice((blk - 1) * L, (blk + 1) * L)
        grp = slice((h // 2) * LANES, (h // 2 + 1) * LANES)
        if keys_transposed:
            k_prev = kh_ref[r, grp, :] if blk == 0 else k_ref[blk - 1, r, grp, :]
            kcat = jnp.concatenate([k_prev, k_ref[blk, r, grp, :]], axis=1)
        elif blk == 0:
            kcat = jnp.concatenate([kh_ref[r, :, grp], k_ref[r, rows, grp]], axis=0)
        else:
            kcat = k_ref[r, both, grp]
        own = first_half if h % 2 == 0 else ~first_half
        qm = jnp.where(own, q_ref[r, rows, grp], jnp.zeros((), BF16))
        return _dot(qm, kcat) if keys_transposed else _dot_nt(qm, kcat)

    def value_stage(r, blk, h, s):
        rows = slice(blk * L, (blk + 1) * L)
        grp = slice((h // 2) * LANES, (h // 2 + 1) * LANES)
        bias = bias_ref[h]
        if blk == 0:
            bias = jnp.where(prev_cols & first_tile, NEG, bias)
        s = s + bias
        m = jnp.max(s, axis=-1, keepdims=True)
        p = jnp.exp2(s - m)
        den = jnp.sum(p, axis=-1, keepdims=True)
        st_old = st_tiles.get((r, blk), jnp.zeros((L, LANES), F32))
        st_tiles[r, blk] = jnp.where(lane == h, m, jnp.where(lane == DSA_HEADS + h, den, st_old))
        if blk == 0:
            vcat = jnp.concatenate([vh_ref[r, :, grp], v_ref[r, rows, grp]], axis=0)
        else:
            vcat = v_ref[r, both_rows(blk), grp]
        out = _dot(p.astype(BF16), vcat)
        if h % 2 == 0:
            pair_out[r, blk] = out
        else:
            o_ref[r, rows, grp] = jnp.where(first_half, pair_out.pop((r, blk)), out).astype(BF16)
        if h == DSA_HEADS - 1:
            st_ref[r, rows, :] = st_tiles.pop((r, blk))

    def both_rows(blk):
        return slice((blk - 1) * L, (blk + 1) * L)

    items = [(r, blk, h) for r in range(res) for blk in range(blocks) for h in range(DSA_HEADS)]
    lag = 1
    pending = {}
    for t in range(len(items) + lag):
        if t < len(items):
            pending[t] = score_stage(*items[t])
        if t - lag >= 0:
            value_stage(*items[t - lag], pending.pop(t - lag))


def _dsa_branch(x, kt, bias, branch, *, rows_per_step):
    batch, d, n, _ = x.shape
    tr = min(rows_per_step, n)
    res = min(rows_per_step // tr, d)
    hb = tr // DSA_BLOCK
    W = DSA_WIDTH
    prev_blk = lambda i: jnp.maximum(i * hb - 1, 0)
    blk = lambda part: pl.BlockSpec((None, res, tr, W), lambda b, r, i: (b, r, i, part))
    halo = lambda part: pl.BlockSpec((None, res, DSA_BLOCK, W), lambda b, r, i: (b, r, prev_blk(i), part))
    if kt is None:
        v_part = 2
        k_arg, k_spec, kh_spec = x, blk(1), halo(1)
    else:
        v_part = 1
        k_arg = kt
        k_spec = pl.BlockSpec((None, hb, res, W, DSA_BLOCK), lambda b, r, i: (b, i, r, 0, 0))
        kh_spec = pl.BlockSpec((None, None, res, W, DSA_BLOCK), lambda b, r, i: (b, prev_blk(i), r, 0, 0))
    return pl.pallas_call(
        functools.partial(_dsa_kernel, res=res, blocks=hb, keys_transposed=kt is not None),
        grid=(batch, d // res, n // tr),
        in_specs=[blk(0), k_spec, blk(v_part), kh_spec, halo(v_part),
                  pl.BlockSpec((None, DSA_HEADS, DSA_BLOCK, 2 * DSA_BLOCK), lambda b, r, i: (branch, 0, 0, 0))],
        out_specs=[pl.BlockSpec((None, res, tr, W), lambda b, r, i: (b, r, i, 0)),
                   pl.BlockSpec((None, res, tr, LANES), lambda b, r, i: (b, r, i, 0))],
        out_shape=[jax.ShapeDtypeStruct((batch, d, n, W), BF16),
                   jax.ShapeDtypeStruct((batch, d, n, LANES), F32)],
        compiler_params=pltpu.CompilerParams(dimension_semantics=("arbitrary",) * 3,
                                             vmem_limit_bytes=VMEM_LIMIT),
        name=f"dsa_d{d}",
    )(x, k_arg, x, k_arg, x, bias)


def _out_kernel(x_ref, oa_ref, o1_ref, o4_ref, o16_ref, l1_ref, l4_ref, l16_ref,
                f1_ref, f4_ref, f16_ref, g1_ref, g4_ref, g16_ref,
                wo_ref, g2_ref, w1_ref, w2_ref, gf_ref, y_ref, osc_ref, lsc_ref, ob_ref, tsc_ref, *, ff_chunk):
    tm = x_ref.shape[0]
    npair = DSA_HEADS // 2
    lane = lax.broadcasted_iota(jnp.int32, (tm, LANES), 1)
    first_half = lane < DSA_DH

    def combine(o1_ref, o4_ref, o16_ref, l1_ref, l4_ref, l16_ref):
        n4, n16 = tm // 4, tm // 16
        for r4 in range(4):
            for r2 in range(4):
                dst = pl.ds(r4 * n4 + r2, n16, stride=4)
                tsc_ref[npair, dst, :] = l16_ref[r4 + 4 * r2]
                for hp in range(npair):
                    tsc_ref[hp, dst, :] = o16_ref[r4 + 4 * r2, :, hp * LANES:(hp + 1) * LANES].astype(F32)
        for r4 in range(4):
            dst = pl.ds(r4, n4, stride=4)
            src = slice(r4 * n4, (r4 + 1) * n4)
            lsc_ref[0, dst, :] = l4_ref[r4]
            lsc_ref[1, dst, :] = tsc_ref[npair, src, :]
            for hp in range(npair):
                osc_ref[hp, dst, :] = o4_ref[r4, :, hp * LANES:(hp + 1) * LANES].astype(F32)
                osc_ref[npair + hp, dst, :] = tsc_ref[hp, src, :]

        sts = (l1_ref[...], lsc_ref[0], lsc_ref[1])
        m = jnp.maximum(jnp.maximum(sts[0], sts[1]), sts[2])
        es = [jnp.exp2(st - m) for st in sts]
        total = sum(e * pltpu.roll(st, LANES - DSA_HEADS, axis=1) for e, st in zip(es, sts))
        inv = 1.0 / jnp.where(lane < DSA_HEADS, total, 1.0)
        ws = [e * inv for e in es]

        for hp in range(npair):
            grp = slice(hp * LANES, (hp + 1) * LANES)
            branch_o = (o1_ref[:, grp].astype(F32), osc_ref[hp], osc_ref[npair + hp])
            acc = jnp.zeros((tm, LANES), F32)
            for w, o in zip(ws, branch_o):
                wa = jnp.sum(jnp.where(lane == 2 * hp, w, 0.0), axis=-1, keepdims=True)
                wb = jnp.sum(jnp.where(lane == 2 * hp + 1, w, 0.0), axis=-1, keepdims=True)
                acc = acc + jnp.where(first_half, wa, wb) * o
            ob_ref[:, grp] = acc.astype(BF16)

    @pl.when(pl.program_id(0) == 0)
    def _():
        combine(f1_ref, f4_ref, f16_ref, g1_ref, g4_ref, g16_ref)

    mixed = _dot(oa_ref[...], wo_ref[:GLA_WIDTH, :]) + _dot(ob_ref[...], wo_ref[GLA_WIDTH:, :])
    h = x_ref[...] + mixed
    nm = (h * lax.rsqrt(jnp.mean(h * h, axis=-1, keepdims=True) + EPS) * g2_ref[...]).astype(BF16)
    ff = None
    for c in range(D_FF // ff_chunk):
        cols = slice(c * ff_chunk, (c + 1) * ff_chunk)
        a = jnp.maximum(_dot(nm, w1_ref[:, cols]), 0.0)
        d = _dot((a * a).astype(BF16), w2_ref[cols, :])
        ff = d if ff is None else ff + d
    h = h + ff
    y_ref[...] = h * lax.rsqrt(jnp.mean(h * h, axis=-1, keepdims=True) + EPS) * gf_ref[...]

    combine(o1_ref, o4_ref, o16_ref, l1_ref, l4_ref, l16_ref)


def _out(x2, oa, os_, ls, wo, g2, w1, w2, gf, *, seq, tm, ff_chunk):
    T = x2.shape[0]
    spb = seq // tm
    nt = T // tm
    nxt = lambda s: jnp.minimum(s + 1, nt - 1)
    one = pl.Buffered(1)
    row = lambda n: pl.BlockSpec((tm, n), lambda s: (s, 0))
    row_nxt = lambda n: pl.BlockSpec((tm, n), lambda s: (nxt(s), 0))
    row_first = lambda n: pl.BlockSpec((tm, n), lambda s: (0, 0), pipeline_mode=one)
    strided = lambda d, n: pl.BlockSpec((None, d, tm // d, n), lambda s: (nxt(s) // spb, 0, nxt(s) % spb, 0))
    strided_first = lambda d, n: pl.BlockSpec((None, d, tm // d, n), lambda s: (0, 0, 0, 0), pipeline_mode=one)
    const = lambda a: pl.BlockSpec(a.shape, lambda s: (0,) * a.ndim, pipeline_mode=one)
    W = DSA_WIDTH
    return pl.pallas_call(
        functools.partial(_out_kernel, ff_chunk=ff_chunk),
        grid=(nt,),
        in_specs=[row(D_MODEL), row(GLA_WIDTH), row_nxt(W), strided(4, W), strided(16, W),
                  row_nxt(LANES), strided(4, LANES), strided(16, LANES),
                  row_first(W), strided_first(4, W), strided_first(16, W),
                  row_first(LANES), strided_first(4, LANES), strided_first(16, LANES),
                  const(wo), const(g2), const(w1), const(w2), const(gf)],
        out_specs=row(D_MODEL),
        out_shape=jax.ShapeDtypeStruct((T, D_MODEL), F32),
        scratch_shapes=[pltpu.VMEM((2 * (W // LANES), tm, LANES), F32),
                        pltpu.VMEM((2, tm, LANES), F32),
                        pltpu.VMEM((tm, W), BF16),
                        pltpu.VMEM((W // LANES + 1, tm, LANES), F32)],
        compiler_params=pltpu.CompilerParams(dimension_semantics=("arbitrary",),
                                             vmem_limit_bytes=VMEM_LIMIT),
        name="out_mlp",
    )(x2, oa, *os_, *ls, *os_, *ls, wo, g2, w1, w2, gf)


_IN_SPLITS = (GLA_QK, GLA_QK, GLA_WIDTH, GLA_WIDTH, GLA_RANK, DSA_WIDTH, DSA_WIDTH, DSA_WIDTH)
_IN_OFFSETS = tuple(int(v) for v in np.cumsum((0,) + _IN_SPLITS))


def _split_w_in_kernel(wt_ref, wq_ref, wk_ref, wv_ref, wr_ref, wg_ref, wd_ref):
    o = _IN_OFFSETS
    lane = lax.broadcasted_iota(jnp.int32, (wt_ref.shape[1], RANK_PAD), 1)

    def piece(start):
        return wt_ref[start:start + LANES, :].T

    for dst, start, width, scale in ((wq_ref, o[0], GLA_QK, GLA_DK ** -0.5), (wk_ref, o[1], GLA_QK, None),
                                     (wv_ref, o[2], GLA_WIDTH, None), (wr_ref, o[3], GLA_WIDTH, None),
                                     (wd_ref, o[5], 3 * DSA_WIDTH, None)):
        for j in range(width // LANES):
            blk = piece(start + j * LANES)
            if scale is not None:
                blk = blk * scale
            if dst is wd_ref and j < DSA_WIDTH // LANES:
                blk = blk * (DSA_DH ** -0.5 * LOG2E)
            dst[:, j * LANES:(j + 1) * LANES] = blk.astype(BF16)
    wg_ref[...] = jnp.where(lane < GLA_RANK, piece(o[4]), 0.0).astype(BF16)


def _split_w_in(wt):
    _, N, K = wt.shape
    shapes = [(K, GLA_QK), (K, GLA_QK), (K, GLA_WIDTH), (K, GLA_WIDTH), (K, RANK_PAD), (K, 3 * DSA_WIDTH)]
    return pl.pallas_call(
        _split_w_in_kernel,
        grid=(1,),
        in_specs=[pl.BlockSpec((None, N, K), lambda i: (0, 0, 0))],
        out_specs=[pl.BlockSpec(s, lambda i: (0, 0)) for s in shapes],
        out_shape=[jax.ShapeDtypeStruct(s, BF16) for s in shapes],
        compiler_params=pltpu.CompilerParams(vmem_limit_bytes=VMEM_LIMIT),
        name="split_w_in",
    )(wt)


def kernel(x, attn_norm_g, w_in, gla_gate_w2, gla_gate_b, gla_norm_g, rel_bias, w_out, mlp_norm_g,
           w_ff1, w_ff2, final_norm_g):
    batch, seq, _ = x.shape
    assert seq % (DSA_PATTERN[-1][1] * DSA_BLOCK) == 0
    T = batch * seq
    x2 = x.reshape(T, D_MODEL)

    wq, wk, wv, wr, wg, wd = _split_w_in(jnp.swapaxes(w_in, 1, 2))
    w2 = jnp.pad(gla_gate_w2[0], ((0, RANK_PAD - GLA_RANK), (0, 0))).astype(BF16)
    gb = gla_gate_b[0].astype(F32).reshape(1, GLA_QK)

    gq, gv, gr, d1, kt, glogt, d4, d16, kt1, kt4 = _proj(
        x2, attn_norm_g[0].reshape(1, D_MODEL).astype(F32), wq, wv, wr, wk, wg, w2, gb, wd,
        batch=batch, seq=seq, tm=1024)

    o_a = _gla(gq, kt, gv, gr, glogt, gla_norm_g[0].reshape(1, GLA_WIDTH).astype(F32),
               batch=batch, seq=seq, tg=2048)

    bias = _bias_tables(rel_bias)
    os_, ls = [], []
    branches = ((d1.reshape(batch, 1, seq, 2 * DSA_WIDTH), kt1), (d4, kt4), (d16, None))
    for branch, (xd, ktd) in enumerate(branches):
        o, lse = _dsa_branch(xd, ktd, bias, branch, rows_per_step=2048)
        os_.append(o)
        ls.append(lse)
    os_[0] = os_[0].reshape(T, DSA_WIDTH)
    ls[0] = ls[0].reshape(T, LANES)

    y = _out(x2, o_a, os_, ls, w_out[0].astype(BF16), mlp_norm_g[0].reshape(1, D_MODEL).astype(F32),
             w_ff1[0].astype(BF16), w_ff2[0].astype(BF16), final_norm_g.reshape(1, D_MODEL).astype(F32),
             seq=seq, tm=512, ff_chunk=2048)
    return y.reshape(batch, seq, D_MODEL)
```

```python
import functools
import math

import numpy as np
import jax
import jax.numpy as jnp
from jax import lax
from jax.experimental import pallas as pl
from jax.experimental.pallas import tpu as pltpu

D_MODEL = 1024
GLA_WIDTH = 512
GLA_HEADS = 4
GLA_DK = 64
GLA_DV = 128
GLA_QK = GLA_HEADS * GLA_DK
GLA_RANK = 16
GLA_TAU = 16.0
GLA_CHUNK = 64
DSA_WIDTH = 512
DSA_HEADS = 8
DSA_DH = 64
DSA_PATTERN = ((128, 1), (512, 4), (2048, 16))
DSA_BLOCK = 128
REL_BUCKETS = 32
REL_MAX_DIST = 2048
D_FF = 4096
EPS = 1e-6
NEG = -1e30
LOG2E = math.log2(math.e)

LANES = 128
RANK_PAD = LANES
VMEM_LIMIT = 56 * 1024 * 1024

F32 = jnp.float32
BF16 = jnp.bfloat16

_NT = (((1,), (1,)), ((), ()))


def _dot(a, b):
    return jnp.dot(a, b, preferred_element_type=F32)


def _dot_nt(a, b):
    return lax.dot_general(a, b, _NT, preferred_element_type=F32)


def _split_hi_lo(x):
    hi = x.astype(BF16)
    lo = (x - hi.astype(F32)).astype(BF16)
    return hi, lo


def _log_sigmoid(x):
    return jnp.minimum(x, 0.0) - jnp.log1p(jnp.exp(-jnp.abs(x)))


def _bucket_tables():
    max_exact = REL_BUCKETS // 2
    L = DSA_BLOCK
    steps = L + np.arange(L)[:, None] - np.arange(2 * L)[None, :]
    tables = []
    for window, dilation in DSA_PATTERN:
        span = window // dilation
        in_band = (steps >= 0) & (steps <= span)
        n = np.maximum(steps * dilation, 0)
        large = max_exact + (np.log(np.maximum(n, 1) / max_exact)
                             / math.log(REL_MAX_DIST / max_exact)
                             * (REL_BUCKETS - max_exact)).astype(np.int32)
        large = np.minimum(large, REL_BUCKETS - 1)
        bucket = np.where(n < max_exact, n, large).astype(np.int32)
        tables.append(np.where(in_band, bucket, -1).astype(np.int32))
    return np.stack(tables)


def _bias_kernel(rb_ref, bt_ref, out_ref):
    for d in range(len(DSA_PATTERN)):
        bt = bt_ref[d]

        def one_head(h, carry, bt=bt, d=d):
            acc = jnp.full(bt.shape, NEG, F32)
            for b in range(REL_BUCKETS):
                acc = jnp.where(bt == b, rb_ref[b, h] * LOG2E, acc)
            out_ref[d, h] = acc
            return carry

        lax.fori_loop(0, DSA_HEADS, one_head, 0)


def _bias_tables(rel_bias):
    nb = len(DSA_PATTERN)
    L = DSA_BLOCK
    return pl.pallas_call(
        _bias_kernel,
        grid=(1,),
        in_specs=[pl.BlockSpec(memory_space=pltpu.SMEM),
                  pl.BlockSpec((nb, L, 2 * L), lambda i: (0, 0, 0))],
        out_specs=pl.BlockSpec((nb, DSA_HEADS, L, 2 * L), lambda i: (0, 0, 0, 0)),
        out_shape=jax.ShapeDtypeStruct((nb, DSA_HEADS, L, 2 * L), F32),
        name="dsa_bias",
    )(rel_bias.astype(F32), jnp.asarray(_bucket_tables()))


def _proj_kernel(x_ref, g_ref, wq_ref, wv_ref, wr_ref, wk_ref, wg_ref, w2_ref, b_ref, wd_ref,
                 gq_ref, gv_ref, gr_ref, d1_ref, kt_ref, glogt_ref, d4_ref, d16_ref,
                 kt1_ref, kt4_ref, dsc_ref, t4_ref):
    tm = x_ref.shape[0]
    ncol = 3 * DSA_WIDTH // LANES
    kcols = range(DSA_WIDTH // LANES, 2 * DSA_WIDTH // LANES)
    n4 = tm // 4
    n16 = tm // 16

    x = x_ref[...]
    ms = jnp.mean(x * x, axis=-1, keepdims=True)
    nx = (x * lax.rsqrt(ms + EPS) * g_ref[...]).astype(BF16)

    for j in range(ncol // 2):
        res = _dot(nx, wd_ref[:, 2 * j * LANES:(2 * j + 2) * LANES])
        halves = {2 * j: res[:, :LANES], 2 * j + 1: res[:, LANES:]}
        for c in halves:
            dsc_ref[c % 2] = halves[c]
        for c in halves:
            sl = c % 2
            cols = slice(c * LANES, (c + 1) * LANES)
            is_key = c in kcols
            krows = slice((c - kcols[0]) * LANES, (c - kcols[0] + 1) * LANES)
            qv = c if c < kcols[0] else c - len(kcols)
            qv_cols = slice(qv * LANES, (qv + 1) * LANES)
            if is_key:
                for jb in range(tm // DSA_BLOCK):
                    blk_rows = slice(jb * DSA_BLOCK, (jb + 1) * DSA_BLOCK)
                    kt1_ref[jb, krows, :] = dsc_ref[sl, blk_rows, :].T.astype(BF16)
            else:
                d1_ref[:, qv_cols] = halves[c].astype(BF16)
            for r4 in range(4):
                sub = dsc_ref[sl, pl.ds(r4, n4, stride=4), :]
                t4_ref[sl, r4 * n4:(r4 + 1) * n4, :] = sub
                if is_key:
                    for jb in range(n4 // DSA_BLOCK):
                        blk_rows = slice(jb * DSA_BLOCK, (jb + 1) * DSA_BLOCK)
                        kt4_ref[jb, r4, krows, :] = sub[blk_rows, :].T.astype(BF16)
                else:
                    d4_ref[r4, :, qv_cols] = sub.astype(BF16)
            for r4 in range(4):
                for r2 in range(4):
                    d16_ref[r4 + 4 * r2, :, cols] = (
                        t4_ref[sl, pl.ds(r4 * n4 + r2, n16, stride=4), :].astype(BF16))

    gq_ref[...] = _dot(nx, wq_ref[...]).astype(BF16)

    kt_ref[0] = _dot(nx, wk_ref[...]).T.astype(BF16)
    glow = _dot(nx, wg_ref[...]).astype(BF16)
    glog = _log_sigmoid(_dot(glow, w2_ref[...]) + b_ref[...]) * (LOG2E / GLA_TAU)
    glogt_ref[0] = glog.T

    gv_ref[...] = _dot(nx, wv_ref[...]).astype(BF16)
    gr_ref[...] = _dot(nx, wr_ref[...]).astype(BF16)


def _proj(x2, g, wq, wv, wr, wk, wg, w2, b, wd, *, batch, seq, tm):
    T = batch * seq
    spb = seq // tm
    nt = T // tm
    full = lambda a: pl.BlockSpec(a.shape, lambda s: (0,) * a.ndim, pipeline_mode=pl.Buffered(1))
    row_cur = lambda n: pl.BlockSpec((tm, n), lambda s: (s, 0))
    colt = pl.BlockSpec((1, GLA_QK, tm), lambda s: (s // spb, 0, s % spb))
    W, L = DSA_WIDTH, DSA_BLOCK
    W2, W3 = 2 * W, 3 * W
    strided = lambda d, n: pl.BlockSpec((None, d, tm // d, n), lambda s: (s // spb, 0, s % spb, 0))
    assert tm % (4 * L) == 0
    kt1_spec = pl.BlockSpec((None, tm // L, None, W, L), lambda s: (s // spb, s % spb, 0, 0, 0))
    kt4_spec = pl.BlockSpec((None, tm // (4 * L), 4, W, L), lambda s: (s // spb, s % spb, 0, 0, 0))
    return pl.pallas_call(
        _proj_kernel,
        grid=(nt,),
        in_specs=[row_cur(D_MODEL)] + [full(a) for a in (g, wq, wv, wr, wk, wg, w2, b, wd)],
        out_specs=[row_cur(GLA_QK), row_cur(GLA_WIDTH), row_cur(GLA_WIDTH), row_cur(W2),
                   colt, colt, strided(4, W2), strided(16, W3), kt1_spec, kt4_spec],
        out_shape=[jax.ShapeDtypeStruct((T, GLA_QK), BF16),
                   jax.ShapeDtypeStruct((T, GLA_WIDTH), BF16),
                   jax.ShapeDtypeStruct((T, GLA_WIDTH), BF16),
                   jax.ShapeDtypeStruct((T, W2), BF16),
                   jax.ShapeDtypeStruct((batch, GLA_QK, seq), BF16),
                   jax.ShapeDtypeStruct((batch, GLA_QK, seq), F32),
                   jax.ShapeDtypeStruct((batch, 4, seq // 4, W2), BF16),
                   jax.ShapeDtypeStruct((batch, 16, seq // 16, W3), BF16),
                   jax.ShapeDtypeStruct((batch, seq // L, 1, W, L), BF16),
                   jax.ShapeDtypeStruct((batch, seq // (4 * L), 4, W, L), BF16)],
        scratch_shapes=[pltpu.VMEM((2, tm, LANES), F32)] * 2,
        compiler_params=pltpu.CompilerParams(dimension_semantics=("arbitrary",),
                                             vmem_limit_bytes=VMEM_LIMIT),
        name="proj",
    )(x2, g, wq, wv, wr, wk, wg, w2, b, wd)


def _gla_kernel(gq_ref, kt_ref, gv_ref, gr_ref, glogt_ref, gn_ref, o_ref, s_ref, *, pairs):
    C = GLA_CHUNK
    P = 2 * C

    @pl.when(pl.program_id(1) == 0)
    def _():
        s_ref[...] = jnp.zeros_like(s_ref)

    ri = lax.broadcasted_iota(jnp.int32, (P, P), 0)
    ci = lax.broadcasted_iota(jnp.int32, (P, P), 1)
    same_chunk = (ri < C) == (ci < C)
    causal = same_chunk & (ci <= ri)
    upp = jnp.where(same_chunk & (ri <= ci), 1.0, 0.0).astype(BF16)
    lane = lax.broadcasted_iota(jnp.int32, (P, LANES), 1)
    first_half = lane < C
    lane_t = lax.broadcasted_iota(jnp.int32, (GLA_QK, P), 1)
    first_t = lane_t < C

    zero16 = jnp.zeros((), BF16)
    state = [s_ref[h * GLA_DK:(h + 1) * GLA_DK, :] for h in range(GLA_HEADS)]

    heads = range(GLA_HEADS)
    hk = [slice(h * GLA_DK, (h + 1) * GLA_DK) for h in heads]
    hv = [slice(h * GLA_DV, (h + 1) * GLA_DV) for h in heads]
    grp = [slice((h // 2) * LANES, (h // 2 + 1) * LANES) for h in heads]

    def decay_stage(p):
        rows = slice(p * P, (p + 1) * P)
        gt = glogt_ref[0, :, rows]
        gt_hi, gt_lo = _split_hi_lo(gt)
        bt = _dot(gt_hi, upp) + _dot(gt_lo, upp)
        b = bt.T
        tot_a = bt[:, C - 1:C]
        tot_b = bt[:, P - 1:P]
        return dict(rows=rows, b=b, bt=bt, dec_a=jnp.exp2(tot_a), dec_b=jnp.exp2(tot_b))

    def score_stage(c):
        rows = c["rows"]
        qd = (gq_ref[rows, :].astype(F32) * jnp.exp2(c["b"])).astype(BF16)
        kinv_t = (kt_ref[0, :, rows].astype(F32) * jnp.exp2(-c["bt"])).astype(BF16)
        kinv_a = jnp.where(first_t, kinv_t, zero16)
        kinv_b = jnp.where(first_t, zero16, kinv_t)
        c["v"] = [gv_ref[rows, hv[h]] for h in heads]
        c["qm"] = [jnp.where(first_half if h % 2 == 0 else ~first_half, qd[:, grp[h]], zero16) for h in heads]
        c["att"] = [_dot(c["qm"][h], kinv_t[grp[h], :]) for h in heads]
        c["upd_a"] = [_dot(kinv_a[hk[h], :], c["v"][h]) for h in heads]
        c["upd_b"] = [_dot(kinv_b[hk[h], :], c["v"][h]) for h in heads]

    def output_stage(c):
        o = []
        s_a = list(state)
        s_b = [c["dec_a"][hk[h], :] * (s_a[h] + c["upd_a"][h]) for h in heads]
        for h in heads:
            state[h] = c["dec_b"][hk[h], :] * (s_b[h] + c["upd_b"][h])
        for h in heads:
            att = jnp.where(causal, c["att"][h], 0.0).astype(BF16)
            pair_a = [s_a[h], s_a[h ^ 1]] if h % 2 == 0 else [s_a[h ^ 1], s_a[h]]
            pair_b = [s_b[h], s_a[h ^ 1]] if h % 2 == 0 else [s_a[h ^ 1], s_b[h]]
            o_a = _dot(c["qm"][h][:C, :], jnp.concatenate(pair_a, axis=0).astype(BF16))
            o_b = _dot(c["qm"][h][C:, :], jnp.concatenate(pair_b, axis=0).astype(BF16))
            o.append(_dot(att, c["v"][h]) + jnp.concatenate([o_a, o_b], axis=0))
        c["o"] = o

    def norm_stage(c):
        rows = c["rows"]
        for h in heads:
            o = c["o"][h]
            o = o * lax.rsqrt(jnp.mean(o * o, axis=-1, keepdims=True) + EPS) * gn_ref[:, hv[h]]
            r = gr_ref[rows, hv[h]].astype(F32)
            o_ref[rows, hv[h]] = (o * (r * jax.nn.sigmoid(r))).astype(BF16)

    ctx = {}
    for t in range(pairs + 3):
        if t < pairs:
            ctx[t] = decay_stage(t)
        if 0 <= t - 1 < pairs:
            score_stage(ctx[t - 1])
        if 0 <= t - 2 < pairs:
            output_stage(ctx[t - 2])
        if 0 <= t - 3 < pairs:
            norm_stage(ctx.pop(t - 3))

    for h in heads:
        s_ref[hk[h], :] = state[h]


def _gla(gq, kt, gv, gr, glogt, gn, *, batch, seq, tg):
    T = batch * seq
    spb = seq // tg
    row = lambda n: pl.BlockSpec((tg, n), lambda b, i: (b * spb + i, 0))
    colt = pl.BlockSpec((1, GLA_QK, tg), lambda b, i: (b, 0, i))
    return pl.pallas_call(
        functools.partial(_gla_kernel, pairs=tg // (2 * GLA_CHUNK)),
        grid=(batch, spb),
        in_specs=[row(GLA_QK), colt, row(GLA_WIDTH), row(GLA_WIDTH), colt,
                  pl.BlockSpec((1, GLA_WIDTH), lambda b, i: (0, 0))],
        out_specs=row(GLA_WIDTH),
        out_shape=jax.ShapeDtypeStruct((T, GLA_WIDTH), BF16),
        scratch_shapes=[pltpu.VMEM((GLA_QK, GLA_DV), F32)],
        compiler_params=pltpu.CompilerParams(dimension_semantics=("arbitrary", "arbitrary"),
                                             vmem_limit_bytes=VMEM_LIMIT),
        name="gla",
    )(gq, kt, gv, gr, glogt, gn)


def _dsa_kernel(q_ref, k_ref, v_ref, kh_ref, vh_ref, bias_ref, o_ref, st_ref, *, res, blocks, keys_transposed):
    L = DSA_BLOCK
    first_tile = pl.program_id(2) == 0
    lane = lax.broadcasted_iota(jnp.int32, (L, LANES), 1)
    first_half = lane < DSA_DH
    prev_cols = lax.broadcasted_iota(jnp.int32, (L, 2 * L), 1) < L

    st_tiles = {}
    pair_out = {}

    def score_stage(r, blk, h):
        rows = slice(blk * L, (blk + 1) * L)
        both = slice((blk - 1) * L, (blk + 1) * L)
        grp = slice((h // 2) * LANES, (h // 2 + 1) * LANES)
        if keys_transposed:
            k_prev = kh_ref[r, grp, :] if blk == 0 else k_ref[blk - 1, r, grp, :]
            kcat = jnp.concatenate([k_prev, k_ref[blk, r, grp, :]], axis=1)
        elif blk == 0:
            kcat = jnp.concatenate([kh_ref[r, :, grp], k_ref[r, rows, grp]], axis=0)
        else:
            kcat = k_ref[r, both, grp]
        own = first_half if h % 2 == 0 else ~first_half
        qm = jnp.where(own, q_ref[r, rows, grp], jnp.zeros((), BF16))
        return _dot(qm, kcat) if keys_transposed else _dot_nt(qm, kcat)

    def value_stage(r, blk, h, s):
        rows = slice(blk * L, (blk + 1) * L)
        grp = slice((h // 2) * LANES, (h // 2 + 1) * LANES)
        bias = bias_ref[h]
        if blk == 0:
            bias = jnp.where(prev_cols & first_tile, NEG, bias)
        s = s + bias
        m = jnp.max(s, axis=-1, keepdims=True)
        p = jnp.exp2(s - m)
        den = jnp.sum(p, axis=-1, keepdims=True)
        st_old = st_tiles.get((r, blk), jnp.zeros((L, LANES), F32))
        st_tiles[r, blk] = jnp.where(lane == h, m, jnp.where(lane == DSA_HEADS + h, den, st_old))
        if blk == 0:
            vcat = jnp.concatenate([vh_ref[r, :, grp], v_ref[r, rows, grp]], axis=0)
        else:
            vcat = v_ref[r, both_rows(blk), grp]
        out = _dot(p.astype(BF16), vcat)
        if h % 2 == 0:
            pair_out[r, blk] = out
        else:
            o_ref[r, rows, grp] = jnp.where(first_half, pair_out.pop((r, blk)), out).astype(BF16)
        if h == DSA_HEADS - 1:
            st_ref[r, rows, :] = st_tiles.pop((r, blk))

    def both_rows(blk):
        return slice((blk - 1) * L, (blk + 1) * L)

    items = [(r, blk, h) for r in range(res) for blk in range(blocks) for h in range(DSA_HEADS)]
    lag = 1
    pending = {}
    for t in range(len(items) + lag):
        if t < len(items):
            pending[t] = score_stage(*items[t])
        if t - lag >= 0:
            value_stage(*items[t - lag], pending.pop(t - lag))


def _dsa_branch(x, kt, bias, branch, *, rows_per_step):
    batch, d, n, _ = x.shape
    tr = min(rows_per_step, n)
    res = min(rows_per_step // tr, d)
    hb = tr // DSA_BLOCK
    W = DSA_WIDTH
    prev_blk = lambda i: jnp.maximum(i * hb - 1, 0)
    blk = lambda part: pl.BlockSpec((None, res, tr, W), lambda b, r, i: (b, r, i, part))
    halo = lambda part: pl.BlockSpec((None, res, DSA_BLOCK, W), lambda b, r, i: (b, r, prev_blk(i), part))
    if kt is None:
        v_part = 2
        k_arg, k_spec, kh_spec = x, blk(1), halo(1)
    else:
        v_part = 1
        k_arg = kt
        k_spec = pl.BlockSpec((None, hb, res, W, DSA_BLOCK), lambda b, r, i: (b, i, r, 0, 0))
        kh_spec = pl.BlockSpec((None, None, res, W, DSA_BLOCK), lambda b, r, i: (b, prev_blk(i), r, 0, 0))
    return pl.pallas_call(
        functools.partial(_dsa_kernel, res=res, blocks=hb, keys_transposed=kt is not None),
        grid=(batch, d // res, n // tr),
        in_specs=[blk(0), k_spec, blk(v_part), kh_spec, halo(v_part),
                  pl.BlockSpec((None, DSA_HEADS, DSA_BLOCK, 2 * DSA_BLOCK), lambda b, r, i: (branch, 0, 0, 0))],
        out_specs=[pl.BlockSpec((None, res, tr, W), lambda b, r, i: (b, r, i, 0)),
                   pl.BlockSpec((None, res, tr, LANES), lambda b, r, i: (b, r, i, 0))],
        out_shape=[jax.ShapeDtypeStruct((batch, d, n, W), BF16),
                   jax.ShapeDtypeStruct((batch, d, n, LANES), F32)],
        compiler_params=pltpu.CompilerParams(dimension_semantics=("arbitrary",) * 3,
                                             vmem_limit_bytes=VMEM_LIMIT),
        name=f"dsa_d{d}",
    )(x, k_arg, x, k_arg, x, bias)


def _out_kernel(x_ref, oa_ref, o1_ref, o4_ref, o16_ref, l1_ref, l4_ref, l16_ref,
                f1_ref, f4_ref, f16_ref, g1_ref, g4_ref, g16_ref,
                wo_ref, g2_ref, w1_ref, w2_ref, gf_ref, y_ref, osc_ref, lsc_ref, ob_ref, tsc_ref, *, ff_chunk):
    tm = x_ref.shape[0]
    npair = DSA_HEADS // 2
    lane = lax.broadcasted_iota(jnp.int32, (tm, LANES), 1)
    first_half = lane < DSA_DH

    def combine(o1_ref, o4_ref, o16_ref, l1_ref, l4_ref, l16_ref):
        n4, n16 = tm // 4, tm // 16
        for r4 in range(4):
            for r2 in range(4):
                dst = pl.ds(r4 * n4 + r2, n16, stride=4)
                tsc_ref[npair, dst, :] = l16_ref[r4 + 4 * r2]
                for hp in range(npair):
                    tsc_ref[hp, dst, :] = o16_ref[r4 + 4 * r2, :, hp * LANES:(hp + 1) * LANES].astype(F32)
        for r4 in range(4):
            dst = pl.ds(r4, n4, stride=4)
            src = slice(r4 * n4, (r4 + 1) * n4)
            lsc_ref[0, dst, :] = l4_ref[r4]
            lsc_ref[1, dst, :] = tsc_ref[npair, src, :]
            for hp in range(npair):
                osc_ref[hp, dst, :] = o4_ref[r4, :, hp * LANES:(hp + 1) * LANES].astype(F32)
                osc_ref[npair + hp, dst, :] = tsc_ref[hp, src, :]

        sts = (l1_ref[...], lsc_ref[0], lsc_ref[1])
        m = jnp.maximum(jnp.maximum(sts[0], sts[1]), sts[2])
        es = [jnp.exp2(st - m) for st in sts]
        total = sum(e * pltpu.roll(st, LANES - DSA_HEADS, axis=1) for e, st in zip(es, sts))
        inv = 1.0 / jnp.where(lane < DSA_HEADS, total, 1.0)
        ws = [e * inv for e in es]

        for hp in range(npair):
            grp = slice(hp * LANES, (hp + 1) * LANES)
            branch_o = (o1_ref[:, grp].astype(F32), osc_ref[hp], osc_ref[npair + hp])
            acc = jnp.zeros((tm, LANES), F32)
            for w, o in zip(ws, branch_o):
                wa = jnp.sum(jnp.where(lane == 2 * hp, w, 0.0), axis=-1, keepdims=True)
                wb = jnp.sum(jnp.where(lane == 2 * hp + 1, w, 0.0), axis=-1, keepdims=True)
                acc = acc + jnp.where(first_half, wa, wb) * o
            ob_ref[:, grp] = acc.astype(BF16)

    @pl.when(pl.program_id(0) == 0)
    def _():
        combine(f1_ref, f4_ref, f16_ref, g1_ref, g4_ref, g16_ref)

    mixed = _dot(oa_ref[...], wo_ref[:GLA_WIDTH, :]) + _dot(ob_ref[...], wo_ref[GLA_WIDTH:, :])
    h = x_ref[...] + mixed
    nm = (h * lax.rsqrt(jnp.mean(h * h, axis=-1, keepdims=True) + EPS) * g2_ref[...]).astype(BF16)
    ff = None
    for c in range(D_FF // ff_chunk):
        cols = slice(c * ff_chunk, (c + 1) * ff_chunk)
        a = jnp.maximum(_dot(nm, w1_ref[:, cols]), 0.0)
        d = _dot((a * a).astype(BF16), w2_ref[cols, :])
        ff = d if ff is None else ff + d
    h = h + ff
    y_ref[...] = h * lax.rsqrt(jnp.mean(h * h, axis=-1, keepdims=True) + EPS) * gf_ref[...]

    combine(o1_ref, o4_ref, o16_ref, l1_ref, l4_ref, l16_ref)


def _out(x2, oa, os_, ls, wo, g2, w1, w2, gf, *, seq, tm, ff_chunk):
    T = x2.shape[0]
    spb = seq // tm
    nt = T // tm
    nxt = lambda s: jnp.minimum(s + 1, nt - 1)
    one = pl.Buffered(1)
    row = lambda n: pl.BlockSpec((tm, n), lambda s: (s, 0))
    row_nxt = lambda n: pl.BlockSpec((tm, n), lambda s: (nxt(s), 0))
    row_first = lambda n: pl.BlockSpec((tm, n), lambda s: (0, 0), pipeline_mode=one)
    strided = lambda d, n: pl.BlockSpec((None, d, tm // d, n), lambda s: (nxt(s) // spb, 0, nxt(s) % spb, 0))
    strided_first = lambda d, n: pl.BlockSpec((None, d, tm // d, n), lambda s: (0, 0, 0, 0), pipeline_mode=one)
    const = lambda a: pl.BlockSpec(a.shape, lambda s: (0,) * a.ndim, pipeline_mode=one)
    W = DSA_WIDTH
    return pl.pallas_call(
        functools.partial(_out_kernel, ff_chunk=ff_chunk),
        grid=(nt,),
        in_specs=[row(D_MODEL), row(GLA_WIDTH), row_nxt(W), strided(4, W), strided(16, W),
                  row_nxt(LANES), strided(4, LANES), strided(16, LANES),
                  row_first(W), strided_first(4, W), strided_first(16, W),
                  row_first(LANES), strided_first(4, LANES), strided_first(16, LANES),
                  const(wo), const(g2), const(w1), const(w2), const(gf)],
        out_specs=row(D_MODEL),
        out_shape=jax.ShapeDtypeStruct((T, D_MODEL), F32),
        scratch_shapes=[pltpu.VMEM((2 * (W // LANES), tm, LANES), F32),
                        pltpu.VMEM((2, tm, LANES), F32),
                        pltpu.VMEM((tm, W), BF16),
                        pltpu.VMEM((W // LANES + 1, tm, LANES), F32)],
        compiler_params=pltpu.CompilerParams(dimension_semantics=("arbitrary",),
                                             vmem_limit_bytes=VMEM_LIMIT),
        name="out_mlp",
    )(x2, oa, *os_, *ls, *os_, *ls, wo, g2, w1, w2, gf)


_IN_SPLITS = (GLA_QK, GLA_QK, GLA_WIDTH, GLA_WIDTH, GLA_RANK, DSA_WIDTH, DSA_WIDTH, DSA_WIDTH)
_IN_OFFSETS = tuple(int(v) for v in np.cumsum((0,) + _IN_SPLITS))


def _split_w_in_kernel(wt_ref, wq_ref, wk_ref, wv_ref, wr_ref, wg_ref, wd_ref):
    o = _IN_OFFSETS
    lane = lax.broadcasted_iota(jnp.int32, (wt_ref.shape[1], RANK_PAD), 1)

    def piece(start):
        return wt_ref[start:start + LANES, :].T

    for dst, start, width, scale in ((wq_ref, o[0], GLA_QK, GLA_DK ** -0.5), (wk_ref, o[1], GLA_QK, None),
                                     (wv_ref, o[2], GLA_WIDTH, None), (wr_ref, o[3], GLA_WIDTH, None),
                                     (wd_ref, o[5], 3 * DSA_WIDTH, None)):
        for j in range(width // LANES):
            blk = piece(start + j * LANES)
            if scale is not None:
                blk = blk * scale
            if dst is wd_ref and j < DSA_WIDTH // LANES:
                blk = blk * (DSA_DH ** -0.5 * LOG2E)
            dst[:, j * LANES:(j + 1) * LANES] = blk.astype(BF16)
    wg_ref[...] = jnp.where(lane < GLA_RANK, piece(o[4]), 0.0).astype(BF16)


def _split_w_in(wt):
    _, N, K = wt.shape
    shapes = [(K, GLA_QK), (K, GLA_QK), (K, GLA_WIDTH), (K, GLA_WIDTH), (K, RANK_PAD), (K, 3 * DSA_WIDTH)]
    return pl.pallas_call(
        _split_w_in_kernel,
        grid=(1,),
        in_specs=[pl.BlockSpec((None, N, K), lambda i: (0, 0, 0))],
        out_specs=[pl.BlockSpec(s, lambda i: (0, 0)) for s in shapes],
        out_shape=[jax.ShapeDtypeStruct(s, BF16) for s in shapes],
        compiler_params=pltpu.CompilerParams(vmem_limit_bytes=VMEM_LIMIT),
        name="split_w_in",
    )(wt)


def kernel(x, attn_norm_g, w_in, gla_gate_w2, gla_gate_b, gla_norm_g, rel_bias, w_out, mlp_norm_g,
           w_ff1, w_ff2, final_norm_g):
    batch, seq, _ = x.shape
    assert seq % (DSA_PATTERN[-1][1] * DSA_BLOCK) == 0
    T = batch * seq
    x2 = x.reshape(T, D_MODEL)

    wq, wk, wv, wr, wg, wd = _split_w_in(jnp.swapaxes(w_in, 1, 2))
    w2 = jnp.pad(gla_gate_w2[0], ((0, RANK_PAD - GLA_RANK), (0, 0))).astype(BF16)
    gb = gla_gate_b[0].astype(F32).reshape(1, GLA_QK)

    gq, gv, gr, d1, kt, glogt, d4, d16, kt1, kt4 = _proj(
        x2, attn_norm_g[0].reshape(1, D_MODEL).astype(F32), wq, wv, wr, wk, wg, w2, gb, wd,
        batch=batch, seq=seq, tm=1024)

    o_a = _gla(gq, kt, gv, gr, glogt, gla_norm_g[0].reshape(1, GLA_WIDTH).astype(F32),
               batch=batch, seq=seq, tg=1024)

    bias = _bias_tables(rel_bias)
    os_, ls = [], []
    branches = ((d1.reshape(batch, 1, seq, 2 * DSA_WIDTH), kt1), (d4, kt4), (d16, None))
    for branch, (xd, ktd) in enumerate(branches):
        o, lse = _dsa_branch(xd, ktd, bias, branch, rows_per_step=2048)
        os_.append(o)
        ls.append(lse)
    os_[0] = os_[0].reshape(T, DSA_WIDTH)
    ls[0] = ls[0].reshape(T, LANES)

    y = _out(x2, o_a, os_, ls, w_out[0].astype(BF16), mlp_norm_g[0].reshape(1, D_MODEL).astype(F32),
             w_ff1[0].astype(BF16), w_ff2[0].astype(BF16), final_norm_g.reshape(1, D_MODEL).astype(F32),
             seq=seq, tm=512, ff_chunk=2048)
    return y.reshape(batch, seq, D_MODEL)
```

```python
import functools
import math

import numpy as np
import jax
import jax.numpy as jnp
from jax import lax
from jax.experimental import pallas as pl
from jax.experimental.pallas import tpu as pltpu

D_MODEL = 1024
GLA_WIDTH = 512
GLA_HEADS = 4
GLA_DK = 64
GLA_DV = 128
GLA_QK = GLA_HEADS * GLA_DK
GLA_RANK = 16
GLA_TAU = 16.0
GLA_CHUNK = 64
DSA_WIDTH = 512
DSA_HEADS = 8
DSA_DH = 64
DSA_PATTERN = ((128, 1), (512, 4), (2048, 16))
DSA_BLOCK = 128
REL_BUCKETS = 32
REL_MAX_DIST = 2048
D_FF = 4096
EPS = 1e-6
NEG = -1e30
LOG2E = math.log2(math.e)

LANES = 128
RANK_PAD = LANES
VMEM_LIMIT = 56 * 1024 * 1024

F32 = jnp.float32
BF16 = jnp.bfloat16

_NT = (((1,), (1,)), ((), ()))


def _dot(a, b):
    return jnp.dot(a, b, preferred_element_type=F32)


def _dot_nt(a, b):
    return lax.dot_general(a, b, _NT, preferred_element_type=F32)


def _split_hi_lo(x):
    hi = x.astype(BF16)
    lo = (x - hi.astype(F32)).astype(BF16)
    return hi, lo


def _log_sigmoid(x):
    return jnp.minimum(x, 0.0) - jnp.log1p(jnp.exp(-jnp.abs(x)))


def _bucket_tables():
    max_exact = REL_BUCKETS // 2
    L = DSA_BLOCK
    steps = L + np.arange(L)[:, None] - np.arange(2 * L)[None, :]
    tables = []
    for window, dilation in DSA_PATTERN:
        span = window // dilation
        in_band = (steps >= 0) & (steps <= span)
        n = np.maximum(steps * dilation, 0)
        large = max_exact + (np.log(np.maximum(n, 1) / max_exact)
                             / math.log(REL_MAX_DIST / max_exact)
                             * (REL_BUCKETS - max_exact)).astype(np.int32)
        large = np.minimum(large, REL_BUCKETS - 1)
        bucket = np.where(n < max_exact, n, large).astype(np.int32)
        tables.append(np.where(in_band, bucket, -1).astype(np.int32))
    return np.stack(tables)


def _bias_kernel(rb_ref, bt_ref, out_ref):
    for d in range(len(DSA_PATTERN)):
        bt = bt_ref[d]

        def one_head(h, carry, bt=bt, d=d):
            acc = jnp.full(bt.shape, NEG, F32)
            for b in range(REL_BUCKETS):
                acc = jnp.where(bt == b, rb_ref[b, h] * LOG2E, acc)
            out_ref[d, h] = acc
            return carry

        lax.fori_loop(0, DSA_HEADS, one_head, 0)


def _bias_tables(rel_bias):
    nb = len(DSA_PATTERN)
    L = DSA_BLOCK
    return pl.pallas_call(
        _bias_kernel,
        grid=(1,),
        in_specs=[pl.BlockSpec(memory_space=pltpu.SMEM),
                  pl.BlockSpec((nb, L, 2 * L), lambda i: (0, 0, 0))],
        out_specs=pl.BlockSpec((nb, DSA_HEADS, L, 2 * L), lambda i: (0, 0, 0, 0)),
        out_shape=jax.ShapeDtypeStruct((nb, DSA_HEADS, L, 2 * L), F32),
        name="dsa_bias",
    )(rel_bias.astype(F32), jnp.asarray(_bucket_tables()))


def _proj_kernel(x_ref, g_ref, wq_ref, wv_ref, wr_ref, wk_ref, wg_ref, w2_ref, b_ref, wd_ref,
                 gq_ref, gv_ref, gr_ref, d1_ref, kt_ref, glogt_ref, d4_ref, d16_ref,
                 kt1_ref, kt4_ref, dsc_ref, t4_ref):
    tm = x_ref.shape[0]
    ncol = 3 * DSA_WIDTH // LANES
    kcols = range(DSA_WIDTH // LANES, 2 * DSA_WIDTH // LANES)
    n4 = tm // 4
    n16 = tm // 16

    x = x_ref[...]
    ms = jnp.mean(x * x, axis=-1, keepdims=True)
    nx = (x * lax.rsqrt(ms + EPS) * g_ref[...]).astype(BF16)

    for j in range(ncol // 2):
        res = _dot(nx, wd_ref[:, 2 * j * LANES:(2 * j + 2) * LANES])
        halves = {2 * j: res[:, :LANES], 2 * j + 1: res[:, LANES:]}
        for c in halves:
            dsc_ref[c % 2] = halves[c]
        for c in halves:
            sl = c % 2
            cols = slice(c * LANES, (c + 1) * LANES)
            is_key = c in kcols
            krows = slice((c - kcols[0]) * LANES, (c - kcols[0] + 1) * LANES)
            qv = c if c < kcols[0] else c - len(kcols)
            qv_cols = slice(qv * LANES, (qv + 1) * LANES)
            if is_key:
                for jb in range(tm // DSA_BLOCK):
                    blk_rows = slice(jb * DSA_BLOCK, (jb + 1) * DSA_BLOCK)
                    kt1_ref[jb, krows, :] = dsc_ref[sl, blk_rows, :].T.astype(BF16)
            else:
                d1_ref[:, qv_cols] = halves[c].astype(BF16)
            for r4 in range(4):
                sub = dsc_ref[sl, pl.ds(r4, n4, stride=4), :]
                t4_ref[sl, r4 * n4:(r4 + 1) * n4, :] = sub
                if is_key:
                    for jb in range(n4 // DSA_BLOCK):
                        blk_rows = slice(jb * DSA_BLOCK, (jb + 1) * DSA_BLOCK)
                        kt4_ref[jb, r4, krows, :] = sub[blk_rows, :].T.astype(BF16)
                else:
                    d4_ref[r4, :, qv_cols] = sub.astype(BF16)
            for r4 in range(4):
                for r2 in range(4):
                    d16_ref[r4 + 4 * r2, :, cols] = (
                        t4_ref[sl, pl.ds(r4 * n4 + r2, n16, stride=4), :].astype(BF16))

    gq_ref[...] = _dot(nx, wq_ref[...]).astype(BF16)

    kt_ref[0] = _dot(nx, wk_ref[...]).T.astype(BF16)
    glow = _dot(nx, wg_ref[...]).astype(BF16)
    glog = _log_sigmoid(_dot(glow, w2_ref[...]) + b_ref[...]) * (LOG2E / GLA_TAU)
    glogt_ref[0] = glog.T

    gv_ref[...] = _dot(nx, wv_ref[...]).astype(BF16)
    gr_ref[...] = _dot(nx, wr_ref[...]).astype(BF16)


def _proj(x2, g, wq, wv, wr, wk, wg, w2, b, wd, *, batch, seq, tm):
    T = batch * seq
    spb = seq // tm
    nt = T // tm
    full = lambda a: pl.BlockSpec(a.shape, lambda s: (0,) * a.ndim, pipeline_mode=pl.Buffered(1))
    row_cur = lambda n: pl.BlockSpec((tm, n), lambda s: (s, 0))
    colt = pl.BlockSpec((1, GLA_QK, tm), lambda s: (s // spb, 0, s % spb))
    W, L = DSA_WIDTH, DSA_BLOCK
    W2, W3 = 2 * W, 3 * W
    strided = lambda d, n: pl.BlockSpec((None, d, tm // d, n), lambda s: (s // spb, 0, s % spb, 0))
    assert tm % (4 * L) == 0
    kt1_spec = pl.BlockSpec((None, tm // L, None, W, L), lambda s: (s // spb, s % spb, 0, 0, 0))
    kt4_spec = pl.BlockSpec((None, tm // (4 * L), 4, W, L), lambda s: (s // spb, s % spb, 0, 0, 0))
    return pl.pallas_call(
        _proj_kernel,
        grid=(nt,),
        in_specs=[row_cur(D_MODEL)] + [full(a) for a in (g, wq, wv, wr, wk, wg, w2, b, wd)],
        out_specs=[row_cur(GLA_QK), row_cur(GLA_WIDTH), row_cur(GLA_WIDTH), row_cur(W2),
                   colt, colt, strided(4, W2), strided(16, W3), kt1_spec, kt4_spec],
        out_shape=[jax.ShapeDtypeStruct((T, GLA_QK), BF16),
                   jax.ShapeDtypeStruct((T, GLA_WIDTH), BF16),
                   jax.ShapeDtypeStruct((T, GLA_WIDTH), BF16),
                   jax.ShapeDtypeStruct((T, W2), BF16),
                   jax.ShapeDtypeStruct((batch, GLA_QK, seq), BF16),
                   jax.ShapeDtypeStruct((batch, GLA_QK, seq), F32),
                   jax.ShapeDtypeStruct((batch, 4, seq // 4, W2), BF16),
                   jax.ShapeDtypeStruct((batch, 16, seq // 16, W3), BF16),
                   jax.ShapeDtypeStruct((batch, seq // L, 1, W, L), BF16),
                   jax.ShapeDtypeStruct((batch, seq // (4 * L), 4, W, L), BF16)],
        scratch_shapes=[pltpu.VMEM((2, tm, LANES), F32)] * 2,
        compiler_params=pltpu.CompilerParams(dimension_semantics=("arbitrary",),
                                             vmem_limit_bytes=VMEM_LIMIT),
        name="proj",
    )(x2, g, wq, wv, wr, wk, wg, w2, b, wd)


def _gla_kernel(gq_ref, kt_ref, gv_ref, gr_ref, glogt_ref, gn_ref, o_ref, s_ref, *, pairs):
    C = GLA_CHUNK
    P = 2 * C

    @pl.when(pl.program_id(1) == 0)
    def _():
        s_ref[...] = jnp.zeros_like(s_ref)

    ri = lax.broadcasted_iota(jnp.int32, (P, P), 0)
    ci = lax.broadcasted_iota(jnp.int32, (P, P), 1)
    same_chunk = (ri < C) == (ci < C)
    causal = same_chunk & (ci <= ri)
    upp = jnp.where(same_chunk & (ri <= ci), 1.0, 0.0).astype(BF16)
    lane = lax.broadcasted_iota(jnp.int32, (P, LANES), 1)
    first_half = lane < C
    lane_t = lax.broadcasted_iota(jnp.int32, (GLA_QK, P), 1)
    first_t = lane_t < C

    zero16 = jnp.zeros((), BF16)
    state = [s_ref[h * GLA_DK:(h + 1) * GLA_DK, :] for h in range(GLA_HEADS)]

    heads = range(GLA_HEADS)
    hk = [slice(h * GLA_DK, (h + 1) * GLA_DK) for h in heads]
    hv = [slice(h * GLA_DV, (h + 1) * GLA_DV) for h in heads]
    grp = [slice((h // 2) * LANES, (h // 2 + 1) * LANES) for h in heads]

    def decay_stage(p):
        rows = slice(p * P, (p + 1) * P)
        gt = glogt_ref[0, :, rows]
        gt_hi, gt_lo = _split_hi_lo(gt)
        bt = _dot(gt_hi, upp) + _dot(gt_lo, upp)
        b = bt.T
        tot_a = bt[:, C - 1:C]
        tot_b = bt[:, P - 1:P]
        return dict(rows=rows, b=b, bt=bt, dec_a=jnp.exp2(tot_a), dec_b=jnp.exp2(tot_b))

    def score_stage(c):
        rows = c["rows"]
        qd = (gq_ref[rows, :].astype(F32) * jnp.exp2(c["b"])).astype(BF16)
        kinv_t = (kt_ref[0, :, rows].astype(F32) * jnp.exp2(-c["bt"])).astype(BF16)
        kinv_a = jnp.where(first_t, kinv_t, zero16)
        kinv_b = jnp.where(first_t, zero16, kinv_t)
        c["v"] = [gv_ref[rows, hv[h]] for h in heads]
        c["qm"] = [jnp.where(first_half if h % 2 == 0 else ~first_half, qd[:, grp[h]], zero16) for h in heads]
        c["att"] = [_dot(c["qm"][h], kinv_t[grp[h], :]) for h in heads]
        c["upd_a"] = [_dot(kinv_a[hk[h], :], c["v"][h]) for h in heads]
        c["upd_b"] = [_dot(kinv_b[hk[h], :], c["v"][h]) for h in heads]

    def output_stage(c):
        o = []
        s_a = list(state)
        s_b = [c["dec_a"][hk[h], :] * (s_a[h] + c["upd_a"][h]) for h in heads]
        for h in heads:
            state[h] = c["dec_b"][hk[h], :] * (s_b[h] + c["upd_b"][h])
        for h in heads:
            att = jnp.where(causal, c["att"][h], 0.0).astype(BF16)
            pair_a = [s_a[h], s_a[h ^ 1]] if h % 2 == 0 else [s_a[h ^ 1], s_a[h]]
            pair_b = [s_b[h], s_a[h ^ 1]] if h % 2 == 0 else [s_a[h ^ 1], s_b[h]]
            o_a = _dot(c["qm"][h][:C, :], jnp.concatenate(pair_a, axis=0).astype(BF16))
            o_b = _dot(c["qm"][h][C:, :], jnp.concatenate(pair_b, axis=0).astype(BF16))
            o.append(_dot(att, c["v"][h]) + jnp.concatenate([o_a, o_b], axis=0))
        c["o"] = o

    def norm_stage(c):
        rows = c["rows"]
        for h in heads:
            o = c["o"][h]
            o = o * lax.rsqrt(jnp.mean(o * o, axis=-1, keepdims=True) + EPS) * gn_ref[:, hv[h]]
            r = gr_ref[rows, hv[h]].astype(F32)
            o_ref[rows, hv[h]] = (o * (r * jax.nn.sigmoid(r))).astype(BF16)

    ctx = {}
    for t in range(pairs + 3):
        if t < pairs:
            ctx[t] = decay_stage(t)
        if 0 <= t - 1 < pairs:
            score_stage(ctx[t - 1])
        if 0 <= t - 2 < pairs:
            output_stage(ctx[t - 2])
        if 0 <= t - 3 < pairs:
            norm_stage(ctx.pop(t - 3))

    for h in heads:
        s_ref[hk[h], :] = state[h]


def _gla(gq, kt, gv, gr, glogt, gn, *, batch, seq, tg):
    T = batch * seq
    spb = seq // tg
    row = lambda n: pl.BlockSpec((tg, n), lambda b, i: (b * spb + i, 0))
    colt = pl.BlockSpec((1, GLA_QK, tg), lambda b, i: (b, 0, i))
    return pl.pallas_call(
        functools.partial(_gla_kernel, pairs=tg // (2 * GLA_CHUNK)),
        grid=(batch, spb),
        in_specs=[row(GLA_QK), colt, row(GLA_WIDTH), row(GLA_WIDTH), colt,
                  pl.BlockSpec((1, GLA_WIDTH), lambda b, i: (0, 0))],
        out_specs=row(GLA_WIDTH),
        out_shape=jax.ShapeDtypeStruct((T, GLA_WIDTH), BF16),
        scratch_shapes=[pltpu.VMEM((GLA_QK, GLA_DV), F32)],
        compiler_params=pltpu.CompilerParams(dimension_semantics=("arbitrary", "arbitrary"),
                                             vmem_limit_bytes=VMEM_LIMIT),
        name="gla",
    )(gq, kt, gv, gr, glogt, gn)


def _dsa_kernel(q_ref, k_ref, v_ref, kh_ref, vh_ref, bias_ref, o_ref, st_ref, *, res, blocks, keys_transposed):
    L = DSA_BLOCK
    first_tile = pl.program_id(2) == 0
    lane = lax.broadcasted_iota(jnp.int32, (L, LANES), 1)
    first_half = lane < DSA_DH
    prev_cols = lax.broadcasted_iota(jnp.int32, (L, 2 * L), 1) < L

    st_tiles = {}
    pair_out = {}

    def score_stage(r, blk, h):
        rows = slice(blk * L, (blk + 1) * L)
        both = slice((blk - 1) * L, (blk + 1) * L)
        grp = slice((h // 2) * LANES, (h // 2 + 1) * LANES)
        if keys_transposed:
            k_prev = kh_ref[r, grp, :] if blk == 0 else k_ref[blk - 1, r, grp, :]
            kcat = jnp.concatenate([k_prev, k_ref[blk, r, grp, :]], axis=1)
        elif blk == 0:
            kcat = jnp.concatenate([kh_ref[r, :, grp], k_ref[r, rows, grp]], axis=0)
        else:
            kcat = k_ref[r, both, grp]
        own = first_half if h % 2 == 0 else ~first_half
        qm = jnp.where(own, q_ref[r, rows, grp], jnp.zeros((), BF16))
        return _dot(qm, kcat) if keys_transposed else _dot_nt(qm, kcat)

    def value_stage(r, blk, h, s):
        rows = slice(blk * L, (blk + 1) * L)
        grp = slice((h // 2) * LANES, (h // 2 + 1) * LANES)
        bias = bias_ref[h]
        if blk == 0:
            bias = jnp.where(prev_cols & first_tile, NEG, bias)
        s = s + bias
        m = jnp.max(s, axis=-1, keepdims=True)
        p = jnp.exp2(s - m)
        den = jnp.sum(p, axis=-1, keepdims=True)
        st_old = st_tiles.get((r, blk), jnp.zeros((L, LANES), F32))
        st_tiles[r, blk] = jnp.where(lane == h, m, jnp.where(lane == DSA_HEADS + h, den, st_old))
        if blk == 0:
            vcat = jnp.concatenate([vh_ref[r, :, grp], v_ref[r, rows, grp]], axis=0)
        else:
            vcat = v_ref[r, both_rows(blk), grp]
        out = _dot(p.astype(BF16), vcat)
        if h % 2 == 0:
            pair_out[r, blk] = out
        else:
            o_ref[r, rows, grp] = jnp.where(first_half, pair_out.pop((r, blk)), out).astype(BF16)
        if h == DSA_HEADS - 1:
            st_ref[r, rows, :] = st_tiles.pop((r, blk))

    def both_rows(blk):
        return slice((blk - 1) * L, (blk + 1) * L)

    items = [(r, blk, h) for r in range(res) for blk in range(blocks) for h in range(DSA_HEADS)]
    lag = 1
    pending = {}
    for t in range(len(items) + lag):
        if t < len(items):
            pending[t] = score_stage(*items[t])
        if t - lag >= 0:
            value_stage(*items[t - lag], pending.pop(t - lag))


def _dsa_branch(x, kt, bias, branch, *, rows_per_step):
    batch, d, n, _ = x.shape
    tr = min(rows_per_step, n)
    res = min(rows_per_step // tr, d)
    hb = tr // DSA_BLOCK
    W = DSA_WIDTH
    prev_blk = lambda i: jnp.maximum(i * hb - 1, 0)
    blk = lambda part: pl.BlockSpec((None, res, tr, W), lambda b, r, i: (b, r, i, part))
    halo = lambda part: pl.BlockSpec((None, res, DSA_BLOCK, W), lambda b, r, i: (b, r, prev_blk(i), part))
    if kt is None:
        v_part = 2
        k_arg, k_spec, kh_spec = x, blk(1), halo(1)
    else:
        v_part = 1
        k_arg = kt
        k_spec = pl.BlockSpec((None, hb, res, W, DSA_BLOCK), lambda b, r, i: (b, i, r, 0, 0))
        kh_spec = pl.BlockSpec((None, None, res, W, DSA_BLOCK), lambda b, r, i: (b, prev_blk(i), r, 0, 0))
    return pl.pallas_call(
        functools.partial(_dsa_kernel, res=res, blocks=hb, keys_transposed=kt is not None),
        grid=(batch, d // res, n // tr),
        in_specs=[blk(0), k_spec, blk(v_part), kh_spec, halo(v_part),
                  pl.BlockSpec((None, DSA_HEADS, DSA_BLOCK, 2 * DSA_BLOCK), lambda b, r, i: (branch, 0, 0, 0))],
        out_specs=[pl.BlockSpec((None, res, tr, W), lambda b, r, i: (b, r, i, 0)),
                   pl.BlockSpec((None, res, tr, LANES), lambda b, r, i: (b, r, i, 0))],
        out_shape=[jax.ShapeDtypeStruct((batch, d, n, W), BF16),
                   jax.ShapeDtypeStruct((batch, d, n, LANES), F32)],
        compiler_params=pltpu.CompilerParams(dimension_semantics=("arbitrary",) * 3,
                                             vmem_limit_bytes=VMEM_LIMIT),
        name=f"dsa_d{d}",
    )(x, k_arg, x, k_arg, x, bias)


def _out_kernel(x_ref, oa_ref, o1_ref, o4_ref, o16_ref, l1_ref, l4_ref, l16_ref,
                f1_ref, f4_ref, f16_ref, g1_ref, g4_ref, g16_ref,
                wo_ref, g2_ref, w1_ref, w2_ref, gf_ref, y_ref, osc_ref, lsc_ref, ob_ref, tsc_ref, *, ff_chunk):
    tm = x_ref.shape[0]
    npair = DSA_HEADS // 2
    lane = lax.broadcasted_iota(jnp.int32, (tm, LANES), 1)
    first_half = lane < DSA_DH

    def combine(o1_ref, o4_ref, o16_ref, l1_ref, l4_ref, l16_ref):
        n4, n16 = tm // 4, tm // 16
        for r4 in range(4):
            for r2 in range(4):
                dst = pl.ds(r4 * n4 + r2, n16, stride=4)
                tsc_ref[npair, dst, :] = l16_ref[r4 + 4 * r2]
                for hp in range(npair):
                    tsc_ref[hp, dst, :] = o16_ref[r4 + 4 * r2, :, hp * LANES:(hp + 1) * LANES].astype(F32)
        for r4 in range(4):
            dst = pl.ds(r4, n4, stride=4)
            src = slice(r4 * n4, (r4 + 1) * n4)
            lsc_ref[0, dst, :] = l4_ref[r4]
            lsc_ref[1, dst, :] = tsc_ref[npair, src, :]
            for hp in range(npair):
                osc_ref[hp, dst, :] = o4_ref[r4, :, hp * LANES:(hp + 1) * LANES].astype(F32)
                osc_ref[npair + hp, dst, :] = tsc_ref[hp, src, :]

        sts = (l1_ref[...], lsc_ref[0], lsc_ref[1])
        m = jnp.maximum(jnp.maximum(sts[0], sts[1]), sts[2])
        es = [jnp.exp2(st - m) for st in sts]
        total = sum(e * pltpu.roll(st, LANES - DSA_HEADS, axis=1) for e, st in zip(es, sts))
        inv = 1.0 / jnp.where(lane < DSA_HEADS, total, 1.0)
        ws = [e * inv for e in es]

        for hp in range(npair):
            grp = slice(hp * LANES, (hp + 1) * LANES)
            branch_o = (o1_ref[:, grp].astype(F32), osc_ref[hp], osc_ref[npair + hp])
            acc = jnp.zeros((tm, LANES), F32)
            for w, o in zip(ws, branch_o):
                wa = jnp.sum(jnp.where(lane == 2 * hp, w, 0.0), axis=-1, keepdims=True)
                wb = jnp.sum(jnp.where(lane == 2 * hp + 1, w, 0.0), axis=-1, keepdims=True)
                acc = acc + jnp.where(first_half, wa, wb) * o
            ob_ref[:, grp] = acc.astype(BF16)

    @pl.when(pl.program_id(0) == 0)
    def _():
        combine(f1_ref, f4_ref, f16_ref, g1_ref, g4_ref, g16_ref)

    mixed = _dot(oa_ref[...], wo_ref[:GLA_WIDTH, :]) + _dot(ob_ref[...], wo_ref[GLA_WIDTH:, :])
    h = x_ref[...] + mixed
    nm = (h * lax.rsqrt(jnp.mean(h * h, axis=-1, keepdims=True) + EPS) * g2_ref[...]).astype(BF16)
    ff = None
    for c in range(D_FF // ff_chunk):
        cols = slice(c * ff_chunk, (c + 1) * ff_chunk)
        a = jnp.maximum(_dot(nm, w1_ref[:, cols]), 0.0)
        d = _dot((a * a).astype(BF16), w2_ref[cols, :])
        ff = d if ff is None else ff + d
    h = h + ff
    y_ref[...] = h * lax.rsqrt(jnp.mean(h * h, axis=-1, keepdims=True) + EPS) * gf_ref[...]

    combine(o1_ref, o4_ref, o16_ref, l1_ref, l4_ref, l16_ref)


def _out(x2, oa, os_, ls, wo, g2, w1, w2, gf, *, seq, tm, ff_chunk):
    T = x2.shape[0]
    spb = seq // tm
    nt = T // tm
    nxt = lambda s: jnp.minimum(s + 1, nt - 1)
    one = pl.Buffered(1)
    row = lambda n: pl.BlockSpec((tm, n), lambda s: (s, 0))
    row_nxt = lambda n: pl.BlockSpec((tm, n), lambda s: (nxt(s), 0))
    row_first = lambda n: pl.BlockSpec((tm, n), lambda s: (0, 0), pipeline_mode=one)
    strided = lambda d, n: pl.BlockSpec((None, d, tm // d, n), lambda s: (nxt(s) // spb, 0, nxt(s) % spb, 0))
    strided_first = lambda d, n: pl.BlockSpec((None, d, tm // d, n), lambda s: (0, 0, 0, 0), pipeline_mode=one)
    const = lambda a: pl.BlockSpec(a.shape, lambda s: (0,) * a.ndim, pipeline_mode=one)
    W = DSA_WIDTH
    return pl.pallas_call(
        functools.partial(_out_kernel, ff_chunk=ff_chunk),
        grid=(nt,),
        in_specs=[row(D_MODEL), row(GLA_WIDTH), row_nxt(W), strided(4, W), strided(16, W),
                  row_nxt(LANES), strided(4, LANES), strided(16, LANES),
                  row_first(W), strided_first(4, W), strided_first(16, W),
                  row_first(LANES), strided_first(4, LANES), strided_first(16, LANES),
                  const(wo), const(g2), const(w1), const(w2), const(gf)],
        out_specs=row(D_MODEL),
        out_shape=jax.ShapeDtypeStruct((T, D_MODEL), F32),
        scratch_shapes=[pltpu.VMEM((2 * (W // LANES), tm, LANES), F32),
                        pltpu.VMEM((2, tm, LANES), F32),
                        pltpu.VMEM((tm, W), BF16),
                        pltpu.VMEM((W // LANES + 1, tm, LANES), F32)],
        compiler_params=pltpu.CompilerParams(dimension_semantics=("arbitrary",),
                                             vmem_limit_bytes=VMEM_LIMIT),
        name="out_mlp",
    )(x2, oa, *os_, *ls, *os_, *ls, wo, g2, w1, w2, gf)


_IN_SPLITS = (GLA_QK, GLA_QK, GLA_WIDTH, GLA_WIDTH, GLA_RANK, DSA_WIDTH, DSA_WIDTH, DSA_WIDTH)
_IN_OFFSETS = tuple(int(v) for v in np.cumsum((0,) + _IN_SPLITS))


def _split_w_in_kernel(wt_ref, wq_ref, wk_ref, wv_ref, wr_ref, wg_ref, wd_ref):
    o = _IN_OFFSETS
    lane = lax.broadcasted_iota(jnp.int32, (wt_ref.shape[1], RANK_PAD), 1)

    def piece(start):
        return wt_ref[start:start + LANES, :].T

    for dst, start, width, scale in ((wq_ref, o[0], GLA_QK, GLA_DK ** -0.5), (wk_ref, o[1], GLA_QK, None),
                                     (wv_ref, o[2], GLA_WIDTH, None), (wr_ref, o[3], GLA_WIDTH, None),
                                     (wd_ref, o[5], 3 * DSA_WIDTH, None)):
        for j in range(width // LANES):
            blk = piece(start + j * LANES)
            if scale is not None:
                blk = blk * scale
            if dst is wd_ref and j < DSA_WIDTH // LANES:
                blk = blk * (DSA_DH ** -0.5 * LOG2E)
            dst[:, j * LANES:(j + 1) * LANES] = blk.astype(BF16)
    wg_ref[...] = jnp.where(lane < GLA_RANK, piece(o[4]), 0.0).astype(BF16)


def _split_w_in(wt):
    _, N, K = wt.shape
    shapes = [(K, GLA_QK), (K, GLA_QK), (K, GLA_WIDTH), (K, GLA_WIDTH), (K, RANK_PAD), (K, 3 * DSA_WIDTH)]
    return pl.pallas_call(
        _split_w_in_kernel,
        grid=(1,),
        in_specs=[pl.BlockSpec((None, N, K), lambda i: (0, 0, 0))],
        out_specs=[pl.BlockSpec(s, lambda i: (0, 0)) for s in shapes],
        out_shape=[jax.ShapeDtypeStruct(s, BF16) for s in shapes],
        compiler_params=pltpu.CompilerParams(vmem_limit_bytes=VMEM_LIMIT),
        name="split_w_in",
    )(wt)


def kernel(x, attn_norm_g, w_in, gla_gate_w2, gla_gate_b, gla_norm_g, rel_bias, w_out, mlp_norm_g,
           w_ff1, w_ff2, final_norm_g):
    batch, seq, _ = x.shape
    assert tuple(d for _, d in DSA_PATTERN) == (1, 4, 16)
    assert seq % (DSA_PATTERN[-1][1] * DSA_BLOCK) == 0
    T = batch * seq
    x2 = x.reshape(T, D_MODEL)

    wq, wk, wv, wr, wg, wd = _split_w_in(jnp.swapaxes(w_in, 1, 2))
    w2 = jnp.pad(gla_gate_w2[0], ((0, RANK_PAD - GLA_RANK), (0, 0))).astype(BF16)
    gb = gla_gate_b[0].astype(F32).reshape(1, GLA_QK)

    gq, gv, gr, d1, kt, glogt, d4, d16, kt1, kt4 = _proj(
        x2, attn_norm_g[0].reshape(1, D_MODEL).astype(F32), wq, wv, wr, wk, wg, w2, gb, wd,
        batch=batch, seq=seq, tm=1024)

    o_a = _gla(gq, kt, gv, gr, glogt, gla_norm_g[0].reshape(1, GLA_WIDTH).astype(F32),
               batch=batch, seq=seq, tg=2048)

    bias = _bias_tables(rel_bias)
    os_, ls = [], []
    branches = ((d1.reshape(batch, 1, seq, 2 * DSA_WIDTH), kt1), (d4, kt4), (d16, None))
    for branch, (xd, ktd) in enumerate(branches):
        o, stats = _dsa_branch(xd, ktd, bias, branch, rows_per_step=2048)
        os_.append(o)
        ls.append(stats)
    os_[0] = os_[0].reshape(T, DSA_WIDTH)
    ls[0] = ls[0].reshape(T, LANES)

    y = _out(x2, o_a, os_, ls, w_out[0].astype(BF16), mlp_norm_g[0].reshape(1, D_MODEL).astype(F32),
             w_ff1[0].astype(BF16), w_ff2[0].astype(BF16), final_norm_g.reshape(1, D_MODEL).astype(F32),
             seq=seq, tm=512, ff_chunk=2048)
    return y.reshape(batch, seq, D_MODEL)
```

```python
import functools
import math

import numpy as np
import jax
import jax.numpy as jnp
from jax import lax
from jax.experimental import pallas as pl
from jax.experimental.pallas import tpu as pltpu

D_MODEL = 1024
GLA_WIDTH = 512
GLA_HEADS = 4
GLA_DK = 64
GLA_DV = 128
GLA_QK = GLA_HEADS * GLA_DK
GLA_RANK = 16
GLA_TAU = 16.0
GLA_CHUNK = 64
DSA_WIDTH = 512
DSA_HEADS = 8
DSA_DH = 64
DSA_PATTERN = ((128, 1), (512, 4), (2048, 16))
DSA_BLOCK = 128
REL_BUCKETS = 32
REL_MAX_DIST = 2048
D_FF = 4096
EPS = 1e-6
NEG = -1e30
LOG2E = math.log2(math.e)

LANES = 128
RANK_PAD = LANES
VMEM_LIMIT = 56 * 1024 * 1024

F32 = jnp.float32
BF16 = jnp.bfloat16

_NT = (((1,), (1,)), ((), ()))


def _dot(a, b):
    return jnp.dot(a, b, preferred_element_type=F32)


def _dot_nt(a, b):
    return lax.dot_general(a, b, _NT, preferred_element_type=F32)


def _split_hi_lo(x):
    hi = x.astype(BF16)
    lo = (x - hi.astype(F32)).astype(BF16)
    return hi, lo


def _log_sigmoid(x):
    return jnp.minimum(x, 0.0) - jnp.log1p(jnp.exp(-jnp.abs(x)))


def _bucket_tables():
    max_exact = REL_BUCKETS // 2
    L = DSA_BLOCK
    steps = L + np.arange(L)[:, None] - np.arange(2 * L)[None, :]
    tables = []
    for window, dilation in DSA_PATTERN:
        span = window // dilation
        in_band = (steps >= 0) & (steps <= span)
        n = np.maximum(steps * dilation, 0)
        large = max_exact + (np.log(np.maximum(n, 1) / max_exact)
                             / math.log(REL_MAX_DIST / max_exact)
                             * (REL_BUCKETS - max_exact)).astype(np.int32)
        large = np.minimum(large, REL_BUCKETS - 1)
        bucket = np.where(n < max_exact, n, large).astype(np.int32)
        tables.append(np.where(in_band, bucket, -1).astype(np.int32))
    return np.stack(tables)


def _fill_bias_tables(rb_ref, bt, out_ref):
    def one_head(h, carry):
        acc = jnp.full(bt.shape, NEG, F32)
        for b in range(REL_BUCKETS):
            acc = jnp.where(bt == b, rb_ref[b, h] * LOG2E, acc)
        out_ref[h] = acc
        return carry

    lax.fori_loop(0, DSA_HEADS, one_head, 0)


def _proj_kernel(x_ref, g_ref, wq_ref, wv_ref, wr_ref, wk_ref, wg_ref, w2_ref, b_ref, wd_ref,
                 gq_ref, gv_ref, gr_ref, d1_ref, kt_ref, glogt_ref, d4_ref, d16_ref,
                 kt1_ref, kt4_ref, dsc_ref, t4_ref):
    tm = x_ref.shape[0]
    ncol = 3 * DSA_WIDTH // LANES
    kcols = range(DSA_WIDTH // LANES, 2 * DSA_WIDTH // LANES)
    n4 = tm // 4
    n16 = tm // 16

    x = x_ref[...]
    ms = jnp.mean(x * x, axis=-1, keepdims=True)
    nx = (x * lax.rsqrt(ms + EPS) * g_ref[...]).astype(BF16)

    for j in range(ncol // 2):
        res = _dot(nx, wd_ref[:, 2 * j * LANES:(2 * j + 2) * LANES])
        halves = {2 * j: res[:, :LANES], 2 * j + 1: res[:, LANES:]}
        for c in halves:
            dsc_ref[c % 2] = halves[c]
        for c in halves:
            sl = c % 2
            cols = slice(c * LANES, (c + 1) * LANES)
            is_key = c in kcols
            krows = slice((c - kcols[0]) * LANES, (c - kcols[0] + 1) * LANES)
            qv = c if c < kcols[0] else c - len(kcols)
            qv_cols = slice(qv * LANES, (qv + 1) * LANES)
            if is_key:
                for jb in range(tm // DSA_BLOCK):
                    blk_rows = slice(jb * DSA_BLOCK, (jb + 1) * DSA_BLOCK)
                    kt1_ref[jb, krows, :] = dsc_ref[sl, blk_rows, :].T.astype(BF16)
            else:
                d1_ref[:, qv_cols] = halves[c].astype(BF16)
            for r4 in range(4):
                sub = dsc_ref[sl, pl.ds(r4, n4, stride=4), :]
                t4_ref[sl, r4 * n4:(r4 + 1) * n4, :] = sub
                if is_key:
                    for jb in range(n4 // DSA_BLOCK):
                        blk_rows = slice(jb * DSA_BLOCK, (jb + 1) * DSA_BLOCK)
                        kt4_ref[jb, r4, krows, :] = sub[blk_rows, :].T.astype(BF16)
                else:
                    d4_ref[r4, :, qv_cols] = sub.astype(BF16)
            for r4 in range(4):
                for r2 in range(4):
                    d16_ref[r4 + 4 * r2, :, cols] = (
                        t4_ref[sl, pl.ds(r4 * n4 + r2, n16, stride=4), :].astype(BF16))

    gq_ref[...] = _dot(nx, wq_ref[...]).astype(BF16)

    kt_ref[0] = _dot(nx, wk_ref[...]).T.astype(BF16)
    glow = _dot(nx, wg_ref[...]).astype(BF16)
    glog = _log_sigmoid(_dot(glow, w2_ref[...]) + b_ref[...]) * (LOG2E / GLA_TAU)
    glogt_ref[0] = glog.T

    gv_ref[...] = _dot(nx, wv_ref[...]).astype(BF16)
    gr_ref[...] = _dot(nx, wr_ref[...]).astype(BF16)


def _proj(x2, g, wq, wv, wr, wk, wg, w2, b, wd, *, batch, seq, tm):
    T = batch * seq
    spb = seq // tm
    nt = T // tm
    full = lambda a: pl.BlockSpec(a.shape, lambda s: (0,) * a.ndim, pipeline_mode=pl.Buffered(1))
    row_cur = lambda n: pl.BlockSpec((tm, n), lambda s: (s, 0))
    colt = pl.BlockSpec((1, GLA_QK, tm), lambda s: (s // spb, 0, s % spb))
    W, L = DSA_WIDTH, DSA_BLOCK
    W2, W3 = 2 * W, 3 * W
    strided = lambda d, n: pl.BlockSpec((None, d, tm // d, n), lambda s: (s // spb, 0, s % spb, 0))
    assert tm % (4 * L) == 0
    kt1_spec = pl.BlockSpec((None, tm // L, None, W, L), lambda s: (s // spb, s % spb, 0, 0, 0))
    kt4_spec = pl.BlockSpec((None, tm // (4 * L), 4, W, L), lambda s: (s // spb, s % spb, 0, 0, 0))
    return pl.pallas_call(
        _proj_kernel,
        grid=(nt,),
        in_specs=[row_cur(D_MODEL)] + [full(a) for a in (g, wq, wv, wr, wk, wg, w2, b, wd)],
        out_specs=[row_cur(GLA_QK), row_cur(GLA_WIDTH), row_cur(GLA_WIDTH), row_cur(W2),
                   colt, colt, strided(4, W2), strided(16, W3), kt1_spec, kt4_spec],
        out_shape=[jax.ShapeDtypeStruct((T, GLA_QK), BF16),
                   jax.ShapeDtypeStruct((T, GLA_WIDTH), BF16),
                   jax.ShapeDtypeStruct((T, GLA_WIDTH), BF16),
                   jax.ShapeDtypeStruct((T, W2), BF16),
                   jax.ShapeDtypeStruct((batch, GLA_QK, seq), BF16),
                   jax.ShapeDtypeStruct((batch, GLA_QK, seq), F32),
                   jax.ShapeDtypeStruct((batch, 4, seq // 4, W2), BF16),
                   jax.ShapeDtypeStruct((batch, 16, seq // 16, W3), BF16),
                   jax.ShapeDtypeStruct((batch, seq // L, 1, W, L), BF16),
                   jax.ShapeDtypeStruct((batch, seq // (4 * L), 4, W, L), BF16)],
        scratch_shapes=[pltpu.VMEM((2, tm, LANES), F32)] * 2,
        compiler_params=pltpu.CompilerParams(dimension_semantics=("arbitrary",),
                                             vmem_limit_bytes=VMEM_LIMIT),
        name="proj",
    )(x2, g, wq, wv, wr, wk, wg, w2, b, wd)


def _gla_kernel(gq_ref, kt_ref, gv_ref, gr_ref, glogt_ref, gn_ref, o_ref, s_ref, *, pairs):
    C = GLA_CHUNK
    P = 2 * C

    @pl.when(pl.program_id(1) == 0)
    def _():
        s_ref[...] = jnp.zeros_like(s_ref)

    ri = lax.broadcasted_iota(jnp.int32, (P, P), 0)
    ci = lax.broadcasted_iota(jnp.int32, (P, P), 1)
    same_chunk = (ri < C) == (ci < C)
    causal = same_chunk & (ci <= ri)
    upp = jnp.where(same_chunk & (ri <= ci), 1.0, 0.0).astype(BF16)
    lane = lax.broadcasted_iota(jnp.int32, (P, LANES), 1)
    first_half = lane < C
    lane_t = lax.broadcasted_iota(jnp.int32, (GLA_QK, P), 1)
    first_t = lane_t < C

    zero16 = jnp.zeros((), BF16)
    state = [s_ref[h * GLA_DK:(h + 1) * GLA_DK, :] for h in range(GLA_HEADS)]

    heads = range(GLA_HEADS)
    hk = [slice(h * GLA_DK, (h + 1) * GLA_DK) for h in heads]
    hv = [slice(h * GLA_DV, (h + 1) * GLA_DV) for h in heads]
    grp = [slice((h // 2) * LANES, (h // 2 + 1) * LANES) for h in heads]

    def decay_stage(p):
        rows = slice(p * P, (p + 1) * P)
        gt = glogt_ref[0, :, rows]
        gt_hi, gt_lo = _split_hi_lo(gt)
        bt = _dot(gt_hi, upp) + _dot(gt_lo, upp)
        b = bt.T
        tot_a = bt[:, C - 1:C]
        tot_b = bt[:, P - 1:P]
        return dict(rows=rows, b=b, bt=bt, dec_a=jnp.exp2(tot_a), dec_b=jnp.exp2(tot_b))

    def score_stage(c):
        rows = c["rows"]
        qd = (gq_ref[rows, :].astype(F32) * jnp.exp2(c["b"])).astype(BF16)
        kinv_t = (kt_ref[0, :, rows].astype(F32) * jnp.exp2(-c["bt"])).astype(BF16)
        kinv_a = jnp.where(first_t, kinv_t, zero16)
        kinv_b = jnp.where(first_t, zero16, kinv_t)
        c["v"] = [gv_ref[rows, hv[h]] for h in heads]
        c["qm"] = [jnp.where(first_half if h % 2 == 0 else ~first_half, qd[:, grp[h]], zero16) for h in heads]
        c["att"] = [_dot(c["qm"][h], kinv_t[grp[h], :]) for h in heads]
        c["upd_a"] = [_dot(kinv_a[hk[h], :], c["v"][h]) for h in heads]
        c["upd_b"] = [_dot(kinv_b[hk[h], :], c["v"][h]) for h in heads]

    def output_stage(c):
        o = []
        s_a = list(state)
        s_b = [c["dec_a"][hk[h], :] * (s_a[h] + c["upd_a"][h]) for h in heads]
        for h in heads:
            state[h] = c["dec_b"][hk[h], :] * (s_b[h] + c["upd_b"][h])
        for h in heads:
            att = jnp.where(causal, c["att"][h], 0.0).astype(BF16)
            pair_a = [s_a[h], s_a[h ^ 1]] if h % 2 == 0 else [s_a[h ^ 1], s_a[h]]
            pair_b = [s_b[h], s_a[h ^ 1]] if h % 2 == 0 else [s_a[h ^ 1], s_b[h]]
            o_a = _dot(c["qm"][h][:C, :], jnp.concatenate(pair_a, axis=0).astype(BF16))
            o_b = _dot(c["qm"][h][C:, :], jnp.concatenate(pair_b, axis=0).astype(BF16))
            o.append(_dot(att, c["v"][h]) + jnp.concatenate([o_a, o_b], axis=0))
        c["o"] = o

    def norm_stage(c):
        rows = c["rows"]
        for h in heads:
            o = c["o"][h]
            o = o * lax.rsqrt(jnp.mean(o * o, axis=-1, keepdims=True) + EPS) * gn_ref[:, hv[h]]
            r = gr_ref[rows, hv[h]].astype(F32)
            o_ref[rows, hv[h]] = (o * (r * jax.nn.sigmoid(r))).astype(BF16)

    ctx = {}
    for t in range(pairs + 3):
        if t < pairs:
            ctx[t] = decay_stage(t)
        if 0 <= t - 1 < pairs:
            score_stage(ctx[t - 1])
        if 0 <= t - 2 < pairs:
            output_stage(ctx[t - 2])
        if 0 <= t - 3 < pairs:
            norm_stage(ctx.pop(t - 3))

    for h in heads:
        s_ref[hk[h], :] = state[h]


def _gla(gq, kt, gv, gr, glogt, gn, *, batch, seq, tg):
    T = batch * seq
    spb = seq // tg
    row = lambda n: pl.BlockSpec((tg, n), lambda b, i: (b * spb + i, 0))
    colt = pl.BlockSpec((1, GLA_QK, tg), lambda b, i: (b, 0, i))
    return pl.pallas_call(
        functools.partial(_gla_kernel, pairs=tg // (2 * GLA_CHUNK)),
        grid=(batch, spb),
        in_specs=[row(GLA_QK), colt, row(GLA_WIDTH), row(GLA_WIDTH), colt,
                  pl.BlockSpec((1, GLA_WIDTH), lambda b, i: (0, 0))],
        out_specs=row(GLA_WIDTH),
        out_shape=jax.ShapeDtypeStruct((T, GLA_WIDTH), BF16),
        scratch_shapes=[pltpu.VMEM((GLA_QK, GLA_DV), F32)],
        compiler_params=pltpu.CompilerParams(dimension_semantics=("arbitrary", "arbitrary"),
                                             vmem_limit_bytes=VMEM_LIMIT),
        name="gla",
    )(gq, kt, gv, gr, glogt, gn)


def _dsa_kernel(q_ref, k_ref, v_ref, kh_ref, vh_ref, rb_ref, bt_ref, o_ref, st_ref, bias_ref, *,
                res, blocks, keys_transposed):
    L = DSA_BLOCK
    first_tile = pl.program_id(2) == 0
    lane = lax.broadcasted_iota(jnp.int32, (L, LANES), 1)
    first_half = lane < DSA_DH
    prev_cols = lax.broadcasted_iota(jnp.int32, (L, 2 * L), 1) < L

    @pl.when((pl.program_id(0) == 0) & (pl.program_id(1) == 0) & first_tile)
    def _():
        _fill_bias_tables(rb_ref, bt_ref[...], bias_ref)

    st_tiles = {}
    pair_out = {}

    def score_stage(r, blk, h):
        rows = slice(blk * L, (blk + 1) * L)
        both = slice((blk - 1) * L, (blk + 1) * L)
        grp = slice((h // 2) * LANES, (h // 2 + 1) * LANES)
        if keys_transposed:
            k_prev = kh_ref[r, grp, :] if blk == 0 else k_ref[blk - 1, r, grp, :]
            kcat = jnp.concatenate([k_prev, k_ref[blk, r, grp, :]], axis=1)
        elif blk == 0:
            kcat = jnp.concatenate([kh_ref[r, :, grp], k_ref[r, rows, grp]], axis=0)
        else:
            kcat = k_ref[r, both, grp]
        own = first_half if h % 2 == 0 else ~first_half
        qm = jnp.where(own, q_ref[r, rows, grp], jnp.zeros((), BF16))
        return _dot(qm, kcat) if keys_transposed else _dot_nt(qm, kcat)

    def value_stage(r, blk, h, s):
        rows = slice(blk * L, (blk + 1) * L)
        grp = slice((h // 2) * LANES, (h // 2 + 1) * LANES)
        bias = bias_ref[h]
        if blk == 0:
            bias = jnp.where(prev_cols & first_tile, NEG, bias)
        s = s + bias
        m = jnp.max(s, axis=-1, keepdims=True)
        p = jnp.exp2(s - m)
        den = jnp.sum(p, axis=-1, keepdims=True)
        st_old = st_tiles.get((r, blk), jnp.zeros((L, LANES), F32))
        st_tiles[r, blk] = jnp.where(lane == h, m, jnp.where(lane == DSA_HEADS + h, den, st_old))
        if blk == 0:
            vcat = jnp.concatenate([vh_ref[r, :, grp], v_ref[r, rows, grp]], axis=0)
        else:
            vcat = v_ref[r, both_rows(blk), grp]
        out = _dot(p.astype(BF16), vcat)
        if h % 2 == 0:
            pair_out[r, blk] = out
        else:
            o_ref[r, rows, grp] = jnp.where(first_half, pair_out.pop((r, blk)), out).astype(BF16)
        if h == DSA_HEADS - 1:
            st_ref[r, rows, :] = st_tiles.pop((r, blk))

    def both_rows(blk):
        return slice((blk - 1) * L, (blk + 1) * L)

    items = [(r, blk, h) for r in range(res) for blk in range(blocks) for h in range(DSA_HEADS)]
    lag = 1
    pending = {}
    for t in range(len(items) + lag):
        if t < len(items):
            pending[t] = score_stage(*items[t])
        if t - lag >= 0:
            value_stage(*items[t - lag], pending.pop(t - lag))


def _dsa_branch(x, kt, rel_bias, buckets, branch, *, rows_per_step):
    batch, d, n, _ = x.shape
    tr = min(rows_per_step, n)
    res = min(rows_per_step // tr, d)
    hb = tr // DSA_BLOCK
    W = DSA_WIDTH
    prev_blk = lambda i: jnp.maximum(i * hb - 1, 0)
    blk = lambda part: pl.BlockSpec((None, res, tr, W), lambda b, r, i: (b, r, i, part))
    halo = lambda part: pl.BlockSpec((None, res, DSA_BLOCK, W), lambda b, r, i: (b, r, prev_blk(i), part))
    if kt is None:
        v_part = 2
        k_arg, k_spec, kh_spec = x, blk(1), halo(1)
    else:
        v_part = 1
        k_arg = kt
        k_spec = pl.BlockSpec((None, hb, res, W, DSA_BLOCK), lambda b, r, i: (b, i, r, 0, 0))
        kh_spec = pl.BlockSpec((None, None, res, W, DSA_BLOCK), lambda b, r, i: (b, prev_blk(i), r, 0, 0))
    return pl.pallas_call(
        functools.partial(_dsa_kernel, res=res, blocks=hb, keys_transposed=kt is not None),
        grid=(batch, d // res, n // tr),
        in_specs=[blk(0), k_spec, blk(v_part), kh_spec, halo(v_part),
                  pl.BlockSpec(memory_space=pltpu.SMEM),
                  pl.BlockSpec((None, DSA_BLOCK, 2 * DSA_BLOCK), lambda b, r, i: (branch, 0, 0))],
        out_specs=[pl.BlockSpec((None, res, tr, W), lambda b, r, i: (b, r, i, 0)),
                   pl.BlockSpec((None, res, tr, LANES), lambda b, r, i: (b, r, i, 0))],
        out_shape=[jax.ShapeDtypeStruct((batch, d, n, W), BF16),
                   jax.ShapeDtypeStruct((batch, d, n, LANES), F32)],
        scratch_shapes=[pltpu.VMEM((DSA_HEADS, DSA_BLOCK, 2 * DSA_BLOCK), F32)],
        compiler_params=pltpu.CompilerParams(dimension_semantics=("arbitrary",) * 3,
                                             vmem_limit_bytes=VMEM_LIMIT),
        name=f"dsa_d{d}",
    )(x, k_arg, x, k_arg, x, rel_bias, buckets)


def _out_kernel(x_ref, oa_ref, o1_ref, o4_ref, o16_ref, l1_ref, l4_ref, l16_ref,
                f1_ref, f4_ref, f16_ref, g1_ref, g4_ref, g16_ref,
                wo_ref, g2_ref, w1_ref, w2_ref, gf_ref, y_ref, osc_ref, lsc_ref, ob_ref, tsc_ref, *, ff_chunk):
    tm = x_ref.shape[0]
    npair = DSA_HEADS // 2
    lane = lax.broadcasted_iota(jnp.int32, (tm, LANES), 1)
    first_half = lane < DSA_DH

    def combine(o1_ref, o4_ref, o16_ref, l1_ref, l4_ref, l16_ref):
        n4, n16 = tm // 4, tm // 16
        for r4 in range(4):
            for r2 in range(4):
                dst = pl.ds(r4 * n4 + r2, n16, stride=4)
                tsc_ref[npair, dst, :] = l16_ref[r4 + 4 * r2]
                for hp in range(npair):
                    tsc_ref[hp, dst, :] = o16_ref[r4 + 4 * r2, :, hp * LANES:(hp + 1) * LANES].astype(F32)
        for r4 in range(4):
            dst = pl.ds(r4, n4, stride=4)
            src = slice(r4 * n4, (r4 + 1) * n4)
            lsc_ref[0, dst, :] = l4_ref[r4]
            lsc_ref[1, dst, :] = tsc_ref[npair, src, :]
            for hp in range(npair):
                osc_ref[hp, dst, :] = o4_ref[r4, :, hp * LANES:(hp + 1) * LANES].astype(F32)
                osc_ref[npair + hp, dst, :] = tsc_ref[hp, src, :]

        sts = (l1_ref[...], lsc_ref[0], lsc_ref[1])
        m = jnp.maximum(jnp.maximum(sts[0], sts[1]), sts[2])
        es = [jnp.exp2(st - m) for st in sts]
        total = sum(e * pltpu.roll(st, LANES - DSA_HEADS, axis=1) for e, st in zip(es, sts))
        inv = 1.0 / jnp.where(lane < DSA_HEADS, total, 1.0)
        ws = [e * inv for e in es]

        for hp in range(npair):
            grp = slice(hp * LANES, (hp + 1) * LANES)
            branch_o = (o1_ref[:, grp].astype(F32), osc_ref[hp], osc_ref[npair + hp])
            acc = jnp.zeros((tm, LANES), F32)
            for w, o in zip(ws, branch_o):
                wa = jnp.sum(jnp.where(lane == 2 * hp, w, 0.0), axis=-1, keepdims=True)
                wb = jnp.sum(jnp.where(lane == 2 * hp + 1, w, 0.0), axis=-1, keepdims=True)
                acc = acc + jnp.where(first_half, wa, wb) * o
            ob_ref[:, grp] = acc.astype(BF16)

    @pl.when(pl.program_id(0) == 0)
    def _():
        combine(f1_ref, f4_ref, f16_ref, g1_ref, g4_ref, g16_ref)

    mixed = _dot(oa_ref[...], wo_ref[:GLA_WIDTH, :]) + _dot(ob_ref[...], wo_ref[GLA_WIDTH:, :])
    h = x_ref[...] + mixed
    nm = (h * lax.rsqrt(jnp.mean(h * h, axis=-1, keepdims=True) + EPS) * g2_ref[...]).astype(BF16)
    ff = None
    for c in range(D_FF // ff_chunk):
        cols = slice(c * ff_chunk, (c + 1) * ff_chunk)
        a = jnp.maximum(_dot(nm, w1_ref[:, cols]), 0.0)
        d = _dot((a * a).astype(BF16), w2_ref[cols, :])
        ff = d if ff is None else ff + d
    h = h + ff
    y_ref[...] = h * lax.rsqrt(jnp.mean(h * h, axis=-1, keepdims=True) + EPS) * gf_ref[...]

    combine(o1_ref, o4_ref, o16_ref, l1_ref, l4_ref, l16_ref)


def _out(x2, oa, os_, ls, wo, g2, w1, w2, gf, *, seq, tm, ff_chunk):
    T = x2.shape[0]
    spb = seq // tm
    nt = T // tm
    nxt = lambda s: jnp.minimum(s + 1, nt - 1)
    one = pl.Buffered(1)
    row = lambda n: pl.BlockSpec((tm, n), lambda s: (s, 0))
    row_nxt = lambda n: pl.BlockSpec((tm, n), lambda s: (nxt(s), 0))
    row_first = lambda n: pl.BlockSpec((tm, n), lambda s: (0, 0), pipeline_mode=one)
    strided = lambda d, n: pl.BlockSpec((None, d, tm // d, n), lambda s: (nxt(s) // spb, 0, nxt(s) % spb, 0))
    strided_first = lambda d, n: pl.BlockSpec((None, d, tm // d, n), lambda s: (0, 0, 0, 0), pipeline_mode=one)
    const = lambda a: pl.BlockSpec(a.shape, lambda s: (0,) * a.ndim, pipeline_mode=one)
    W = DSA_WIDTH
    return pl.pallas_call(
        functools.partial(_out_kernel, ff_chunk=ff_chunk),
        grid=(nt,),
        in_specs=[row(D_MODEL), row(GLA_WIDTH), row_nxt(W), strided(4, W), strided(16, W),
                  row_nxt(LANES), strided(4, LANES), strided(16, LANES),
                  row_first(W), strided_first(4, W), strided_first(16, W),
                  row_first(LANES), strided_first(4, LANES), strided_first(16, LANES),
                  const(wo), const(g2), const(w1), const(w2), const(gf)],
        out_specs=row(D_MODEL),
        out_shape=jax.ShapeDtypeStruct((T, D_MODEL), F32),
        scratch_shapes=[pltpu.VMEM((2 * (W // LANES), tm, LANES), F32),
                        pltpu.VMEM((2, tm, LANES), F32),
                        pltpu.VMEM((tm, W), BF16),
                        pltpu.VMEM((W // LANES + 1, tm, LANES), F32)],
        compiler_params=pltpu.CompilerParams(dimension_semantics=("arbitrary",),
                                             vmem_limit_bytes=VMEM_LIMIT),
        name="out_mlp",
    )(x2, oa, *os_, *ls, *os_, *ls, wo, g2, w1, w2, gf)


_IN_SPLITS = (GLA_QK, GLA_QK, GLA_WIDTH, GLA_WIDTH, GLA_RANK, DSA_WIDTH, DSA_WIDTH, DSA_WIDTH)
_IN_OFFSETS = tuple(int(v) for v in np.cumsum((0,) + _IN_SPLITS))


def _split_w_in_kernel(wt_ref, wq_ref, wk_ref, wv_ref, wr_ref, wg_ref, wd_ref):
    o = _IN_OFFSETS
    lane = lax.broadcasted_iota(jnp.int32, (wt_ref.shape[1], RANK_PAD), 1)

    def piece(start):
        return wt_ref[start:start + LANES, :].T

    for dst, start, width, scale in ((wq_ref, o[0], GLA_QK, GLA_DK ** -0.5), (wk_ref, o[1], GLA_QK, None),
                                     (wv_ref, o[2], GLA_WIDTH, None), (wr_ref, o[3], GLA_WIDTH, None),
                                     (wd_ref, o[5], 3 * DSA_WIDTH, None)):
        for j in range(width // LANES):
            blk = piece(start + j * LANES)
            if scale is not None:
                blk = blk * scale
            if dst is wd_ref and j < DSA_WIDTH // LANES:
                blk = blk * (DSA_DH ** -0.5 * LOG2E)
            dst[:, j * LANES:(j + 1) * LANES] = blk.astype(BF16)
    wg_ref[...] = jnp.where(lane < GLA_RANK, piece(o[4]), 0.0).astype(BF16)


def _split_w_in(wt):
    _, N, K = wt.shape
    shapes = [(K, GLA_QK), (K, GLA_QK), (K, GLA_WIDTH), (K, GLA_WIDTH), (K, RANK_PAD), (K, 3 * DSA_WIDTH)]
    return pl.pallas_call(
        _split_w_in_kernel,
        grid=(1,),
        in_specs=[pl.BlockSpec((None, N, K), lambda i: (0, 0, 0))],
        out_specs=[pl.BlockSpec(s, lambda i: (0, 0)) for s in shapes],
        out_shape=[jax.ShapeDtypeStruct(s, BF16) for s in shapes],
        compiler_params=pltpu.CompilerParams(vmem_limit_bytes=VMEM_LIMIT),
        name="split_w_in",
    )(wt)


def kernel(x, attn_norm_g, w_in, gla_gate_w2, gla_gate_b, gla_norm_g, rel_bias, w_out, mlp_norm_g,
           w_ff1, w_ff2, final_norm_g):
    batch, seq, _ = x.shape
    assert tuple(d for _, d in DSA_PATTERN) == (1, 4, 16)
    assert seq % (DSA_PATTERN[-1][1] * DSA_BLOCK) == 0
    T = batch * seq
    x2 = x.reshape(T, D_MODEL)

    wq, wk, wv, wr, wg, wd = _split_w_in(jnp.swapaxes(w_in, 1, 2))
    w2 = jnp.pad(gla_gate_w2[0], ((0, RANK_PAD - GLA_RANK), (0, 0))).astype(BF16)
    gb = gla_gate_b[0].astype(F32).reshape(1, GLA_QK)

    gq, gv, gr, d1, kt, glogt, d4, d16, kt1, kt4 = _proj(
        x2, attn_norm_g[0].reshape(1, D_MODEL).astype(F32), wq, wv, wr, wk, wg, w2, gb, wd,
        batch=batch, seq=seq, tm=1024)

    o_a = _gla(gq, kt, gv, gr, glogt, gla_norm_g[0].reshape(1, GLA_WIDTH).astype(F32),
               batch=batch, seq=seq, tg=2048)

    rel_bias = rel_bias.astype(F32)
    buckets = jnp.asarray(_bucket_tables())
    os_, ls = [], []
    branches = ((d1.reshape(batch, 1, seq, 2 * DSA_WIDTH), kt1), (d4, kt4), (d16, None))
    for branch, (xd, ktd) in enumerate(branches):
        o, stats = _dsa_branch(xd, ktd, rel_bias, buckets, branch, rows_per_step=2048)
        os_.append(o)
        ls.append(stats)
    os_[0] = os_[0].reshape(T, DSA_WIDTH)
    ls[0] = ls[0].reshape(T, LANES)

    y = _out(x2, o_a, os_, ls, w_out[0].astype(BF16), mlp_norm_g[0].reshape(1, D_MODEL).astype(F32),
             w_ff1[0].astype(BF16), w_ff2[0].astype(BF16), final_norm_g.reshape(1, D_MODEL).astype(F32),
             seq=seq, tm=512, ff_chunk=2048)
    return y.reshape(batch, seq, D_MODEL)
```
